```python
import jax, jax.numpy as jnp
from jax import lax
import numpy as np

D_MODEL = 1024
BATCH = 4
SEQ = 4096
DEPTH = 2
DEC_BATCH = 128
DEC_SEQ = 8
PAST_LEN = 2048
PAGE_SIZE = 128

N_EVEN = (DEPTH + 1) // 2
N_ODD = DEPTH // 2
N_HEADS = 8
HEAD_DIM = 64
N_KV_HEADS = 2
KV_GROUP = N_HEADS // N_KV_HEADS
D_ATTN = N_HEADS * HEAD_DIM
D_KV = N_KV_HEADS * HEAD_DIM
N_IDX_HEADS = 8
D_IDX = 64
TOPK_MAX = 256
Q_BLOCK = 128
ROPE_THETA = 10000.0
POOL_WINDOWS = (2, 4, 8, 16)
N_POOL_GROUPS = 4
D_POOL = D_MODEL // 2
D_POOL_GROUP = D_POOL // N_POOL_GROUPS
POOL_BUF = 15
D_IN_EVEN = D_ATTN + 2 * D_KV + N_IDX_HEADS * D_IDX + D_IDX + N_IDX_HEADS + D_POOL
D_OUT_EVEN = D_ATTN + D_POOL
D_SGU = D_MODEL
N_SGU_HEADS = 8
D_SGU_HEAD = D_SGU // N_SGU_HEADS
CHUNK = 128
D_FF = -(-8 * D_MODEL // (3 * 256)) * 256
EPS = 1e-6

kernel_name = 'hybrid_dsa_pool_gmlp_decode_step'


def _rmsnorm(x, g):
    x32 = x.astype(jnp.float32)
    y = x32 * lax.rsqrt(jnp.mean(x32 * x32, axis=-1, keepdims=True) + EPS)
    return (y * g.astype(jnp.float32)).astype(x.dtype)


def _layernorm(x, g, b):
    x32 = x.astype(jnp.float32)
    mu = jnp.mean(x32, axis=-1, keepdims=True)
    xc = x32 - mu
    y = xc * lax.rsqrt(jnp.mean(xc * xc, axis=-1, keepdims=True) + EPS)
    return (y * g.astype(jnp.float32) + b.astype(jnp.float32)).astype(x.dtype)


def _rope(x, pos):
    half = x.shape[-1] // 2
    inv = ROPE_THETA ** (-jnp.arange(half, dtype=jnp.float32) / half)
    ang = pos.astype(jnp.float32)[:, None] * inv[None, :]
    cos = jnp.cos(ang)[None, :, None, :]
    sin = jnp.sin(ang)[None, :, None, :]
    x32 = x.astype(jnp.float32)
    x1, x2 = x32[..., :half], x32[..., half:]
    return jnp.concatenate([x1 * cos - x2 * sin, x2 * cos + x1 * sin], axis=-1).astype(x.dtype)


def _dsa_block(q, iq, iw, qpos, k_all, v_all, ik_all, topk):
    b, n = q.shape[:2]
    L = k_all.shape[1]
    rel = jax.nn.relu(jnp.einsum('bnhd,bsd->bnhs', iq.astype(jnp.float32), ik_all.astype(jnp.float32)) * D_IDX ** -0.5)
    score = jnp.einsum('bnh,bnhs->bns', iw.astype(jnp.float32) * N_IDX_HEADS ** -0.5, rel)
    kpos = jnp.arange(L, dtype=jnp.int32)
    score = jnp.where((kpos[None, :] <= qpos[:, None])[None], score, -jnp.inf)
    _, sel = lax.top_k(score, topk)
    take = jax.vmap(lambda a, i: a[i])
    k_sel = take(k_all, sel)
    v_sel = take(v_all, sel)
    valid = sel <= qpos[None, :, None]
    qg = q.reshape(b, n, N_KV_HEADS, KV_GROUP, HEAD_DIM).astype(jnp.float32)
    logits = jnp.einsum('bnhgd,bnshd->bnhgs', qg, k_sel.astype(jnp.float32)) * HEAD_DIM ** -0.5
    logits = jnp.where(valid[:, :, None, None, :], logits, -jnp.inf)
    p = jax.nn.softmax(logits, axis=-1)
    o = jnp.einsum('bnhgs,bnshd->bnhgd', p, v_sel.astype(jnp.float32))
    return o.reshape(b, n, D_ATTN).astype(q.dtype)


def _dsa_attend(q, iq, iw, qpos, k_all, v_all, ik_all, topk):
    b, T = q.shape[:2]
    blk = min(Q_BLOCK, T)
    nb = T // blk

    def to_blocks(a):
        return jnp.moveaxis(a.reshape((b, nb, blk) + a.shape[2:]), 1, 0)

    out = lax.map(lambda args: _dsa_block(*args, k_all, v_all, ik_all, topk),
                  (to_blocks(q), to_blocks(iq), to_blocks(iw), qpos.reshape(nb, blk)))
    return jnp.moveaxis(out, 0, 1).reshape(b, T, D_ATTN)


def _pool_mix(xp, prefix, pos, w_pg, pool_scale):
    b, T, _ = xp.shape
    ext = jnp.concatenate([prefix.astype(xp.dtype), xp], axis=1).astype(jnp.float32)
    cs = jnp.concatenate([jnp.zeros((b, 1, D_POOL), jnp.float32), jnp.cumsum(ext, axis=1)], axis=1)
    groups = []
    for g, w in enumerate(POOL_WINDOWS):
        sl = slice(g * D_POOL_GROUP, (g + 1) * D_POOL_GROUP)
        win_sum = cs[:, POOL_BUF + 1:POOL_BUF + 1 + T, sl] - cs[:, POOL_BUF + 1 - w:POOL_BUF + 1 - w + T, sl]
        count = jnp.minimum(w, pos + 1).astype(jnp.float32)[None, :, None]
        groups.append(win_sum / count - ext[:, POOL_BUF:, sl])
    d = jnp.stack(groups, axis=2).astype(xp.dtype)
    y = jnp.einsum('btgc,gce->btge', d, w_pg).reshape(b, T, D_POOL) * pool_scale
    return y, ext[:, T:].astype(xp.dtype)


def _even_mixer(h, pos, topk, past, w_in, w_out, w_pg, pool_scale):
    b, T, _ = h.shape
    widths = [D_ATTN, D_KV, D_KV, N_IDX_HEADS * D_IDX, D_IDX, N_IDX_HEADS]
    cuts, acc = [], 0
    for wd in widths:
        acc += wd
        cuts.append(acc)
    q, k, v, iq, ik, iw, xp = jnp.split(h @ w_in, cuts, axis=-1)
    q = _rope(q.reshape(b, T, N_HEADS, HEAD_DIM), pos)
    k = _rope(k.reshape(b, T, N_KV_HEADS, HEAD_DIM), pos)
    v = v.reshape(b, T, N_KV_HEADS, HEAD_DIM)
    iq = _rope(iq.reshape(b, T, N_IDX_HEADS, D_IDX), pos)
    ik = _rope(ik.reshape(b, T, 1, D_IDX), pos)[:, :, 0]
    if past is None:
        k_all, v_all, ik_all = k, v, ik
        prefix = jnp.zeros((b, POOL_BUF, D_POOL), h.dtype)
    else:
        k_past, v_past, ik_past, prefix = past
        k_all = jnp.concatenate([k_past.astype(k.dtype), k], axis=1)
        v_all = jnp.concatenate([v_past.astype(v.dtype), v], axis=1)
        ik_all = jnp.concatenate([ik_past.astype(ik.dtype), ik], axis=1)
    attn = _dsa_attend(q, iq, iw, pos, k_all, v_all, ik_all, topk)
    pool, new_prefix = _pool_mix(xp, prefix, pos, w_pg, pool_scale)
    out = jnp.concatenate([attn, pool], axis=-1) @ w_out
    return out, k, v, ik, new_prefix


def _odd_mixer(h, w_in, ln_g, ln_b, w_s, b_s, w_out):
    b, T, _ = h.shape
    u, v = jnp.split(jax.nn.gelu(h @ w_in, approximate=False), 2, axis=-1)
    z = _layernorm(v, ln_g, ln_b)
    c = min(CHUNK, T)
    nc = T // c
    mask = jnp.tril(jnp.ones((c, c), dtype=bool))
    w_m = jnp.where(mask[None], w_s[:, :c, :c], 0.0)
    zc = z.reshape(b, nc, c, N_SGU_HEADS, D_SGU_HEAD)
    s = jnp.einsum('gts,bnsgd->bntgd', w_m, zc) + jnp.swapaxes(b_s[:, :c], 0, 1)[None, None, :, :, None]
    y = u * s.reshape(b, T, D_SGU)
    return y @ w_out, z


def _swiglu(h, w_gu, w_down):
    gate, up = jnp.split(h @ w_gu, 2, axis=-1)
    return (jax.nn.silu(gate) * up) @ w_down


def _gather_pages(pool, page_table):
    g = pool[page_table]
    return g.reshape((g.shape[0], g.shape[1] * g.shape[2]) + g.shape[3:])


def setup_inputs(seed: int = 0) -> dict:
    key = jax.random.key(seed)
    ks = jax.random.split(key, 32)
    f32 = jnp.float32
    n_pages = PAST_LEN // PAGE_SIZE
    n_pool = (DEC_BATCH * n_pages * 5) // 4

    def nrm(k, shape, scale=1.0):
        return scale * jax.random.normal(k, shape, f32)

    def gain(k, shape):
        return 1.0 + 0.05 * jax.random.normal(k, shape, f32)

    page_table = jax.random.permutation(ks[6], n_pool)[:DEC_BATCH * n_pages].reshape(DEC_BATCH, n_pages).astype(jnp.int32)
    return {
        'x_prompt': nrm(ks[0], (BATCH, SEQ, D_MODEL)),
        'x_sample': nrm(ks[1], (DEC_BATCH, DEC_SEQ, D_MODEL)),
        'cache_k': nrm(ks[2], (N_EVEN, n_pool, PAGE_SIZE, N_KV_HEADS, HEAD_DIM)),
        'cache_v': nrm(ks[3], (N_EVEN, n_pool, PAGE_SIZE, N_KV_HEADS, HEAD_DIM)),
        'cache_idx_k': nrm(ks[4], (N_EVEN, n_pool, PAGE_SIZE, D_IDX)),
        'state_pool': nrm(ks[5], (N_EVEN, DEC_BATCH, POOL_BUF, D_POOL)),
        'page_table': page_table,
        'norm_mix_pre': gain(ks[7], (DEPTH, D_MODEL)),
        'norm_mix_post': gain(ks[8], (DEPTH, D_MODEL)),
        'norm_ffn_pre': gain(ks[9], (DEPTH, D_MODEL)),
        'norm_ffn_post': gain(ks[10], (DEPTH, D_MODEL)),
        'w_in_even': nrm(ks[11], (N_EVEN, D_MODEL, D_IN_EVEN), D_MODEL ** -0.5),
        'w_out_even': nrm(ks[12], (N_EVEN, D_OUT_EVEN, D_MODEL), D_OUT_EVEN ** -0.5),
        'w_pool_group': nrm(ks[13], (N_EVEN, N_POOL_GROUPS, D_POOL_GROUP, D_POOL_GROUP), D_POOL_GROUP ** -0.5),
        'pool_scale': gain(ks[14], (N_EVEN, D_POOL)),
        'w_in_odd': nrm(ks[15], (N_ODD, D_MODEL, 2 * D_SGU), D_MODEL ** -0.5),
        'sgu_norm_g': gain(ks[16], (N_ODD, D_SGU)),
        'sgu_norm_b': nrm(ks[17], (N_ODD, D_SGU), 0.02),
        'w_spatial': nrm(ks[18], (N_ODD, N_SGU_HEADS, CHUNK, CHUNK), CHUNK ** -0.5),
        'b_spatial': gain(ks[19], (N_ODD, N_SGU_HEADS, CHUNK)),
        'w_out_odd': nrm(ks[20], (N_ODD, D_SGU, D_MODEL), D_SGU ** -0.5),
        'w_ffn_gate_up': nrm(ks[21], (DEPTH, D_MODEL, 2 * D_FF), D_MODEL ** -0.5),
        'w_ffn_down': nrm(ks[22], (DEPTH, D_FF, D_MODEL), D_FF ** -0.5),
    }


def reference(x_prompt, x_sample, cache_k, cache_v, cache_idx_k, state_pool, page_table,
              norm_mix_pre, norm_mix_post, norm_ffn_pre, norm_ffn_post,
              w_in_even, w_out_even, w_pool_group, pool_scale,
              w_in_odd, sgu_norm_g, sgu_norm_b, w_spatial, b_spatial, w_out_odd,
              w_ffn_gate_up, w_ffn_down):
    t_prompt = x_prompt.shape[1]
    t_sample = x_sample.shape[1]
    past_len = page_table.shape[1] * PAGE_SIZE
    pos_p = jnp.arange(t_prompt, dtype=jnp.int32)
    pos_s = past_len + jnp.arange(t_sample, dtype=jnp.int32)
    topk_p = min(TOPK_MAX, t_prompt // 4)
    topk_s = min(TOPK_MAX, (past_len + t_sample) // 4)

    hp, hs = x_prompt, x_sample
    nk_p, nv_p, nik_p, npool_p = [], [], [], []
    nk_s, nv_s, nik_s, npool_s, nsgu_s = [], [], [], [], []
    for layer in range(DEPTH):
        li = layer // 2
        ap = _rmsnorm(hp, norm_mix_pre[layer])
        as_ = _rmsnorm(hs, norm_mix_pre[layer])
        if layer % 2 == 0:
            mp, kp, vp, ikp, poolp = _even_mixer(ap, pos_p, topk_p, None,
                                                 w_in_even[li], w_out_even[li], w_pool_group[li], pool_scale[li])
            past = (_gather_pages(cache_k[li], page_table), _gather_pages(cache_v[li], page_table),
                    _gather_pages(cache_idx_k[li], page_table), state_pool[li])
            ms, ks_, vs, iks, pools = _even_mixer(as_, pos_s, topk_s, past,
                                                  w_in_even[li], w_out_even[li], w_pool_group[li], pool_scale[li])
            nk_p.append(kp); nv_p.append(vp); nik_p.append(ikp); npool_p.append(poolp)
            nk_s.append(ks_); nv_s.append(vs); nik_s.append(iks); npool_s.append(pools)
        else:
            mp, _ = _odd_mixer(ap, w_in_odd[li], sgu_norm_g[li], sgu_norm_b[li], w_spatial[li], b_spatial[li], w_out_odd[li])
            ms, zs = _odd_mixer(as_, w_in_odd[li], sgu_norm_g[li], sgu_norm_b[li], w_spatial[li], b_spatial[li], w_out_odd[li])
            nsgu_s.append(zs)
        hp = hp + _rmsnorm(mp, norm_mix_post[layer])
        hs = hs + _rmsnorm(ms, norm_mix_post[layer])
        hp = hp + _rmsnorm(_swiglu(_rmsnorm(hp, norm_ffn_pre[layer]), w_ffn_gate_up[layer], w_ffn_down[layer]), norm_ffn_post[layer])
        hs = hs + _rmsnorm(_swiglu(_rmsnorm(hs, norm_ffn_pre[layer]), w_ffn_gate_up[layer], w_ffn_down[layer]), norm_ffn_post[layer])

    return (hp, hs,
            jnp.stack(nk_p), jnp.stack(nv_p), jnp.stack(nik_p), jnp.stack(npool_p),
            jnp.stack(nk_s), jnp.stack(nv_s), jnp.stack(nik_s), jnp.stack(npool_s),
            jnp.stack(nsgu_s))
```

```python
import functools

import jax
import jax.numpy as jnp
import numpy as np
from jax import lax
from jax.experimental import pallas as pl
from jax.experimental.pallas import tpu as pltpu

EPS = 1e-6
N_HEADS = 8
HEAD_DIM = 64
N_KV_HEADS = 2
KV_GROUP = N_HEADS // N_KV_HEADS
D_ATTN = N_HEADS * HEAD_DIM
D_KV = N_KV_HEADS * HEAD_DIM
N_IDX_HEADS = 8
D_IDX = 64
TOPK_MAX = 256
Q_BLOCK = 128
ROPE_THETA = 10000.0
POOL_WINDOWS = (2, 4, 8, 16)
POOL_BUF = 15
PAGE_SIZE = 128
CHUNK = 128
N_SGU_HEADS = 8

LANES = 128
SUBLANES = 8
VMEM_LIMIT = 56 * 1024 * 1024
INT_MIN = -2 ** 31
NEG_BIG = -1e30

F32 = jnp.float32
BF16 = jnp.bfloat16
I32 = jnp.int32

_NT = (((1,), (1,)), ((), ()))


def _cparams(sem):
    return pltpu.CompilerParams(dimension_semantics=sem, vmem_limit_bytes=VMEM_LIMIT)


def _rms(x, g):
    return x * lax.rsqrt(jnp.mean(x * x, axis=-1, keepdims=True) + EPS) * g


def _dot(a, b):
    return jnp.dot(a, b, preferred_element_type=F32)


def _gelu(x):
    return 0.5 * x * (1.0 + lax.erf(x * (2.0 ** -0.5)))


C_Q, C_K, C_V, C_IQ, C_IK2, C_XP, C_END = 0, 512, 640, 768, 1280, 1536, 2048
D_POOL_GROUP = 128


def _rope_tile(t, cos, sin, first_half):
    partner = jnp.where(first_half, pltpu.roll(t, 96, 1), pltpu.roll(t, 32, 1))
    return t * cos + partner * sin


def _even_in_kernel(*refs, tm, decode, past_len):
    if decode:
        (x_ref, g_ref, w_ref, wvt_ref, wiwt_ref, cos_ref, sin_ref, wpg_ref, psc_ref, pre_ref,
         q_ref, k_ref, kb_ref, v_ref, vt_ref, iq_ref, ik2_ref, ik_ref, iwt_ref, pool_ref, xp_ref,
         ext_s) = refs
    else:
        (x_ref, g_ref, w_ref, wvt_ref, wiwt_ref, cos_ref, sin_ref, wpg_ref, psc_ref,
         q_ref, k_ref, kb_ref, v_ref, vt_ref, iq_ref, ik2_ref, ik_ref, iwt_ref, pool_ref, xp_ref,
         ext_s) = refs
    j = pl.program_id(1)
    xn = _rms(x_ref[...], g_ref[...]).astype(BF16)
    cos = cos_ref[...]
    sin = sin_ref[...]
    lane = lax.broadcasted_iota(I32, (1, LANES), 1)
    first_half = (lane % HEAD_DIM) < (HEAD_DIM // 2)

    def proj_rope(c0, c1):
        t = _dot(xn, w_ref[:, c0:c1])
        return [_rope_tile(t[:, c:c + LANES], cos, sin, first_half) for c in range(0, c1 - c0, LANES)]

    for half in range(2):
        tiles = proj_rope(C_Q + 256 * half, C_Q + 256 * (half + 1))
        for c, t in enumerate(tiles):
            col = 256 * half + LANES * c
            q_ref[:, col:col + LANES] = (t * (HEAD_DIM ** -0.5)).astype(BF16)
    kt = proj_rope(C_K, C_V)[0]
    k_ref[...] = kt
    kb_ref[...] = kt.astype(BF16)
    v_ref[...] = _dot(xn, w_ref[:, C_V:C_IQ])
    vt_ref[0] = lax.dot_general(wvt_ref[...], xn, _NT, preferred_element_type=F32).astype(BF16)
    for half in range(2):
        tiles = proj_rope(C_IQ + 256 * half, C_IQ + 256 * (half + 1))
        for c, t in enumerate(tiles):
            col = 256 * half + LANES * c
            iq_ref[:, col:col + LANES] = t.astype(BF16)
    tiles = proj_rope(C_IK2, C_XP)
    ik2_ref[:, 0:LANES] = tiles[0].astype(BF16)
    ik2_ref[:, LANES:2 * LANES] = tiles[1].astype(BF16)
    ik_ref[...] = tiles[0][:, 0:D_IDX]
    iwt = lax.dot_general(wiwt_ref[...], xn, _NT, preferred_element_type=F32)
    iwt_ref[0] = iwt * (N_IDX_HEADS ** -0.5) * (D_IDX ** -0.5)

    xp = _dot(xn, w_ref[:, C_XP:C_END])
    row = lax.broadcasted_iota(I32, (tm, 1), 0)
    if decode:
        ns = tm // SUBLANES
        ext_s[:, 0:16, :] = pre_ref[...]
        ext_s[:, 16:24, :] = xp.reshape(ns, SUBLANES, 4 * D_POOL_GROUP)
        pos = past_len + (row % SUBLANES)
    else:
        @pl.when(j == 0)
        def _():
            ext_s[0:16, :] = jnp.zeros((16, 4 * D_POOL_GROUP), F32)
        ext_s[16:16 + tm, :] = xp
        pos = j * tm + row
    for g, w in enumerate(POOL_WINDOWS):
        cs = slice(g * D_POOL_GROUP, (g + 1) * D_POOL_GROUP)
        tok = xp[:, cs]
        acc = tok
        for i in range(1, w):
            if decode:
                acc = acc + ext_s[:, 16 - i:24 - i, cs].reshape(tm, D_POOL_GROUP)
            else:
                acc = acc + ext_s[16 - i:16 - i + tm, cs]
        cnt = jnp.minimum(w, pos + 1).astype(F32)
        d = acc / cnt - tok
        y = _dot(d.astype(BF16), wpg_ref[g]) * psc_ref[:, cs]
        pool_ref[:, cs] = y.astype(BF16)
    if decode:
        xp_ref[...] = xp
    else:
        ext_s[0:16, :] = xp[tm - 16:tm, :]
        xp_ref[0] = xp[tm - 16:tm, :]


def _rope_tables(pos):
    half = HEAD_DIM // 2
    inv = ROPE_THETA ** (-jnp.arange(half, dtype=F32) / half)
    ang = pos.astype(F32)[:, None] * inv[None, :]
    cos32, sin32 = jnp.cos(ang), jnp.sin(ang)
    cos = jnp.tile(cos32, (1, LANES // half))
    sin = jnp.tile(jnp.concatenate([-sin32, sin32], axis=1), (1, LANES // HEAD_DIM))
    return cos, sin


def _arrange_w_in_even(w):
    q = w[:, 0:512].reshape(-1, N_HEADS, HEAD_DIM)
    q = jnp.stack([q[:, 0:4], q[:, 4:8]], axis=2).reshape(-1, 512)
    k = w[:, 512:640]
    v = w[:, 640:768]
    iq = w[:, 768:1280]
    ik = w[:, 1280:1344]
    iw = w[:, 1344:1352]
    xp = w[:, 1352:1864]
    z = jnp.zeros_like(ik)
    main = jnp.concatenate([q, k, v, iq, ik, z, z, ik, xp], axis=1).astype(BF16)
    return main, v.T.astype(BF16), iw.T.astype(BF16)


def _even_in(x, g, wmain, wvt, wiwt, cos, sin, wpg, psc, prefix, *, nb, tm, past_len):
    n, d = x.shape
    t = n // nb
    nt = t // tm
    decode = prefix is not None
    row = lambda b, j: (b * nt + j, 0)
    const = lambda b, j: (0, 0)
    in_specs = [
        pl.BlockSpec((tm, d), row),
        pl.BlockSpec((1, d), const),
        pl.BlockSpec(wmain.shape, const),
        pl.BlockSpec(wvt.shape, const),
        pl.BlockSpec(wiwt.shape, const),
        pl.BlockSpec((tm, LANES), row),
        pl.BlockSpec((tm, LANES), row),
        pl.BlockSpec(wpg.shape, lambda b, j: (0, 0, 0)),
        pl.BlockSpec((1, 512), const),
    ]
    args = [x, g, wmain, wvt, wiwt, cos, sin, wpg, psc]
    if decode:
        ns = tm // SUBLANES
        in_specs.append(pl.BlockSpec((ns, 16, 512), lambda b, j: (b * nt + j, 0, 0)))
        args.append(prefix)
        xp_shape = jax.ShapeDtypeStruct((n, 512), F32)
        xp_spec = pl.BlockSpec((tm, 512), row)
        scratch = [pltpu.VMEM((ns, 24, 512), F32)]
    else:
        xp_shape = jax.ShapeDtypeStruct((nb, 16, 512), F32)
        xp_spec = pl.BlockSpec((1, 16, 512), lambda b, j: (b, 0, 0))
        scratch = [pltpu.VMEM((tm + 16, 512), F32)]
    out_shape = [
        jax.ShapeDtypeStruct((n, 512), BF16),
        jax.ShapeDtypeStruct((n, 128), F32),
        jax.ShapeDtypeStruct((n, 128), BF16),
        jax.ShapeDtypeStruct((n, 128), F32),
        jax.ShapeDtypeStruct((nb, 128, t), BF16),
        jax.ShapeDtypeStruct((n, 512), BF16),
        jax.ShapeDtypeStruct((n, 256), BF16),
        jax.ShapeDtypeStruct((n, D_IDX), F32),
        jax.ShapeDtypeStruct((nb, 8, t), F32),
        jax.ShapeDtypeStruct((n, 512), BF16),
        xp_shape,
    ]
    out_specs = [
        pl.BlockSpec((tm, 512), row),
        pl.BlockSpec((tm, 128), row),
        pl.BlockSpec((tm, 128), row),
        pl.BlockSpec((tm, 128), row),
        pl.BlockSpec((1, 128, tm), lambda b, j: (b, 0, j)),
        pl.BlockSpec((tm, 512), row),
        pl.BlockSpec((tm, 256), row),
        pl.BlockSpec((tm, D_IDX), row),
        pl.BlockSpec((1, 8, tm), lambda b, j: (b, 0, j)),
        pl.BlockSpec((tm, 512), row),
        xp_spec,
    ]
    return pl.pallas_call(
        functools.partial(_even_in_kernel, tm=tm, decode=decode, past_len=past_len),
        grid=(nb, nt), in_specs=in_specs, out_specs=out_specs, out_shape=out_shape,
        scratch_shapes=scratch, compiler_params=_cparams(("arbitrary", "arbitrary")),
        name="even_in_decode" if decode else "even_in_prompt",
    )(*args)


def _sort_key(s):
    s = jnp.where(s == 0.0, 0.0, s)
    b = lax.bitcast_convert_type(s, I32)
    return b ^ ((b >> 31) & 0x7FFFFFFF)


def _colsum(m):
    return m.reshape(m.shape[0] // SUBLANES, SUBLANES, LANES).sum(axis=0)


def _topk_select_params(keys_ref, n_tiles, tk, topk):
    def count(pred):
        def body(t, acc):
            r0 = pl.multiple_of(t * tk, tk)
            kk = keys_ref[pl.ds(r0, tk), :]
            idx = r0 + lax.broadcasted_iota(I32, (tk, 1), 0)
            return acc + _colsum(pred(kk, idx).astype(I32))
        acc = lax.fori_loop(0, n_tiles, body, jnp.zeros((SUBLANES, LANES), I32))
        return acc.sum(axis=0, keepdims=True)

    def bit_step(i, prefix):
        cand = prefix + jnp.left_shift(jnp.int32(1), 31 - i)
        c = count(lambda kk, idx: kk >= cand)
        return jnp.where(c >= topk, cand, prefix)

    thr = lax.fori_loop(0, 32, bit_step, jnp.full((1, LANES), INT_MIN, I32))
    c_gt = count(lambda kk, idx: kk > thr)
    c_eq = count(lambda kk, idx: kk == thr)
    need = topk - c_gt
    ambiguous = jnp.logical_and(c_eq > need, thr != INT_MIN)

    def tie_search():
        nbits = max(1, int(np.ceil(np.log2(keys_ref.shape[0]))))
        def step(i, lo):
            cand = lo + jnp.left_shift(jnp.int32(1), nbits - 1 - i)
            c = count(lambda kk, idx: jnp.logical_and(kk == thr, idx < cand))
            return jnp.where(c < need, cand, lo)
        return lax.fori_loop(0, nbits, step, jnp.zeros((1, LANES), I32))

    big = jnp.full((1, LANES), 2 ** 30, I32)
    any_amb = jnp.max(ambiguous.astype(I32)) > 0
    cut = lax.cond(any_amb, lambda: jnp.where(ambiguous, tie_search(), big), lambda: big)
    return thr, cut


def _attend_tile(sel, k_tile, vt_tile, qm_refs, m_s, l_s, acc_s):
    for g in range(N_KV_HEADS):
        qm = qm_refs[g][...]
        nq = qm.shape[0]
        st = lax.dot_general(k_tile, qm, _NT, preferred_element_type=F32)
        selg = jnp.concatenate([sel] * (nq // LANES), axis=1) if nq > LANES else sel
        m_old = m_s[g]
        m_new = jnp.maximum(m_old, jnp.max(jnp.where(selg, st, NEG_BIG), axis=0, keepdims=True))
        p = jnp.where(selg, jnp.exp(st - m_new), 0.0)
        alpha = jnp.exp(m_old - m_new)
        l_s[g] = alpha * l_s[g] + jnp.sum(p, axis=0, keepdims=True)
        m_s[g] = m_new
        vt = vt_tile[g * HEAD_DIM:(g + 1) * HEAD_DIM, :]
        acc_s[g] = acc_s[g] * alpha + _dot(vt, p.astype(BF16))


def _dsa_prompt_kernel(q_ref, iq_ref, iwt_ref, kb_ref, vt_ref, ik2_ref, o_ref,
                       keys_s, qm0_s, qm1_s, iqs_s, m_s, l_s, acc_s, *, tk, topk):
    i = pl.program_id(1)
    n_tiles = (i * Q_BLOCK + Q_BLOCK + tk - 1) // tk
    qpos = i * Q_BLOCK + lax.broadcasted_iota(I32, (1, LANES), 1)
    lane = lax.broadcasted_iota(I32, (1, LANES), 1)
    lo_half = lane < HEAD_DIM

    for jj in range(4):
        qt = q_ref[:, jj * LANES:(jj + 1) * LANES]
        qm0_s[jj * Q_BLOCK:(jj + 1) * Q_BLOCK, :] = jnp.where(lo_half, qt, jnp.zeros_like(qt))
        qm1_s[jj * Q_BLOCK:(jj + 1) * Q_BLOCK, :] = jnp.where(lo_half, jnp.zeros_like(qt), qt)
        iqs_s[jj * Q_BLOCK:(jj + 1) * Q_BLOCK, :] = iq_ref[:, jj * LANES:(jj + 1) * LANES]
    w = iwt_ref[0]

    def score_tile(t, carry):
        r0 = pl.multiple_of(t * tk, tk)
        ik2 = ik2_ref[pl.ds(r0, tk), :]
        iqs = iqs_s[...]
        rel_e = lax.dot_general(ik2[:, 0:LANES], iqs, _NT, preferred_element_type=F32)
        rel_o = lax.dot_general(ik2[:, LANES:2 * LANES], iqs, _NT, preferred_element_type=F32)
        s = jnp.zeros((tk, LANES), F32)
        for jj in range(4):
            cs = slice(jj * LANES, (jj + 1) * LANES)
            s = s + w[2 * jj:2 * jj + 1, :] * jnp.maximum(rel_e[:, cs], 0.0)
            s = s + w[2 * jj + 1:2 * jj + 2, :] * jnp.maximum(rel_o[:, cs], 0.0)
        kpos = r0 + lax.broadcasted_iota(I32, (tk, 1), 0)
        keys_s[pl.ds(r0, tk), :] = jnp.where(kpos <= qpos, _sort_key(s), INT_MIN)
        return carry

    lax.fori_loop(0, n_tiles, score_tile, 0)
    thr, cut = _topk_select_params(keys_s, n_tiles, tk, topk)

    m_s[...] = jnp.full(m_s.shape, NEG_BIG, F32)
    l_s[...] = jnp.zeros(l_s.shape, F32)
    acc_s[...] = jnp.zeros(acc_s.shape, F32)

    def attn_tile(t, carry):
        r0 = pl.multiple_of(t * tk, tk)
        kk = keys_s[pl.ds(r0, tk), :]
        kpos = r0 + lax.broadcasted_iota(I32, (tk, 1), 0)
        sel = jnp.logical_or(kk > thr, jnp.logical_and(kk == thr, kpos <= cut))
        sel = jnp.logical_and(sel, kpos <= qpos)
        _attend_tile(sel, kb_ref[pl.ds(r0, tk), :], vt_ref[0, :, pl.ds(r0, tk)],
                     (qm0_s, qm1_s), m_s, l_s, acc_s)
        return carry

    lax.fori_loop(0, n_tiles, attn_tile, 0)
    heads = []
    for g in range(N_KV_HEADS):
        og = acc_s[g] / l_s[g]
        heads += [og[:, jj * LANES:(jj + 1) * LANES] for jj in range(4)]
    o_ref[...] = jnp.concatenate(heads, axis=0).T.astype(BF16)


def _dsa_prompt(q, iq, iwt, kb, vt, ik2, *, nb, tk, topk):
    n = q.shape[0]
    t = n // nb
    nq = t // Q_BLOCK
    blk = lambda b, i: (b * nq + i, 0)
    seq = lambda b, i: (b, 0)
    return pl.pallas_call(
        functools.partial(_dsa_prompt_kernel, tk=tk, topk=topk),
        grid=(nb, nq),
        in_specs=[
            pl.BlockSpec((Q_BLOCK, 512), blk),
            pl.BlockSpec((Q_BLOCK, 512), blk),
            pl.BlockSpec((1, 8, Q_BLOCK), lambda b, i: (b, 0, i)),
            pl.BlockSpec((t, 128), seq),
            pl.BlockSpec((1, 128, t), lambda b, i: (b, 0, 0)),
            pl.BlockSpec((t, 256), seq),
        ],
        out_specs=pl.BlockSpec((Q_BLOCK, 512), blk),
        out_shape=jax.ShapeDtypeStruct((n, 512), BF16),
        scratch_shapes=[
            pltpu.VMEM((t, LANES), I32),
            pltpu.VMEM((4 * Q_BLOCK, LANES), BF16),
            pltpu.VMEM((4 * Q_BLOCK, LANES), BF16),
            pltpu.VMEM((4 * Q_BLOCK, LANES), BF16),
            pltpu.VMEM((N_KV_HEADS, 1, 4 * Q_BLOCK), F32),
            pltpu.VMEM((N_KV_HEADS, 1, 4 * Q_BLOCK), F32),
            pltpu.VMEM((N_KV_HEADS, HEAD_DIM, 4 * Q_BLOCK), F32),
        ],
        compiler_params=_cparams(("arbitrary", "arbitrary")),
        name="dsa_prompt",
    )(q, iq, iwt, kb, vt, ik2)


def _dsa_decode_kernel(*refs, n_pages, past_len, topk):
    pt_ref = refs[0]
    del pt_ref
    kp = refs[1:1 + n_pages]
    vp = refs[1 + n_pages:1 + 2 * n_pages]
    ip = refs[1 + 2 * n_pages:1 + 3 * n_pages]
    (kn_ref, vn_ref, in_ref, qd_ref, iqd_ref, wd_ref, o_ref,
     keys_s, kc_s, vtc_s, m_s, l_s, acc_s) = refs[1 + 3 * n_pages:]
    tk = PAGE_SIZE
    n_tiles = n_pages + 1
    lane = lax.broadcasted_iota(I32, (1, LANES), 1)
    qpos = past_len + (lane % SUBLANES)
    iqd = iqd_ref[0]
    w = wd_ref[0]
    nnew = kn_ref.shape[1]
    pad = jnp.zeros((tk - nnew, LANES), F32)

    def score(ik_tile, r0):
        rel = lax.dot_general(ik_tile.astype(BF16), iqd, _NT, preferred_element_type=F32)
        x = w * jnp.maximum(rel, 0.0)
        s = x
        for h in range(1, N_IDX_HEADS):
            s = s + pltpu.roll(x, SUBLANES * h, 1)
        kpos = r0 + lax.broadcasted_iota(I32, (tk, 1), 0)
        keys_s[r0:r0 + tk, :] = jnp.where(kpos <= qpos, _sort_key(s), INT_MIN)

    for p in range(n_pages):
        r0 = p * tk
        score(ip[p][0], r0)
        kc_s[r0:r0 + tk, :] = kp[p][0].astype(BF16)
        vtc_s[:, r0:r0 + tk] = vp[p][0].T.astype(BF16)
    r0 = n_pages * tk
    score(jnp.concatenate([in_ref[0], jnp.zeros((tk - nnew, D_IDX), F32)], axis=0), r0)
    kc_s[r0:r0 + tk, :] = jnp.concatenate([kn_ref[0], pad], axis=0).astype(BF16)
    vtc_s[:, r0:r0 + tk] = jnp.concatenate([vn_ref[0], pad], axis=0).T.astype(BF16)

    thr, cut = _topk_select_params(keys_s, n_tiles, tk, topk)
    m_s[...] = jnp.full(m_s.shape, NEG_BIG, F32)
    l_s[...] = jnp.zeros(l_s.shape, F32)
    acc_s[...] = jnp.zeros(acc_s.shape, F32)

    def attn_tile(t, carry):
        r0 = pl.multiple_of(t * tk, tk)
        kk = keys_s[pl.ds(r0, tk), :]
        kpos = r0 + lax.broadcasted_iota(I32, (tk, 1), 0)
        sel = jnp.logical_or(kk > thr, jnp.logical_and(kk == thr, kpos <= cut))
        sel = jnp.logical_and(sel, kpos <= qpos)
        _attend_tile(sel, kc_s[pl.ds(r0, tk), :], vtc_s[:, pl.ds(r0, tk)],
                     (qd_ref.at[0], qd_ref.at[1]), m_s, l_s, acc_s)
        return carry

    lax.fori_loop(0, n_tiles, attn_tile, 0)
    grp0 = ((lane // SUBLANES) % N_HEADS) < KV_GROUP
    o_ref[0] = jnp.where(grp0, acc_s[0] / l_s[0], acc_s[1] / l_s[1])


def _dsa_decode(page_table, ck, cv, ci, kn, vn, inew, qd, iqd, wd, *, past_len, topk):
    ns, n_pages = page_table.shape
    nnew = kn.shape[1]
    nrows = (n_pages + 1) * PAGE_SIZE

    def page_spec(width, p):
        return pl.BlockSpec((1, PAGE_SIZE, width), lambda s, pt, p=p: (pt[s, p], 0, 0))

    seq3 = lambda s, pt: (s, 0, 0)
    in_specs = ([page_spec(128, p) for p in range(n_pages)]
                + [page_spec(128, p) for p in range(n_pages)]
                + [page_spec(D_IDX, p) for p in range(n_pages)]
                + [pl.BlockSpec((1, nnew, 128), seq3),
                   pl.BlockSpec((1, nnew, 128), seq3),
                   pl.BlockSpec((1, nnew, D_IDX), seq3),
                   pl.BlockSpec((None, 2, LANES, 128), lambda s, pt: (s, 0, 0, 0)),
                   pl.BlockSpec((1, LANES, D_IDX), seq3),
                   pl.BlockSpec((1, 1, LANES), seq3)])
    grid_spec = pltpu.PrefetchScalarGridSpec(
        num_scalar_prefetch=1, grid=(ns,), in_specs=in_specs,
        out_specs=pl.BlockSpec((1, HEAD_DIM, LANES), seq3),
        scratch_shapes=[
            pltpu.VMEM((nrows, LANES), I32),
            pltpu.VMEM((nrows, LANES), BF16),
            pltpu.VMEM((LANES, nrows), BF16),
            pltpu.VMEM((N_KV_HEADS, 1, LANES), F32),
            pltpu.VMEM((N_KV_HEADS, 1, LANES), F32),
            pltpu.VMEM((N_KV_HEADS, HEAD_DIM, LANES), F32),
        ])
    return pl.pallas_call(
        functools.partial(_dsa_decode_kernel, n_pages=n_pages, past_len=past_len, topk=topk),
        grid_spec=grid_spec,
        out_shape=jax.ShapeDtypeStruct((ns, HEAD_DIM, LANES), F32),
        compiler_params=_cparams(("arbitrary",)),
        name="dsa_decode",
    )(page_table, *([ck] * n_pages), *([cv] * n_pages), *([ci] * n_pages), kn, vn, inew, qd, iqd, wd)


FF_CHUNK = 256


def _ffn_block(h, gpre, gpost, wgu_ref, wd_ref, a_s):
    d_ff = wd_ref.shape[0]
    hn = _rms(h, gpre).astype(BF16)
    for c in range(0, d_ff, FF_CHUNK):
        gate = _dot(hn, wgu_ref[:, c:c + FF_CHUNK])
        up = _dot(hn, wgu_ref[:, d_ff + c:d_ff + c + FF_CHUNK])
        a_s[:, c:c + FF_CHUNK] = (gate * jax.nn.sigmoid(gate) * up).astype(BF16)
    return h + _rms(_dot(a_s[...], wd_ref[...]), gpost)


def _even_out_kernel(x_ref, attn_ref, pool_ref, wo_ref, gmix_ref, gpre_ref, gpost_ref, wgu_ref, wd_ref,
                     o_ref, a_s):
    mix = _dot(attn_ref[...], wo_ref[0:D_ATTN, :]) + _dot(pool_ref[...], wo_ref[D_ATTN:, :])
    h = x_ref[...] + _rms(mix, gmix_ref[...])
    o_ref[...] = _ffn_block(h, gpre_ref[...], gpost_ref[...], wgu_ref, wd_ref, a_s)


def _even_out(x, attn, pool, wo, gmix, gpre, gpost, wgu, wd, *, tm):
    n, d = x.shape
    row = lambda i: (i, 0)
    const = lambda i: (0, 0)
    return pl.pallas_call(
        _even_out_kernel, grid=(n // tm,),
        in_specs=[pl.BlockSpec((tm, d), row), pl.BlockSpec((tm, 512), row), pl.BlockSpec((tm, 512), row),
                  pl.BlockSpec(wo.shape, const), pl.BlockSpec((1, d), const), pl.BlockSpec((1, d), const),
                  pl.BlockSpec((1, d), const), pl.BlockSpec(wgu.shape, const), pl.BlockSpec(wd.shape, const)],
        out_specs=pl.BlockSpec((tm, d), row),
        out_shape=jax.ShapeDtypeStruct((n, d), F32),
        scratch_shapes=[pltpu.VMEM((tm, wd.shape[0]), BF16)],
        compiler_params=_cparams(("arbitrary",)),
        name="even_out_ffn",
    )(x, attn, pool, wo, gmix, gpre, gpost, wgu, wd)


def _odd_kernel(x_ref, g_ref, win_ref, lng_ref, lnb_ref, ws_ref, bs_ref, wout_ref, gmix_ref,
                gpre_ref, gpost_ref, wgu_ref, wd_ref, o_ref, z_ref, y_s, a_s, *, tm, seq_len):
    d_sgu = wout_ref.shape[0]
    dh = d_sgu // N_SGU_HEADS
    x = x_ref[...]
    xn = _rms(x, g_ref[...]).astype(BF16)
    r = lax.broadcasted_iota(I32, (CHUNK, CHUNK), 0)
    c = lax.broadcasted_iota(I32, (CHUNK, CHUNK), 1)
    causal = jnp.logical_and(r // seq_len == c // seq_len, c <= r)
    for hd in range(N_SGU_HEADS):
        cs = slice(hd * dh, (hd + 1) * dh)
        u = _gelu(_dot(xn, win_ref[:, cs]))
        y_s[:, cs] = u.astype(BF16)
    vs = []
    for hd in range(N_SGU_HEADS):
        cs = slice(d_sgu + hd * dh, d_sgu + (hd + 1) * dh)
        vs.append(_gelu(_dot(xn, win_ref[:, cs])))
    v = jnp.concatenate(vs, axis=1)
    mu = jnp.mean(v, axis=-1, keepdims=True)
    vc = v - mu
    z = vc * lax.rsqrt(jnp.mean(vc * vc, axis=-1, keepdims=True) + EPS) * lng_ref[...] + lnb_ref[...]
    z_ref[...] = z
    zb = z.astype(BF16)
    for hd in range(N_SGU_HEADS):
        cs = slice(hd * dh, (hd + 1) * dh)
        wm = jnp.where(causal, ws_ref[hd], 0.0).astype(BF16)
        bias = bs_ref[hd]
        for ch in range(tm // CHUNK):
            rs = slice(ch * CHUNK, (ch + 1) * CHUNK)
            s = _dot(wm, zb[rs, cs]) + bias
            y_s[rs, cs] = (y_s[rs, cs].astype(F32) * s).astype(BF16)
    h = x + _rms(_dot(y_s[...], wout_ref[...]), gmix_ref[...])
    o_ref[...] = _ffn_block(h, gpre_ref[...], gpost_ref[...], wgu_ref, wd_ref, a_s)


def _odd(x, g, win, lng, lnb, ws, bs, wout, gmix, gpre, gpost, wgu, wd, *, tm, seq_len):
    n, d = x.shape
    d_sgu = wout.shape[0]
    row = lambda i: (i, 0)
    const = lambda i: (0, 0)
    const3 = lambda i: (0, 0, 0)
    return pl.pallas_call(
        functools.partial(_odd_kernel, tm=tm, seq_len=seq_len), grid=(n // tm,),
        in_specs=[pl.BlockSpec((tm, d), row), pl.BlockSpec((1, d), const), pl.BlockSpec(win.shape, const),
                  pl.BlockSpec((1, d_sgu), const), pl.BlockSpec((1, d_sgu), const),
                  pl.BlockSpec(ws.shape, const3), pl.BlockSpec(bs.shape, const3),
                  pl.BlockSpec(wout.shape, const), pl.BlockSpec((1, d), const), pl.BlockSpec((1, d), const),
                  pl.BlockSpec((1, d), const), pl.BlockSpec(wgu.shape, const), pl.BlockSpec(wd.shape, const)],
        out_specs=[pl.BlockSpec((tm, d), row), pl.BlockSpec((tm, d_sgu), row)],
        out_shape=[jax.ShapeDtypeStruct((n, d), F32), jax.ShapeDtypeStruct((n, d_sgu), F32)],
        scratch_shapes=[pltpu.VMEM((tm, d_sgu), BF16), pltpu.VMEM((tm, wd.shape[0]), BF16)],
        compiler_params=_cparams(("arbitrary",)),
        name="odd_mixer_ffn",
    )(x, g, win, lng, lnb, ws, bs, wout, gmix, gpre, gpost, wgu, wd)


def _tile_rows(n, want):
    tm = min(want, n)
    while n % tm:
        tm //= 2
    return tm


def kernel(x_prompt, x_sample, cache_k, cache_v, cache_idx_k, state_pool, page_table, norm_mix_pre, norm_mix_post, norm_ffn_pre, norm_ffn_post, w_in_even, w_out_even, w_pool_group, pool_scale, w_in_odd, sgu_norm_g, sgu_norm_b, w_spatial, b_spatial, w_out_odd, w_ffn_gate_up, w_ffn_down):
    bp, tp, d = x_prompt.shape
    bs, ts, _ = x_sample.shape
    n_pages = page_table.shape[1]
    past_len = n_pages * PAGE_SIZE
    topk_p = min(TOPK_MAX, tp // 4)
    topk_s = min(TOPK_MAX, (past_len + ts) // 4)
    depth = norm_mix_pre.shape[0]
    assert ts == SUBLANES and tp % Q_BLOCK == 0 and d % LANES == 0

    hp = x_prompt.reshape(bp * tp, d)
    hs = x_sample.reshape(bs * ts, d)
    tm_p = _tile_rows(bp * tp, 512)
    tm_p_seq = _tile_rows(tp, 512)
    tm_s = _tile_rows(bs * ts, 256)
    row = lambda a: a.reshape(1, -1)

    cos_p, sin_p = _rope_tables(jnp.tile(jnp.arange(tp, dtype=I32), bp))
    cos_s, sin_s = _rope_tables(jnp.tile(past_len + jnp.arange(ts, dtype=I32), bs))

    outs_p, outs_s, sgu_s = [], [], []
    for layer in range(depth):
        li = layer // 2
        gpre, gpost = row(norm_ffn_pre[layer]), row(norm_ffn_post[layer])
        wgu = w_ffn_gate_up[layer].astype(BF16)
        wd = w_ffn_down[layer].astype(BF16)
        gmix_pre, gmix_post = row(norm_mix_pre[layer]), row(norm_mix_post[layer])
        if layer % 2 == 0:
            wmain, wvt, wiwt = _arrange_w_in_even(w_in_even[li])
            wpg = w_pool_group[li].astype(BF16)
            psc = row(pool_scale[li])
            wo = w_out_even[li].astype(BF16)
            (q, k, kb, v, vt, iq, ik2, ik, iwt, pool, xp_tail) = _even_in(
                hp, gmix_pre, wmain, wvt, wiwt, cos_p, sin_p, wpg, psc, None, nb=bp, tm=tm_p_seq, past_len=0)
            attn = _dsa_prompt(q, iq, iwt, kb, vt, ik2, nb=bp, tk=min(256, tp), topk=topk_p)
            hp = _even_out(hp, attn, pool, wo, gmix_post, gpre, gpost, wgu, wd, tm=tm_p)
            outs_p.append((k.reshape(bp, tp, N_KV_HEADS, HEAD_DIM), v.reshape(bp, tp, N_KV_HEADS, HEAD_DIM),
                           ik.reshape(bp, tp, D_IDX), xp_tail[:, 1:, :]))
            prefix = jnp.pad(state_pool[li], ((0, 0), (1, 0), (0, 0)))
            (q, k, kb, v, vt, iq, ik2s, iks, iwt, pool, xp) = _even_in(
                hs, gmix_pre, wmain, wvt, wiwt, cos_s, sin_s, wpg, psc, prefix, nb=1, tm=tm_s, past_len=past_len)
            del kb, vt, ik2s
            q3 = q.reshape(bs, ts, 4, 2, HEAD_DIM).astype(F32)
            zq = jnp.zeros_like(q3[:, :, :, 0])
            qg0 = jnp.concatenate([q3[:, :, :, 0], zq], axis=-1)
            qg1 = jnp.concatenate([zq, q3[:, :, :, 1]], axis=-1)
            zz = jnp.zeros_like(qg0)
            def slots(a):
                a = jnp.transpose(a, (0, 2, 1, 3)).reshape(bs, N_HEADS * ts, a.shape[-1])
                return jnp.concatenate([a, a], axis=1)
            qd = jnp.stack([slots(jnp.concatenate([qg0, zz], axis=2)),
                            slots(jnp.concatenate([zz, qg1], axis=2))], axis=1).astype(BF16)
            iqd = slots(iq.reshape(bs, ts, N_IDX_HEADS, D_IDX))
            wd_slots = jnp.transpose(iwt[0].reshape(N_IDX_HEADS, bs, ts), (1, 0, 2)).reshape(bs, 1, N_IDX_HEADS * ts)
            wd_slots = jnp.concatenate([wd_slots, wd_slots], axis=2)
            ot = _dsa_decode(page_table,
                             cache_k[li].reshape(-1, PAGE_SIZE, D_KV), cache_v[li].reshape(-1, PAGE_SIZE, D_KV),
                             cache_idx_k[li], k.reshape(bs, ts, D_KV), v.reshape(bs, ts, D_KV),
                             iks.reshape(bs, ts, D_IDX), qd, iqd, wd_slots, past_len=past_len, topk=topk_s)
            attn_s = jnp.transpose(ot[:, :, 0:N_HEADS * ts].reshape(bs, HEAD_DIM, N_HEADS, ts), (0, 3, 2, 1))
            attn_s = attn_s.reshape(bs * ts, D_ATTN).astype(BF16)
            hs = _even_out(hs, attn_s, pool, wo, gmix_post, gpre, gpost, wgu, wd, tm=tm_s)
            new_pool_s = jnp.concatenate([state_pool[li][:, ts:, :], xp.reshape(bs, ts, -1)], axis=1)
            outs_s.append((k.reshape(bs, ts, N_KV_HEADS, HEAD_DIM), v.reshape(bs, ts, N_KV_HEADS, HEAD_DIM),
                           iks.reshape(bs, ts, D_IDX), new_pool_s))
        else:
            win = w_in_odd[li].astype(BF16)
            wout = w_out_odd[li].astype(BF16)
            lng, lnb = row(sgu_norm_g[li]), row(sgu_norm_b[li])
            ws_p = w_spatial[li][:, :CHUNK, :CHUNK]
            bs_p = b_spatial[li][:, :CHUNK, None]
            hp, _ = _odd(hp, gmix_pre, win, lng, lnb, ws_p, bs_p, wout, gmix_post, gpre, gpost, wgu, wd,
                         tm=tm_p, seq_len=CHUNK)
            reps = CHUNK // ts
            ws_s = jnp.tile(w_spatial[li][:, :ts, :ts], (1, reps, reps))
            bs_s = jnp.tile(b_spatial[li][:, :ts], (1, reps))[:, :, None]
            hs, zs = _odd(hs, gmix_pre, win, lng, lnb, ws_s, bs_s, wout, gmix_post, gpre, gpost, wgu, wd,
                          tm=_tile_rows(bs * ts, 256), seq_len=ts)
            sgu_s.append(zs.reshape(bs, ts, -1))

    return (hp.reshape(bp, tp, d), hs.reshape(bs, ts, d),
            jnp.stack([o[0] for o in outs_p]), jnp.stack([o[1] for o in outs_p]),
            jnp.stack([o[2] for o in outs_p]), jnp.stack([o[3] for o in outs_p]),
            jnp.stack([o[0] for o in outs_s]), jnp.stack([o[1] for o in outs_s]),
            jnp.stack([o[2] for o in outs_s]), jnp.stack([o[3] for o in outs_s]),
            jnp.stack(sgu_s))
```

```python
import functools

import jax
import jax.numpy as jnp
import numpy as np
from jax import lax
from jax.experimental import pallas as pl
from jax.experimental.pallas import tpu as pltpu

EPS = 1e-6
N_HEADS = 8
HEAD_DIM = 64
N_KV_HEADS = 2
KV_GROUP = N_HEADS // N_KV_HEADS
D_ATTN = N_HEADS * HEAD_DIM
D_KV = N_KV_HEADS * HEAD_DIM
N_IDX_HEADS = 8
D_IDX = 64
TOPK_MAX = 256
Q_BLOCK = 128
ROPE_THETA = 10000.0
POOL_WINDOWS = (2, 4, 8, 16)
POOL_BUF = 15
PAGE_SIZE = 128
CHUNK = 128
N_SGU_HEADS = 8
D_POOL_GROUP = 128

LANES = 128
SUBLANES = 8
VMEM_LIMIT = 56 * 1024 * 1024
INT_MIN = -2 ** 31
NEG_BIG = -1e30

F32 = jnp.float32
BF16 = jnp.bfloat16
I32 = jnp.int32

_NT = (((1,), (1,)), ((), ()))


def _cparams(sem):
    return pltpu.CompilerParams(dimension_semantics=sem, vmem_limit_bytes=VMEM_LIMIT)


def _rms(x, g):
    return x * lax.rsqrt(jnp.mean(x * x, axis=-1, keepdims=True) + EPS) * g


def _dot(a, b):
    return jnp.dot(a, b, preferred_element_type=F32)


def _dot_nt(a, b):
    return lax.dot_general(a, b, _NT, preferred_element_type=F32)


def _gelu(x):
    return 0.5 * x * (1.0 + lax.erf(x * (2.0 ** -0.5)))


C_Q, C_IQ, C_IK2, C_XP, C_END = 0, 512, 1024, 1280, 1792
R_K, R_V, R_IK, R_END = 0, 128, 256, 320


def _rope_tile(t, cos, sin, first_half):
    partner = jnp.where(first_half, pltpu.roll(t, 96, 1), pltpu.roll(t, 32, 1))
    return t * cos + partner * sin


def _rope_rows(t, cos_t, sin_t):
    half = HEAD_DIM // 2
    x1, x2 = t[0:half], t[half:HEAD_DIM]
    return jnp.concatenate([x1 * cos_t - x2 * sin_t, x2 * cos_t + x1 * sin_t], axis=0)


def _even_in_kernel(*refs, tm, decode, past_len):
    if decode:
        (x_ref, g_ref, w_ref, wt_ref, wiwt_ref, cos_ref, sin_ref, cost_ref, sint_ref, wpg_ref, psc_ref, pre_ref,
         q_ref, iq_ref, ik2_ref, kt_ref, vt_ref, ktb_ref, vtb_ref, ikt_ref, iwt_ref, pool_ref, xp_ref,
         ext_s) = refs
    else:
        (x_ref, g_ref, w_ref, wt_ref, wiwt_ref, cos_ref, sin_ref, cost_ref, sint_ref, wpg_ref, psc_ref,
         q_ref, iq_ref, ik2_ref, kt_ref, vt_ref, ktb_ref, vtb_ref, ikt_ref, iwt_ref, pool_ref, xp_ref,
         ext_s) = refs
    j = pl.program_id(1)
    xn = _rms(x_ref[...], g_ref[...]).astype(BF16)
    cos = cos_ref[...]
    sin = sin_ref[...]
    lane = lax.broadcasted_iota(I32, (1, LANES), 1)
    first_half = (lane % HEAD_DIM) < (HEAD_DIM // 2)

    def proj_rope(c0, c1):
        t = _dot(xn, w_ref[:, c0:c1])
        return [_rope_tile(t[:, c:c + LANES], cos, sin, first_half) for c in range(0, c1 - c0, LANES)]

    for half in range(2):
        tiles = proj_rope(C_Q + 256 * half, C_Q + 256 * (half + 1))
        for c, t in enumerate(tiles):
            col = 256 * half + LANES * c
            q_ref[:, col:col + LANES] = (t * (HEAD_DIM ** -0.5)).astype(BF16)
    for half in range(2):
        tiles = proj_rope(C_IQ + 256 * half, C_IQ + 256 * (half + 1))
        for c, t in enumerate(tiles):
            col = 256 * half + LANES * c
            iq_ref[:, col:col + LANES] = t.astype(BF16)
    tiles = proj_rope(C_IK2, C_XP)
    ik2_ref[:, 0:LANES] = tiles[0].astype(BF16)
    ik2_ref[:, LANES:2 * LANES] = tiles[1].astype(BF16)

    cos_t = cost_ref[...]
    sin_t = sint_ref[...]
    ft = _dot_nt(wt_ref[...], xn)
    kt = jnp.concatenate([_rope_rows(ft[R_K + h * HEAD_DIM:R_K + (h + 1) * HEAD_DIM], cos_t, sin_t)
                          for h in range(N_KV_HEADS)], axis=0)
    vt = ft[R_V:R_IK]
    kt_ref[0] = kt
    ktb_ref[0] = kt.astype(BF16)
    vt_ref[0] = vt
    vtb_ref[0] = vt.astype(BF16)
    ikt_ref[0] = _rope_rows(ft[R_IK:R_END], cos_t, sin_t)
    iwt_ref[0] = _dot_nt(wiwt_ref[...], xn) * (N_IDX_HEADS ** -0.5) * (D_IDX ** -0.5)

    xp = _dot(xn, w_ref[:, C_XP:C_END])
    row = lax.broadcasted_iota(I32, (tm, 1), 0)
    if decode:
        ns = tm // SUBLANES
        ext_s[:, 0:16, :] = pre_ref[...]
        ext_s[:, 16:24, :] = xp.reshape(ns, SUBLANES, 4 * D_POOL_GROUP)
        pos = past_len + (row % SUBLANES)
    else:
        @pl.when(j == 0)
        def _():
            ext_s[0:16, :] = jnp.zeros((16, 4 * D_POOL_GROUP), F32)
        ext_s[16:16 + tm, :] = xp
        pos = j * tm + row
    for g, w in enumerate(POOL_WINDOWS):
        cs = slice(g * D_POOL_GROUP, (g + 1) * D_POOL_GROUP)
        tok = xp[:, cs]
        acc = tok
        for i in range(1, w):
            if decode:
                acc = acc + ext_s[:, 16 - i:24 - i, cs].reshape(tm, D_POOL_GROUP)
            else:
                acc = acc + ext_s[16 - i:16 - i + tm, cs]
        cnt = jnp.minimum(w, pos + 1).astype(F32)
        d = acc / cnt - tok
        y = _dot(d.astype(BF16), wpg_ref[g]) * psc_ref[:, cs]
        pool_ref[:, cs] = y.astype(BF16)
    if decode:
        xp_ref[...] = xp
    else:
        ext_s[0:16, :] = xp[tm - 16:tm, :]
        xp_ref[0] = xp[tm - 16:tm, :]


def _rope_tables(pos):
    half = HEAD_DIM // 2
    inv = ROPE_THETA ** (-jnp.arange(half, dtype=F32) / half)
    ang = pos.astype(F32)[:, None] * inv[None, :]
    cos32, sin32 = jnp.cos(ang), jnp.sin(ang)
    cos = jnp.tile(cos32, (1, LANES // half))
    sin = jnp.tile(jnp.concatenate([-sin32, sin32], axis=1), (1, LANES // HEAD_DIM))
    return cos, sin, cos32.T, sin32.T


def _arrange_w_in_even(w):
    q = w[:, 0:512].reshape(-1, N_HEADS, HEAD_DIM)
    q = jnp.stack([q[:, 0:4], q[:, 4:8]], axis=2).reshape(-1, 512)
    k = w[:, 512:640]
    v = w[:, 640:768]
    iq = w[:, 768:1280]
    ik = w[:, 1280:1344]
    iw = w[:, 1344:1352]
    xp = w[:, 1352:1864]
    z = jnp.zeros_like(ik)
    main = jnp.concatenate([q, iq, ik, z, z, ik, xp], axis=1).astype(BF16)
    feat = jnp.concatenate([k, v, ik], axis=1).T.astype(BF16)
    return main, feat, iw.T.astype(BF16)


def _even_in(x, g, wmain, wfeat, wiwt, tabs, wpg, psc, prefix, *, nb, tm, past_len):
    n, d = x.shape
    t = n // nb
    nt = t // tm
    decode = prefix is not None
    cos, sin, cos_t, sin_t = tabs
    row = lambda b, j: (b * nt + j, 0)
    colblk = lambda b, j: (0, b * nt + j)
    const = lambda b, j: (0, 0)
    feat = lambda b, j: (b, 0, j)
    in_specs = [
        pl.BlockSpec((tm, d), row),
        pl.BlockSpec((1, d), const),
        pl.BlockSpec(wmain.shape, const),
        pl.BlockSpec(wfeat.shape, const),
        pl.BlockSpec(wiwt.shape, const),
        pl.BlockSpec((tm, LANES), row),
        pl.BlockSpec((tm, LANES), row),
        pl.BlockSpec((HEAD_DIM // 2, tm), colblk),
        pl.BlockSpec((HEAD_DIM // 2, tm), colblk),
        pl.BlockSpec(wpg.shape, lambda b, j: (0, 0, 0)),
        pl.BlockSpec((1, 512), const),
    ]
    args = [x, g, wmain, wfeat, wiwt, cos, sin, cos_t, sin_t, wpg, psc]
    if decode:
        ns = tm // SUBLANES
        in_specs.append(pl.BlockSpec((ns, 16, 512), lambda b, j: (b * nt + j, 0, 0)))
        args.append(prefix)
        xp_shape = jax.ShapeDtypeStruct((n, 512), F32)
        xp_spec = pl.BlockSpec((tm, 512), row)
        scratch = [pltpu.VMEM((ns, 24, 512), F32)]
    else:
        xp_shape = jax.ShapeDtypeStruct((nb, 16, 512), F32)
        xp_spec = pl.BlockSpec((1, 16, 512), lambda b, j: (b, 0, 0))
        scratch = [pltpu.VMEM((tm + 16, 512), F32)]
    out_shape = [
        jax.ShapeDtypeStruct((n, 512), BF16),
        jax.ShapeDtypeStruct((n, 512), BF16),
        jax.ShapeDtypeStruct((n, 256), BF16),
        jax.ShapeDtypeStruct((nb, 128, t), F32),
        jax.ShapeDtypeStruct((nb, 128, t), F32),
        jax.ShapeDtypeStruct((nb, 128, t), BF16),
        jax.ShapeDtypeStruct((nb, 128, t), BF16),
        jax.ShapeDtypeStruct((nb, D_IDX, t), F32),
        jax.ShapeDtypeStruct((nb, 8, t), F32),
        jax.ShapeDtypeStruct((n, 512), BF16),
        xp_shape,
    ]
    out_specs = [
        pl.BlockSpec((tm, 512), row),
        pl.BlockSpec((tm, 512), row),
        pl.BlockSpec((tm, 256), row),
        pl.BlockSpec((1, 128, tm), feat),
        pl.BlockSpec((1, 128, tm), feat),
        pl.BlockSpec((1, 128, tm), feat),
        pl.BlockSpec((1, 128, tm), feat),
        pl.BlockSpec((1, D_IDX, tm), feat),
        pl.BlockSpec((1, 8, tm), feat),
        pl.BlockSpec((tm, 512), row),
        xp_spec,
    ]
    return pl.pallas_call(
        functools.partial(_even_in_kernel, tm=tm, decode=decode, past_len=past_len),
        grid=(nb, nt), in_specs=in_specs, out_specs=out_specs, out_shape=out_shape,
        scratch_shapes=scratch, compiler_params=_cparams(("arbitrary", "arbitrary")),
        name="even_in_decode" if decode else "even_in_prompt",
    )(*args)


def _sort_key(s):
    s = jnp.where(s == 0.0, 0.0, s)
    b = lax.bitcast_convert_type(s, I32)
    return b ^ ((b >> 31) & 0x7FFFFFFF)


def _topk_search(count, shape, n_index_bits, topk):
    def bit_step(i, prefix):
        cand = prefix + jnp.left_shift(jnp.int32(1), 31 - i)
        c = count(lambda kk, idx: kk >= cand)
        return jnp.where(c >= topk, cand, prefix)

    thr = lax.fori_loop(0, 32, bit_step, jnp.full(shape, INT_MIN, I32))
    c_gt = count(lambda kk, idx: kk > thr)
    c_eq = count(lambda kk, idx: kk == thr)
    need = topk - c_gt
    ambiguous = jnp.logical_and(c_eq > need, thr != INT_MIN)

    def tie_search():
        def step(i, lo):
            cand = lo + jnp.left_shift(jnp.int32(1), n_index_bits - 1 - i)
            c = count(lambda kk, idx: jnp.logical_and(kk == thr, idx < cand))
            return jnp.where(c < need, cand, lo)
        return lax.fori_loop(0, n_index_bits, step, jnp.zeros(shape, I32))

    big = jnp.full(shape, 2 ** 30, I32)
    any_amb = jnp.max(ambiguous.astype(I32)) > 0
    cut = lax.cond(any_amb, lambda: jnp.where(ambiguous, tie_search(), big), lambda: big)
    return thr, cut


def _index_bits(n):
    return max(1, int(np.ceil(np.log2(n))))


def _dsa_prompt_kernel(q_ref, iq_ref, iwt_ref, kt_ref, vt_ref, ik2_ref, o_ref,
                       keys_s, qs_s, iqs_s, m_s, l_s, acc_s, *, tk, topk):
    i = pl.program_id(1)
    n_tiles = (i * Q_BLOCK + Q_BLOCK + tk - 1) // tk
    lane = lax.broadcasted_iota(I32, (1, LANES), 1)
    qpos_l = i * Q_BLOCK + lane
    qpos_r = i * Q_BLOCK + lax.broadcasted_iota(I32, (Q_BLOCK, 1), 0)
    lo_half = lane < HEAD_DIM

    for jj in range(4):
        qt = q_ref[:, jj * LANES:(jj + 1) * LANES]
        zero = jnp.zeros_like(qt)
        qs_s[jj * Q_BLOCK:(jj + 1) * Q_BLOCK, :] = jnp.where(lo_half, qt, zero)
        qs_s[(4 + jj) * Q_BLOCK:(5 + jj) * Q_BLOCK, :] = jnp.where(lo_half, zero, qt)
        iqs_s[jj * Q_BLOCK:(jj + 1) * Q_BLOCK, :] = iq_ref[:, jj * LANES:(jj + 1) * LANES]
    w = iwt_ref[0]

    def score_tile(t, carry):
        r0 = pl.multiple_of(t * tk, tk)
        ik2 = ik2_ref[pl.ds(r0, tk), :]
        iqs = iqs_s[...]
        rel_e = _dot_nt(ik2[:, 0:LANES], iqs)
        rel_o = _dot_nt(ik2[:, LANES:2 * LANES], iqs)
        s = jnp.zeros((tk, LANES), F32)
        for jj in range(4):
            cs = slice(jj * LANES, (jj + 1) * LANES)
            s = s + w[2 * jj:2 * jj + 1, :] * jnp.maximum(rel_e[:, cs], 0.0)
            s = s + w[2 * jj + 1:2 * jj + 2, :] * jnp.maximum(rel_o[:, cs], 0.0)
        kpos = r0 + lax.broadcasted_iota(I32, (tk, 1), 0)
        keys_s[pl.ds(r0, tk), :] = jnp.where(kpos <= qpos_l, _sort_key(s), INT_MIN)
        return carry

    lax.fori_loop(0, n_tiles, score_tile, 0)

    def count(pred):
        def body(t, acc):
            r0 = pl.multiple_of(t * tk, tk)
            kk = keys_s[pl.ds(r0, tk), :]
            idx = r0 + lax.broadcasted_iota(I32, (tk, 1), 0)
            m = pred(kk, idx).astype(I32)
            return acc + m.reshape(tk // SUBLANES, SUBLANES, LANES).sum(axis=0)
        acc = lax.fori_loop(0, n_tiles, body, jnp.zeros((SUBLANES, LANES), I32))
        return acc.sum(axis=0, keepdims=True)

    thr, cut = _topk_search(count, (1, LANES), _index_bits(keys_s.shape[0]), topk)
    thr_r = jnp.broadcast_to(thr, (LANES, LANES)).T[:, 0:1]
    cut_r = jnp.broadcast_to(cut, (LANES, LANES)).T[:, 0:1]

    m_s[...] = jnp.full(m_s.shape, NEG_BIG, F32)
    l_s[...] = jnp.zeros(l_s.shape, F32)
    acc_s[...] = jnp.zeros(acc_s.shape, F32)

    def attn_tile(t, carry):
        r0 = pl.multiple_of(t * tk, tk)
        kk = jnp.concatenate([keys_s[pl.ds(r0 + c, LANES), :].T for c in range(0, tk, LANES)], axis=1)
        kpos = r0 + lax.broadcasted_iota(I32, (1, tk), 1)
        sel = jnp.logical_or(kk > thr_r, jnp.logical_and(kk == thr_r, kpos <= cut_r))
        sel = jnp.logical_and(sel, kpos <= qpos_r)
        kt = kt_ref[0, :, pl.ds(r0, tk)]
        vt = vt_ref[0, :, pl.ds(r0, tk)]
        for h in range(N_HEADS):
            st = _dot(qs_s[h * Q_BLOCK:(h + 1) * Q_BLOCK, :], kt)
            m_old = m_s[h]
            m_new = jnp.maximum(m_old, jnp.max(jnp.where(sel, st, NEG_BIG), axis=1, keepdims=True))
            p = jnp.where(sel, jnp.exp(st - m_new), 0.0)
            alpha = jnp.exp(m_old - m_new)
            l_s[h] = alpha * l_s[h] + jnp.sum(p, axis=1, keepdims=True)
            m_s[h] = m_new
            acc_s[h] = acc_s[h] * alpha + _dot_nt(p.astype(BF16), vt)
        return carry

    lax.fori_loop(0, n_tiles, attn_tile, 0)
    _store_heads(o_ref, [acc_s[h] / l_s[h] for h in range(N_HEADS)], lo_half)


def _store_heads(o_ref, heads, lo_half):
    for jj in range(N_HEADS // 2):
        a, b = heads[2 * jj], heads[2 * jj + 1]
        if 2 * jj < KV_GROUP:
            tile = jnp.where(lo_half, a, pltpu.roll(b, HEAD_DIM, 1))
        else:
            tile = jnp.where(lo_half, pltpu.roll(a, HEAD_DIM, 1), b)
        o_ref[:, jj * LANES:(jj + 1) * LANES] = tile.astype(o_ref.dtype)


def _dsa_prompt(q, iq, iwt, ktb, vtb, ik2, *, nb, tk, topk):
    n = q.shape[0]
    t = n // nb
    nq = t // Q_BLOCK
    blk = lambda b, i: (b * nq + i, 0)
    seq = lambda b, i: (b, 0)
    feat = lambda b, i: (b, 0, 0)
    return pl.pallas_call(
        functools.partial(_dsa_prompt_kernel, tk=tk, topk=topk),
        grid=(nb, nq),
        in_specs=[
            pl.BlockSpec((Q_BLOCK, 512), blk),
            pl.BlockSpec((Q_BLOCK, 512), blk),
            pl.BlockSpec((1, 8, Q_BLOCK), lambda b, i: (b, 0, i)),
            pl.BlockSpec((1, 128, t), feat),
            pl.BlockSpec((1, 128, t), feat),
            pl.BlockSpec((t, 256), seq),
        ],
        out_specs=pl.BlockSpec((Q_BLOCK, 512), blk),
        out_shape=jax.ShapeDtypeStruct((n, 512), BF16),
        scratch_shapes=[
            pltpu.VMEM((t, LANES), I32),
            pltpu.VMEM((N_HEADS * Q_BLOCK, LANES), BF16),
            pltpu.VMEM((4 * Q_BLOCK, LANES), BF16),
            pltpu.VMEM((N_HEADS, Q_BLOCK, 1), F32),
            pltpu.VMEM((N_HEADS, Q_BLOCK, 1), F32),
            pltpu.VMEM((N_HEADS, Q_BLOCK, LANES), F32),
        ],
        compiler_params=_cparams(("arbitrary", "arbitrary")),
        name="dsa_prompt",
    )(q, iq, iwt, ktb, vtb, ik2)


SEQ_GROUP = Q_BLOCK // SUBLANES


def _dsa_decode_kernel(*refs, n_pages, topk, nnew):
    ip = refs[1:1 + n_pages]
    kp = refs[1 + n_pages:1 + 2 * n_pages]
    vp = refs[1 + 2 * n_pages:1 + 3 * n_pages]
    (iqd_ref, wcol_ref, inew_ref, qd_ref, knew_ref, vnew_ref, o_ref,
     keys_s, thr_s, cut_s) = refs[1 + 3 * n_pages:]
    t = pl.program_id(1)
    n_keys = keys_s.shape[1]
    past = n_pages * PAGE_SIZE
    n_tiles = n_keys // LANES
    lane = lax.broadcasted_iota(I32, (1, LANES), 1)
    qrow = lax.broadcasted_iota(I32, (nnew, 1), 0)
    new_valid = jnp.logical_and(lane <= qrow, lane < nnew)

    def head_sum(x):
        return x.reshape(N_IDX_HEADS, nnew, x.shape[1]).sum(axis=0)

    def pad_rows(a):
        return jnp.concatenate([a, jnp.zeros((LANES - nnew, a.shape[1]), a.dtype)], axis=0).astype(BF16)

    @pl.when(t < SEQ_GROUP)
    def _score():
        r0 = pl.multiple_of(t * nnew, nnew)
        iqd = iqd_ref[0]
        wcol = wcol_ref[0]
        for p in range(n_pages):
            rel = _dot(iqd, ip[p][0].astype(BF16))
            s = head_sum(wcol * jnp.maximum(rel, 0.0))
            keys_s[pl.ds(r0, nnew), p * LANES:(p + 1) * LANES] = _sort_key(s)
        rel = _dot_nt(iqd, pad_rows(inew_ref[0]))
        s = head_sum(wcol * jnp.maximum(rel, 0.0))
        keys_s[pl.ds(r0, nnew), past:past + LANES] = jnp.where(new_valid, _sort_key(s), INT_MIN)

    @pl.when(t == SEQ_GROUP)
    def _search():
        def count(pred):
            acc = jnp.zeros((Q_BLOCK, LANES), I32)
            for c in range(n_tiles):
                idx = c * LANES + lane
                acc = acc + pred(keys_s[:, c * LANES:(c + 1) * LANES], idx).astype(I32)
            return acc.sum(axis=1, keepdims=True)
        thr, cut = _topk_search(count, (Q_BLOCK, 1), _index_bits(n_keys), topk)
        thr_s[...] = thr
        cut_s[...] = cut

    @pl.when(t >= SEQ_GROUP)
    def _attend():
        r0 = pl.multiple_of((t - SEQ_GROUP) * nnew, nnew)
        kk = keys_s[pl.ds(r0, nnew), :]
        thr = thr_s[pl.ds(r0, nnew), :]
        cut = cut_s[pl.ds(r0, nnew), :]
        idx = lax.broadcasted_iota(I32, (1, n_keys), 1)
        sel = jnp.logical_or(kk > thr, jnp.logical_and(kk == thr, idx <= cut))
        sel = jnp.logical_and(sel, kk != INT_MIN)
        sel = jnp.concatenate([sel.astype(F32)] * N_HEADS, axis=0) > 0.5
        qd = qd_ref[0]
        st = jnp.concatenate([_dot(qd, kp[p][0].astype(BF16)) for p in range(n_pages)]
                             + [_dot_nt(qd, pad_rows(knew_ref[0]))], axis=1)
        m = jnp.max(jnp.where(sel, st, NEG_BIG), axis=1, keepdims=True)
        pr = jnp.where(sel, jnp.exp(st - m), 0.0)
        l = jnp.sum(pr, axis=1, keepdims=True)
        pb = pr.astype(BF16)
        o = _dot(pb[:, past:past + LANES], pad_rows(vnew_ref[0]))
        for p in range(n_pages):
            o = o + _dot_nt(pb[:, p * LANES:(p + 1) * LANES], vp[p][0].astype(BF16))
        o = o / l
        _store_heads(o_ref, [o[h * nnew:(h + 1) * nnew] for h in range(N_HEADS)], lane < HEAD_DIM)


def _dsa_decode(page_table, ci_t, ck_t, cv_t, iqd, wcol, inew, qd, knew, vnew, *, topk):
    ns, n_pages = page_table.shape
    nnew = knew.shape[1]
    assert ns % SEQ_GROUP == 0 and nnew == SUBLANES
    n_keys = (n_pages + 1) * PAGE_SIZE

    def seq_a(g, t):
        return g * SEQ_GROUP + jnp.minimum(t, SEQ_GROUP - 1)

    def seq_c(g, t):
        return g * SEQ_GROUP + jnp.maximum(t - SEQ_GROUP, 0)

    def page_spec(rows, which, p):
        return pl.BlockSpec((1, rows, PAGE_SIZE), lambda g, t, pt, p=p: (pt[which(g, t), p], 0, 0))

    a3 = lambda g, t, pt: (seq_a(g, t), 0, 0)
    c3 = lambda g, t, pt: (seq_c(g, t), 0, 0)
    in_specs = ([page_spec(D_IDX, seq_a, p) for p in range(n_pages)]
                + [page_spec(D_KV, seq_c, p) for p in range(n_pages)]
                + [page_spec(D_KV, seq_c, p) for p in range(n_pages)]
                + [pl.BlockSpec((1, N_IDX_HEADS * nnew, D_IDX), a3),
                   pl.BlockSpec((1, N_IDX_HEADS * nnew, 1), a3),
                   pl.BlockSpec((1, nnew, D_IDX), a3),
                   pl.BlockSpec((1, N_HEADS * nnew, D_KV), c3),
                   pl.BlockSpec((1, nnew, D_KV), c3),
                   pl.BlockSpec((1, nnew, D_KV), c3)])
    grid_spec = pltpu.PrefetchScalarGridSpec(
        num_scalar_prefetch=1, grid=(ns // SEQ_GROUP, 2 * SEQ_GROUP), in_specs=in_specs,
        out_specs=pl.BlockSpec((nnew, D_ATTN), lambda g, t, pt: (seq_c(g, t), 0)),
        scratch_shapes=[
            pltpu.VMEM((Q_BLOCK, n_keys), I32),
            pltpu.VMEM((Q_BLOCK, 1), I32),
            pltpu.VMEM((Q_BLOCK, 1), I32),
        ])
    return pl.pallas_call(
        functools.partial(_dsa_decode_kernel, n_pages=n_pages, topk=topk, nnew=nnew),
        grid_spec=grid_spec,
        out_shape=jax.ShapeDtypeStruct((ns * nnew, D_ATTN), F32),
        compiler_params=_cparams(("arbitrary", "arbitrary")),
        name="dsa_decode",
    )(page_table, *([ci_t] * n_pages), *([ck_t] * n_pages), *([cv_t] * n_pages),
      iqd, wcol, inew, qd, knew, vnew)


FF_CHUNK = 256


def _ffn_block(h, gpre, gpost, wgu_ref, wd_ref, a_s):
    d_ff = wd_ref.shape[0]
    hn = _rms(h, gpre).astype(BF16)
    for c in range(0, d_ff, FF_CHUNK):
        gate = _dot(hn, wgu_ref[:, c:c + FF_CHUNK])
        up = _dot(hn, wgu_ref[:, d_ff + c:d_ff + c + FF_CHUNK])
        a_s[:, c:c + FF_CHUNK] = (gate * jax.nn.sigmoid(gate) * up).astype(BF16)
    return h + _rms(_dot(a_s[...], wd_ref[...]), gpost)


def _even_out_kernel(x_ref, attn_ref, pool_ref, wo_ref, gmix_ref, gpre_ref, gpost_ref, wgu_ref, wd_ref,
                     o_ref, a_s):
    mix = _dot(attn_ref[...], wo_ref[0:D_ATTN, :]) + _dot(pool_ref[...], wo_ref[D_ATTN:, :])
    h = x_ref[...] + _rms(mix, gmix_ref[...])
    o_ref[...] = _ffn_block(h, gpre_ref[...], gpost_ref[...], wgu_ref, wd_ref, a_s)


def _even_out(x, attn, pool, wo, gmix, gpre, gpost, wgu, wd, *, tm):
    n, d = x.shape
    row = lambda i: (i, 0)
    const = lambda i: (0, 0)
    return pl.pallas_call(
        _even_out_kernel, grid=(n // tm,),
        in_specs=[pl.BlockSpec((tm, d), row), pl.BlockSpec((tm, 512), row), pl.BlockSpec((tm, 512), row),
                  pl.BlockSpec(wo.shape, const), pl.BlockSpec((1, d), const), pl.BlockSpec((1, d), const),
                  pl.BlockSpec((1, d), const), pl.BlockSpec(wgu.shape, const), pl.BlockSpec(wd.shape, const)],
        out_specs=pl.BlockSpec((tm, d), row),
        out_shape=jax.ShapeDtypeStruct((n, d), F32),
        scratch_shapes=[pltpu.VMEM((tm, wd.shape[0]), BF16)],
        compiler_params=_cparams(("arbitrary",)),
        name="even_out_ffn",
    )(x, attn, pool, wo, gmix, gpre, gpost, wgu, wd)


def _odd_kernel(x_ref, g_ref, win_ref, lng_ref, lnb_ref, ws_ref, bs_ref, wout_ref, gmix_ref,
                gpre_ref, gpost_ref, wgu_ref, wd_ref, o_ref, z_ref, y_s, a_s, *, tm, seq_len):
    d_sgu = wout_ref.shape[0]
    dh = d_sgu // N_SGU_HEADS
    x = x_ref[...]
    xn = _rms(x, g_ref[...]).astype(BF16)
    r = lax.broadcasted_iota(I32, (CHUNK, CHUNK), 0)
    c = lax.broadcasted_iota(I32, (CHUNK, CHUNK), 1)
    causal = jnp.logical_and(r // seq_len == c // seq_len, c <= r)
    for hd in range(N_SGU_HEADS):
        cs = slice(hd * dh, (hd + 1) * dh)
        u = _gelu(_dot(xn, win_ref[:, cs]))
        y_s[:, cs] = u.astype(BF16)
    vs = []
    for hd in range(N_SGU_HEADS):
        cs = slice(d_sgu + hd * dh, d_sgu + (hd + 1) * dh)
        vs.append(_gelu(_dot(xn, win_ref[:, cs])))
    v = jnp.concatenate(vs, axis=1)
    mu = jnp.mean(v, axis=-1, keepdims=True)
    vc = v - mu
    z = vc * lax.rsqrt(jnp.mean(vc * vc, axis=-1, keepdims=True) + EPS) * lng_ref[...] + lnb_ref[...]
    z_ref[...] = z
    zb = z.astype(BF16)
    for hd in range(N_SGU_HEADS):
        cs = slice(hd * dh, (hd + 1) * dh)
        wm = jnp.where(causal, ws_ref[hd], 0.0).astype(BF16)
        bias = bs_ref[hd]
        for ch in range(tm // CHUNK):
            rs = slice(ch * CHUNK, (ch + 1) * CHUNK)
            s = _dot(wm, zb[rs, cs]) + bias
            y_s[rs, cs] = (y_s[rs, cs].astype(F32) * s).astype(BF16)
    h = x + _rms(_dot(y_s[...], wout_ref[...]), gmix_ref[...])
    o_ref[...] = _ffn_block(h, gpre_ref[...], gpost_ref[...], wgu_ref, wd_ref, a_s)


def _odd(x, g, win, lng, lnb, ws, bs, wout, gmix, gpre, gpost, wgu, wd, *, tm, seq_len):
    n, d = x.shape
    d_sgu = wout.shape[0]
    row = lambda i: (i, 0)
    const = lambda i: (0, 0)
    const3 = lambda i: (0, 0, 0)
    return pl.pallas_call(
        functools.partial(_odd_kernel, tm=tm, seq_len=seq_len), grid=(n // tm,),
        in_specs=[pl.BlockSpec((tm, d), row), pl.BlockSpec((1, d), const), pl.BlockSpec(win.shape, const),
                  pl.BlockSpec((1, d_sgu), const), pl.BlockSpec((1, d_sgu), const),
                  pl.BlockSpec(ws.shape, const3), pl.BlockSpec(bs.shape, const3),
                  pl.BlockSpec(wout.shape, const), pl.BlockSpec((1, d), const), pl.BlockSpec((1, d), const),
                  pl.BlockSpec((1, d), const), pl.BlockSpec(wgu.shape, const), pl.BlockSpec(wd.shape, const)],
        out_specs=[pl.BlockSpec((tm, d), row), pl.BlockSpec((tm, d_sgu), row)],
        out_shape=[jax.ShapeDtypeStruct((n, d), F32), jax.ShapeDtypeStruct((n, d_sgu), F32)],
        scratch_shapes=[pltpu.VMEM((tm, d_sgu), BF16), pltpu.VMEM((tm, wd.shape[0]), BF16)],
        compiler_params=_cparams(("arbitrary",)),
        name="odd_mixer_ffn",
    )(x, g, win, lng, lnb, ws, bs, wout, gmix, gpre, gpost, wgu, wd)


def _tile_rows(n, want):
    tm = min(want, n)
    while n % tm:
        tm //= 2
    return tm


def _from_feature_major(a, heads):
    b, f, t = a.shape
    if heads is None:
        return jnp.transpose(a, (0, 2, 1))
    return jnp.transpose(a.reshape(b, heads, f // heads, t), (0, 3, 1, 2))


def kernel(x_prompt, x_sample, cache_k, cache_v, cache_idx_k, state_pool, page_table, norm_mix_pre, norm_mix_post, norm_ffn_pre, norm_ffn_post, w_in_even, w_out_even, w_pool_group, pool_scale, w_in_odd, sgu_norm_g, sgu_norm_b, w_spatial, b_spatial, w_out_odd, w_ffn_gate_up, w_ffn_down):
    bp, tp, d = x_prompt.shape
    bs, ts, _ = x_sample.shape
    n_pages = page_table.shape[1]
    past_len = n_pages * PAGE_SIZE
    topk_p = min(TOPK_MAX, tp // 4)
    topk_s = min(TOPK_MAX, (past_len + ts) // 4)
    depth = norm_mix_pre.shape[0]
    assert ts == SUBLANES and tp % Q_BLOCK == 0 and d % LANES == 0

    hp = x_prompt.reshape(bp * tp, d)
    hs = x_sample.reshape(bs * ts, d)
    tm_p = _tile_rows(bp * tp, 512)
    tm_p_seq = _tile_rows(tp, 512)
    tm_s = _tile_rows(bs * ts, 256)
    row = lambda a: a.reshape(1, -1)

    tabs_p = _rope_tables(jnp.tile(jnp.arange(tp, dtype=I32), bp))
    tabs_s = _rope_tables(jnp.tile(past_len + jnp.arange(ts, dtype=I32), bs))

    outs_p, outs_s, sgu_s = [], [], []
    for layer in range(depth):
        li = layer // 2
        gpre, gpost = row(norm_ffn_pre[layer]), row(norm_ffn_post[layer])
        wgu = w_ffn_gate_up[layer].astype(BF16)
        wd = w_ffn_down[layer].astype(BF16)
        gmix_pre, gmix_post = row(norm_mix_pre[layer]), row(norm_mix_post[layer])
        if layer % 2 == 0:
            wmain, wfeat, wiwt = _arrange_w_in_even(w_in_even[li])
            wpg = w_pool_group[li].astype(BF16)
            psc = row(pool_scale[li])
            wo = w_out_even[li].astype(BF16)
            (q, iq, ik2, kt, vt, ktb, vtb, ikt, iwt, pool, xp_tail) = _even_in(
                hp, gmix_pre, wmain, wfeat, wiwt, tabs_p, wpg, psc, None, nb=bp, tm=tm_p_seq, past_len=0)
            attn = _dsa_prompt(q, iq, iwt, ktb, vtb, ik2, nb=bp, tk=min(256, tp), topk=topk_p)
            hp = _even_out(hp, attn, pool, wo, gmix_post, gpre, gpost, wgu, wd, tm=tm_p)
            outs_p.append((_from_feature_major(kt, N_KV_HEADS), _from_feature_major(vt, N_KV_HEADS),
                           _from_feature_major(ikt, None), xp_tail[:, 1:, :]))
            prefix = jnp.pad(state_pool[li], ((0, 0), (1, 0), (0, 0)))
            (q, iq, ik2, kt, vt, ktb, vtb, ikt, iwt, pool, xp) = _even_in(
                hs, gmix_pre, wmain, wfeat, wiwt, tabs_s, wpg, psc, prefix, nb=1, tm=tm_s, past_len=past_len)
            del ktb, vtb
            k_new = _from_feature_major(kt.reshape(1, D_KV, bs * ts), None).reshape(bs, ts, D_KV)
            v_new = _from_feature_major(vt.reshape(1, D_KV, bs * ts), None).reshape(bs, ts, D_KV)
            ik_new = _from_feature_major(ikt, None).reshape(bs, ts, D_IDX)
            q3 = q.reshape(bs, ts, 4, 2, HEAD_DIM)
            zq = jnp.zeros_like(q3[:, :, :, 0])
            qg = jnp.concatenate([jnp.concatenate([q3[:, :, :, 0], zq], axis=-1),
                                  jnp.concatenate([zq, q3[:, :, :, 1]], axis=-1)], axis=2)
            qd = jnp.transpose(qg, (0, 2, 1, 3)).reshape(bs, N_HEADS * ts, D_KV)
            iqd = jnp.transpose(iq.reshape(bs, ts, N_IDX_HEADS, D_IDX), (0, 2, 1, 3)).reshape(bs, N_IDX_HEADS * ts, D_IDX)
            wcol = jnp.transpose(iwt[0].reshape(N_IDX_HEADS, bs, ts), (1, 0, 2)).reshape(bs, N_IDX_HEADS * ts, 1)
            ci_t = jnp.transpose(cache_idx_k[li], (0, 2, 1))
            ck_t = jnp.transpose(cache_k[li], (0, 2, 3, 1)).reshape(-1, D_KV, PAGE_SIZE)
            cv_t = jnp.transpose(cache_v[li], (0, 2, 3, 1)).reshape(-1, D_KV, PAGE_SIZE)
            attn_s = _dsa_decode(page_table, ci_t, ck_t, cv_t, iqd, wcol, ik_new, qd, k_new, v_new, topk=topk_s)
            hs = _even_out(hs, attn_s.astype(BF16), pool, wo, gmix_post, gpre, gpost, wgu, wd, tm=tm_s)
            new_pool_s = jnp.concatenate([state_pool[li][:, ts:, :], xp.reshape(bs, ts, -1)], axis=1)
            outs_s.append((k_new.reshape(bs, ts, N_KV_HEADS, HEAD_DIM), v_new.reshape(bs, ts, N_KV_HEADS, HEAD_DIM),
                           ik_new, new_pool_s))
        else:
            win = w_in_odd[li].astype(BF16)
            wout = w_out_odd[li].astype(BF16)
            lng, lnb = row(sgu_norm_g[li]), row(sgu_norm_b[li])
            ws_p = w_spatial[li][:, :CHUNK, :CHUNK]
            bs_p = b_spatial[li][:, :CHUNK, None]
            hp, _ = _odd(hp, gmix_pre, win, lng, lnb, ws_p, bs_p, wout, gmix_post, gpre, gpost, wgu, wd,
                         tm=tm_p, seq_len=CHUNK)
            reps = CHUNK // ts
            ws_s = jnp.tile(w_spatial[li][:, :ts, :ts], (1, reps, reps))
            bs_s = jnp.tile(b_spatial[li][:, :ts], (1, reps))[:, :, None]
            hs, zs = _odd(hs, gmix_pre, win, lng, lnb, ws_s, bs_s, wout, gmix_post, gpre, gpost, wgu, wd,
                          tm=_tile_rows(bs * ts, 256), seq_len=ts)
            sgu_s.append(zs.reshape(bs, ts, -1))

    return (hp.reshape(bp, tp, d), hs.reshape(bs, ts, d),
            jnp.stack([o[0] for o in outs_p]), jnp.stack([o[1] for o in outs_p]),
            jnp.stack([o[2] for o in outs_p]), jnp.stack([o[3] for o in outs_p]),
            jnp.stack([o[0] for o in outs_s]), jnp.stack([o[1] for o in outs_s]),
            jnp.stack([o[2] for o in outs_s]), jnp.stack([o[3] for o in outs_s]),
            jnp.stack(sgu_s))
```

```python
import functools

import jax
import jax.numpy as jnp
import numpy as np
from jax import lax
from jax.experimental import pallas as pl
from jax.experimental.pallas import tpu as pltpu

EPS = 1e-6
N_HEADS = 8
HEAD_DIM = 64
N_KV_HEADS = 2
KV_GROUP = N_HEADS // N_KV_HEADS
D_ATTN = N_HEADS * HEAD_DIM
D_KV = N_KV_HEADS * HEAD_DIM
N_IDX_HEADS = 8
D_IDX = 64
TOPK_MAX = 256
Q_BLOCK = 128
ROPE_THETA = 10000.0
POOL_WINDOWS = (2, 4, 8, 16)
POOL_BUF = 15
PAGE_SIZE = 128
CHUNK = 128
N_SGU_HEADS = 8
D_POOL_GROUP = 128

LANES = 128
SUBLANES = 8
VMEM_LIMIT = 56 * 1024 * 1024
INT_MIN = -2 ** 31
NEG_BIG = -1e30
LOG2E = 1.4426950408889634

F32 = jnp.float32
BF16 = jnp.bfloat16
I32 = jnp.int32

_NT = (((1,), (1,)), ((), ()))


def _cparams(sem):
    return pltpu.CompilerParams(dimension_semantics=sem, vmem_limit_bytes=VMEM_LIMIT)


def _rms(x, g):
    return x * lax.rsqrt(jnp.mean(x * x, axis=-1, keepdims=True) + EPS) * g


def _dot(a, b):
    return jnp.dot(a, b, preferred_element_type=F32)


def _dot_nt(a, b):
    return lax.dot_general(a, b, _NT, preferred_element_type=F32)


def _gelu(x):
    return 0.5 * x * (1.0 + lax.erf(x * (2.0 ** -0.5)))


C_Q, C_IQ, C_IK2, C_K, C_XP, C_END = 0, 512, 1024, 1280, 1408, 1920
R_K, R_V, R_IK, R_END = 0, 128, 256, 320


def _rope_tile(t, cos, sin, first_half):
    partner = jnp.where(first_half, pltpu.roll(t, 96, 1), pltpu.roll(t, 32, 1))
    return t * cos + partner * sin


def _rope_rows(t, cos_t, sin_t):
    half = HEAD_DIM // 2
    x1, x2 = t[0:half], t[half:HEAD_DIM]
    return jnp.concatenate([x1 * cos_t - x2 * sin_t, x2 * cos_t + x1 * sin_t], axis=0)


def _even_in_kernel(*refs, tm, decode, past_len):
    if decode:
        (x_ref, g_ref, w_ref, wt_ref, wiwt_ref, cos_ref, sin_ref, cost_ref, sint_ref, wpg_ref, psc_ref, pre_ref,
         q_ref, iq_ref, ik2_ref, kb_ref, kt_ref, vt_ref, vtb_ref, ikt_ref, iwt_ref, pool_ref, xp_ref,
         ext_s) = refs
    else:
        (x_ref, g_ref, w_ref, wt_ref, wiwt_ref, cos_ref, sin_ref, cost_ref, sint_ref, wpg_ref, psc_ref,
         q_ref, iq_ref, ik2_ref, kb_ref, kt_ref, vt_ref, vtb_ref, ikt_ref, iwt_ref, pool_ref, xp_ref,
         ext_s) = refs
    j = pl.program_id(1)
    xn = _rms(x_ref[...], g_ref[...]).astype(BF16)
    cos = cos_ref[...]
    sin = sin_ref[...]
    lane = lax.broadcasted_iota(I32, (1, LANES), 1)
    first_half = (lane % HEAD_DIM) < (HEAD_DIM // 2)

    def proj_rope(c0, c1):
        t = _dot(xn, w_ref[:, c0:c1])
        return [_rope_tile(t[:, c:c + LANES], cos, sin, first_half) for c in range(0, c1 - c0, LANES)]

    for half in range(2):
        tiles = proj_rope(C_Q + 256 * half, C_Q + 256 * (half + 1))
        for c, t in enumerate(tiles):
            col = 256 * half + LANES * c
            q_ref[:, col:col + LANES] = (t * (HEAD_DIM ** -0.5 * LOG2E)).astype(BF16)
    for half in range(2):
        tiles = proj_rope(C_IQ + 256 * half, C_IQ + 256 * (half + 1))
        for c, t in enumerate(tiles):
            col = 256 * half + LANES * c
            iq_ref[:, col:col + LANES] = t.astype(BF16)
    tiles = proj_rope(C_IK2, C_K)
    ik2_ref[:, 0:LANES] = tiles[0].astype(BF16)
    ik2_ref[:, LANES:2 * LANES] = tiles[1].astype(BF16)
    kb_ref[...] = proj_rope(C_K, C_XP)[0].astype(BF16)

    cos_t = cost_ref[...]
    sin_t = sint_ref[...]
    ft = _dot_nt(wt_ref[...], xn)
    kt = jnp.concatenate([_rope_rows(ft[R_K + h * HEAD_DIM:R_K + (h + 1) * HEAD_DIM], cos_t, sin_t)
                          for h in range(N_KV_HEADS)], axis=0)
    vt = ft[R_V:R_IK]
    kt_ref[0] = kt
    vt_ref[0] = vt
    vtb_ref[0] = vt.astype(BF16)
    ikt_ref[0] = _rope_rows(ft[R_IK:R_END], cos_t, sin_t)
    iwt_ref[0] = _dot_nt(wiwt_ref[...], xn) * (N_IDX_HEADS ** -0.5) * (D_IDX ** -0.5)

    xp = _dot(xn, w_ref[:, C_XP:C_END])
    row = lax.broadcasted_iota(I32, (tm, 1), 0)
    if decode:
        ns = tm // SUBLANES
        ext_s[:, 0:16, :] = pre_ref[...]
        ext_s[:, 16:24, :] = xp.reshape(ns, SUBLANES, 4 * D_POOL_GROUP)
        pos = past_len + (row % SUBLANES)
    else:
        @pl.when(j == 0)
        def _():
            ext_s[0:16, :] = jnp.zeros((16, 4 * D_POOL_GROUP), F32)
        ext_s[16:16 + tm, :] = xp
        pos = j * tm + row
    for g, w in enumerate(POOL_WINDOWS):
        cs = slice(g * D_POOL_GROUP, (g + 1) * D_POOL_GROUP)
        tok = xp[:, cs]
        acc = tok
        for i in range(1, w):
            if decode:
                acc = acc + ext_s[:, 16 - i:24 - i, cs].reshape(tm, D_POOL_GROUP)
            else:
                acc = acc + ext_s[16 - i:16 - i + tm, cs]
        cnt = jnp.minimum(w, pos + 1).astype(F32)
        d = acc / cnt - tok
        y = _dot(d.astype(BF16), wpg_ref[g]) * psc_ref[:, cs]
        pool_ref[:, cs] = y.astype(BF16)
    if decode:
        xp_ref[...] = xp
    else:
        ext_s[0:16, :] = xp[tm - 16:tm, :]
        xp_ref[0] = xp[tm - 16:tm, :]


def _rope_tables(pos):
    half = HEAD_DIM // 2
    inv = ROPE_THETA ** (-jnp.arange(half, dtype=F32) / half)
    ang = pos.astype(F32)[:, None] * inv[None, :]
    cos32, sin32 = jnp.cos(ang), jnp.sin(ang)
    cos = jnp.tile(cos32, (1, LANES // half))
    sin = jnp.tile(jnp.concatenate([-sin32, sin32], axis=1), (1, LANES // HEAD_DIM))
    return cos, sin, cos32.T, sin32.T


def _arrange_w_in_even(w):
    q = w[:, 0:512].reshape(-1, N_HEADS, HEAD_DIM)
    q = jnp.stack([q[:, 0:4], q[:, 4:8]], axis=2).reshape(-1, 512)
    k = w[:, 512:640]
    v = w[:, 640:768]
    iq = w[:, 768:1280]
    ik = w[:, 1280:1344]
    iw = w[:, 1344:1352]
    xp = w[:, 1352:1864]
    z = jnp.zeros_like(ik)
    main = jnp.concatenate([q, iq, ik, z, z, ik, k, xp], axis=1).astype(BF16)
    feat = jnp.concatenate([k, v, ik], axis=1).T.astype(BF16)
    return main, feat, iw.T.astype(BF16)


def _even_in(x, g, wmain, wfeat, wiwt, tabs, wpg, psc, prefix, *, nb, tm, past_len):
    n, d = x.shape
    t = n // nb
    nt = t // tm
    decode = prefix is not None
    cos, sin, cos_t, sin_t = tabs
    row = lambda b, j: (b * nt + j, 0)
    colblk = lambda b, j: (0, b * nt + j)
    const = lambda b, j: (0, 0)
    feat = lambda b, j: (b, 0, j)
    in_specs = [
        pl.BlockSpec((tm, d), row),
        pl.BlockSpec((1, d), const),
        pl.BlockSpec(wmain.shape, const),
        pl.BlockSpec(wfeat.shape, const),
        pl.BlockSpec(wiwt.shape, const),
        pl.BlockSpec((tm, LANES), row),
        pl.BlockSpec((tm, LANES), row),
        pl.BlockSpec((HEAD_DIM // 2, tm), colblk),
        pl.BlockSpec((HEAD_DIM // 2, tm), colblk),
        pl.BlockSpec(wpg.shape, lambda b, j: (0, 0, 0)),
        pl.BlockSpec((1, 512), const),
    ]
    args = [x, g, wmain, wfeat, wiwt, cos, sin, cos_t, sin_t, wpg, psc]
    if decode:
        ns = tm // SUBLANES
        in_specs.append(pl.BlockSpec((ns, 16, 512), lambda b, j: (b * nt + j, 0, 0)))
        args.append(prefix)
        xp_shape = jax.ShapeDtypeStruct((n, 512), F32)
        xp_spec = pl.BlockSpec((tm, 512), row)
        scratch = [pltpu.VMEM((ns, 24, 512), F32)]
    else:
        xp_shape = jax.ShapeDtypeStruct((nb, 16, 512), F32)
        xp_spec = pl.BlockSpec((1, 16, 512), lambda b, j: (b, 0, 0))
        scratch = [pltpu.VMEM((tm + 16, 512), F32)]
    out_shape = [
        jax.ShapeDtypeStruct((n, 512), BF16),
        jax.ShapeDtypeStruct((n, 512), BF16),
        jax.ShapeDtypeStruct((n, 256), BF16),
        jax.ShapeDtypeStruct((n, 128), BF16),
        jax.ShapeDtypeStruct((nb, 128, t), F32),
        jax.ShapeDtypeStruct((nb, 128, t), F32),
        jax.ShapeDtypeStruct((nb, 128, t), BF16),
        jax.ShapeDtypeStruct((nb, D_IDX, t), F32),
        jax.ShapeDtypeStruct((nb, 8, t), F32),
        jax.ShapeDtypeStruct((n, 512), BF16),
        xp_shape,
    ]
    out_specs = [
        pl.BlockSpec((tm, 512), row),
        pl.BlockSpec((tm, 512), row),
        pl.BlockSpec((tm, 256), row),
        pl.BlockSpec((tm, 128), row),
        pl.BlockSpec((1, 128, tm), feat),
        pl.BlockSpec((1, 128, tm), feat),
        pl.BlockSpec((1, 128, tm), feat),
        pl.BlockSpec((1, D_IDX, tm), feat),
        pl.BlockSpec((1, 8, tm), feat),
        pl.BlockSpec((tm, 512), row),
        xp_spec,
    ]
    return pl.pallas_call(
        functools.partial(_even_in_kernel, tm=tm, decode=decode, past_len=past_len),
        grid=(nb, nt), in_specs=in_specs, out_specs=out_specs, out_shape=out_shape,
        scratch_shapes=scratch, compiler_params=_cparams(("arbitrary", "arbitrary")),
        name="even_in_decode" if decode else "even_in_prompt",
    )(*args)


def _sort_key(s):
    s = jnp.where(s == 0.0, 0.0, s)
    b = lax.bitcast_convert_type(s, I32)
    return b ^ ((b >> 31) & 0x7FFFFFFF)


def _topk_search(count, shape, n_index_bits, topk):
    def bit_step(i, prefix):
        cand = prefix + jnp.left_shift(jnp.int32(1), 31 - i)
        c = count(lambda kk, idx: kk >= cand)
        return jnp.where(c >= topk, cand, prefix)

    thr = lax.fori_loop(0, 32, bit_step, jnp.full(shape, INT_MIN, I32))
    c_gt = count(lambda kk, idx: kk > thr)
    c_eq = count(lambda kk, idx: kk == thr)
    need = topk - c_gt
    ambiguous = jnp.logical_and(c_eq > need, thr != INT_MIN)

    def tie_search():
        def step(i, lo):
            cand = lo + jnp.left_shift(jnp.int32(1), n_index_bits - 1 - i)
            c = count(lambda kk, idx: jnp.logical_and(kk == thr, idx < cand))
            return jnp.where(c < need, cand, lo)
        return lax.fori_loop(0, n_index_bits, step, jnp.zeros(shape, I32))

    big = jnp.full(shape, 2 ** 30, I32)
    any_amb = jnp.max(ambiguous.astype(I32)) > 0
    cut = lax.cond(any_amb, lambda: jnp.where(ambiguous, tie_search(), big), lambda: big)
    return thr, cut


def _index_bits(n):
    return max(1, int(np.ceil(np.log2(n))))


def _fold_rows(x, op):
    return op(x.reshape(x.shape[0] // SUBLANES, SUBLANES, x.shape[1]), axis=0)


def _dsa_prompt_kernel(q_ref, iq_ref, iwt_ref, kb_ref, vt_ref, ik2_ref, o_ref,
                       keys_s, qm_s, iqs_s, acc_s, *, tk, topk):
    i = pl.program_id(1)
    n_tiles = (i * Q_BLOCK + Q_BLOCK + tk - 1) // tk
    lane = lax.broadcasted_iota(I32, (1, LANES), 1)
    qpos = i * Q_BLOCK + lane
    qpos_l = qpos
    lo_half = lane < HEAD_DIM
    nq = KV_GROUP * Q_BLOCK

    for jj in range(KV_GROUP):
        qt = q_ref[:, jj * LANES:(jj + 1) * LANES]
        zero = jnp.zeros_like(qt)
        qm_s[0, jj * Q_BLOCK:(jj + 1) * Q_BLOCK, :] = jnp.where(lo_half, qt, zero)
        qm_s[1, jj * Q_BLOCK:(jj + 1) * Q_BLOCK, :] = jnp.where(lo_half, zero, qt)
        iqs_s[jj * Q_BLOCK:(jj + 1) * Q_BLOCK, :] = iq_ref[:, jj * LANES:(jj + 1) * LANES]
    w = iwt_ref[0]

    def score_tile(t, carry):
        r0 = pl.multiple_of(t * tk, tk)
        ik2 = ik2_ref[pl.ds(r0, tk), :]
        iqs = iqs_s[...]
        rel_e = _dot_nt(ik2[:, 0:LANES], iqs)
        rel_o = _dot_nt(ik2[:, LANES:2 * LANES], iqs)
        s = jnp.zeros((tk, LANES), F32)
        for jj in range(4):
            cs = slice(jj * LANES, (jj + 1) * LANES)
            s = s + w[2 * jj:2 * jj + 1, :] * jnp.maximum(rel_e[:, cs], 0.0)
            s = s + w[2 * jj + 1:2 * jj + 2, :] * jnp.maximum(rel_o[:, cs], 0.0)
        kpos = r0 + lax.broadcasted_iota(I32, (tk, 1), 0)
        keys_s[pl.ds(r0, tk), :] = jnp.where(kpos <= qpos_l, _sort_key(s), INT_MIN)
        return carry

    lax.fori_loop(0, n_tiles, score_tile, 0)

    def count(pred):
        def body(t, acc):
            r0 = pl.multiple_of(t * tk, tk)
            kk = keys_s[pl.ds(r0, tk), :]
            idx = r0 + lax.broadcasted_iota(I32, (tk, 1), 0)
            return acc + _fold_rows(pred(kk, idx).astype(I32), jnp.sum)
        acc = lax.fori_loop(0, n_tiles, body, jnp.zeros((SUBLANES, LANES), I32))
        return acc.sum(axis=0, keepdims=True)

    thr, cut = _topk_search(count, (1, LANES), _index_bits(keys_s.shape[0]), topk)

    def masked_logits(t):
        r0 = pl.multiple_of(t * tk, tk)
        kk = keys_s[pl.ds(r0, tk), :]
        kpos = r0 + lax.broadcasted_iota(I32, (tk, 1), 0)
        sel = jnp.logical_or(kk > thr, jnp.logical_and(kk == thr, kpos <= cut))
        sel = jnp.logical_and(sel, kpos <= qpos)
        sel = jnp.concatenate([sel] * KV_GROUP, axis=1)
        kt = kb_ref[pl.ds(r0, tk), :]
        return r0, [jnp.where(sel, _dot_nt(kt, qm_s[g]), NEG_BIG) for g in range(N_KV_HEADS)]

    acc_s[...] = jnp.zeros(acc_s.shape, F32)

    def pv_tile(t, carry):
        ms, ls = carry
        r0, st = masked_logits(t)
        ms_new, ls_new = [], []
        for g in range(N_KV_HEADS):
            m_new = jnp.maximum(ms[g], jnp.max(_fold_rows(st[g], jnp.max), axis=0, keepdims=True))
            alpha = jnp.exp2(ms[g] - m_new)
            p = jnp.exp2(st[g] - m_new)
            ls_new.append(ls[g] * alpha + _fold_rows(p, jnp.sum))
            ms_new.append(m_new)
            vt = vt_ref[0, g * HEAD_DIM:(g + 1) * HEAD_DIM, pl.ds(r0, tk)]
            acc_s[g] = acc_s[g] * alpha + _dot(vt, p.astype(BF16))
        return tuple(ms_new), tuple(ls_new)

    _, ls = lax.fori_loop(
        0, n_tiles, pv_tile,
        (tuple(jnp.full((1, nq), NEG_BIG, F32) for _ in range(N_KV_HEADS)),
         tuple(jnp.zeros((SUBLANES, nq), F32) for _ in range(N_KV_HEADS))))
    heads = []
    for g in range(N_KV_HEADS):
        og = acc_s[g] / jnp.sum(ls[g], axis=0, keepdims=True)
        heads += [og[:, jj * Q_BLOCK:(jj + 1) * Q_BLOCK] for jj in range(KV_GROUP)]
    o_ref[...] = jnp.concatenate(heads, axis=0).T.astype(o_ref.dtype)


def _store_heads(o_ref, heads, lo_half):
    for jj in range(N_HEADS // 2):
        a, b = heads[2 * jj], heads[2 * jj + 1]
        if 2 * jj < KV_GROUP:
            tile = jnp.where(lo_half, a, pltpu.roll(b, HEAD_DIM, 1))
        else:
            tile = jnp.where(lo_half, pltpu.roll(a, HEAD_DIM, 1), b)
        o_ref[:, jj * LANES:(jj + 1) * LANES] = tile.astype(o_ref.dtype)


def _dsa_prompt(q, iq, iwt, kb, vtb, ik2, *, nb, tk, topk):
    n = q.shape[0]
    t = n // nb
    nq = t // Q_BLOCK
    blk = lambda b, i: (b * nq + i, 0)
    seq = lambda b, i: (b, 0)
    feat = lambda b, i: (b, 0, 0)
    return pl.pallas_call(
        functools.partial(_dsa_prompt_kernel, tk=tk, topk=topk),
        grid=(nb, nq),
        in_specs=[
            pl.BlockSpec((Q_BLOCK, 512), blk),
            pl.BlockSpec((Q_BLOCK, 512), blk),
            pl.BlockSpec((1, 8, Q_BLOCK), lambda b, i: (b, 0, i)),
            pl.BlockSpec((t, 128), seq),
            pl.BlockSpec((1, 128, t), feat),
            pl.BlockSpec((t, 256), seq),
        ],
        out_specs=pl.BlockSpec((Q_BLOCK, 512), blk),
        out_shape=jax.ShapeDtypeStruct((n, 512), BF16),
        scratch_shapes=[
            pltpu.VMEM((t, LANES), I32),
            pltpu.VMEM((N_KV_HEADS, KV_GROUP * Q_BLOCK, LANES), BF16),
            pltpu.VMEM((4 * Q_BLOCK, LANES), BF16),
            pltpu.VMEM((N_KV_HEADS, HEAD_DIM, KV_GROUP * Q_BLOCK), F32),
        ],
        compiler_params=_cparams(("arbitrary", "arbitrary")),
        name="dsa_prompt",
    )(q, iq, iwt, kb, vtb, ik2)


SEQ_GROUP = Q_BLOCK // SUBLANES


def _dsa_decode_kernel(*refs, n_pages, topk, nnew):
    ip = refs[1:1 + n_pages]
    kp = refs[1 + n_pages:1 + 2 * n_pages]
    vp = refs[1 + 2 * n_pages:1 + 3 * n_pages]
    (iqd_ref, wcol_ref, inew_ref, qd_ref, knew_ref, vnew_ref, o_ref,
     keys_s, thr_s, cut_s) = refs[1 + 3 * n_pages:]
    t = pl.program_id(1)
    n_keys = keys_s.shape[1]
    past = n_pages * PAGE_SIZE
    n_tiles = n_keys // LANES
    lane = lax.broadcasted_iota(I32, (1, LANES), 1)
    qrow = lax.broadcasted_iota(I32, (nnew, 1), 0)
    new_valid = jnp.logical_and(lane <= qrow, lane < nnew)

    def head_sum(x):
        return x.reshape(N_IDX_HEADS, nnew, x.shape[1]).sum(axis=0)

    def pad_rows(a):
        return jnp.concatenate([a, jnp.zeros((LANES - nnew, a.shape[1]), a.dtype)], axis=0).astype(BF16)

    @pl.when(t < SEQ_GROUP)
    def _score():
        r0 = pl.multiple_of(t * nnew, nnew)
        iqd = iqd_ref[0]
        wcol = wcol_ref[0]
        for p in range(n_pages):
            rel = _dot(iqd, ip[p][0].astype(BF16))
            s = head_sum(wcol * jnp.maximum(rel, 0.0))
            keys_s[pl.ds(r0, nnew), p * LANES:(p + 1) * LANES] = _sort_key(s)
        rel = _dot_nt(iqd, pad_rows(inew_ref[0]))
        s = head_sum(wcol * jnp.maximum(rel, 0.0))
        keys_s[pl.ds(r0, nnew), past:past + LANES] = jnp.where(new_valid, _sort_key(s), INT_MIN)

    @pl.when(t == SEQ_GROUP)
    def _search():
        def count(pred):
            acc = jnp.zeros((Q_BLOCK, LANES), I32)
            for c in range(n_tiles):
                idx = c * LANES + lane
                acc = acc + pred(keys_s[:, c * LANES:(c + 1) * LANES], idx).astype(I32)
            return acc.sum(axis=1, keepdims=True)
        thr, cut = _topk_search(count, (Q_BLOCK, 1), _index_bits(n_keys), topk)
        thr_s[...] = thr
        cut_s[...] = cut

    @pl.when(t >= SEQ_GROUP)
    def _attend():
        r0 = pl.multiple_of((t - SEQ_GROUP) * nnew, nnew)
        kk = keys_s[pl.ds(r0, nnew), :]
        thr = thr_s[pl.ds(r0, nnew), :]
        cut = cut_s[pl.ds(r0, nnew), :]
        idx = lax.broadcasted_iota(I32, (1, n_keys), 1)
        sel = jnp.logical_or(kk > thr, jnp.logical_and(kk == thr, idx <= cut))
        sel = jnp.logical_and(sel, kk != INT_MIN)
        sel = jnp.concatenate([sel.astype(F32)] * N_HEADS, axis=0) > 0.5
        qd = qd_ref[0]
        st = jnp.concatenate([_dot(qd, kp[p][0].astype(BF16)) for p in range(n_pages)]
                             + [_dot_nt(qd, pad_rows(knew_ref[0]))], axis=1)
        m = jnp.max(jnp.where(sel, st, NEG_BIG), axis=1, keepdims=True)
        pr = jnp.where(sel, jnp.exp2(st - m), 0.0)
        l = jnp.sum(pr, axis=1, keepdims=True)
        pb = pr.astype(BF16)
        o = _dot(pb[:, past:past + LANES], pad_rows(vnew_ref[0]))
        for p in range(n_pages):
            o = o + _dot_nt(pb[:, p * LANES:(p + 1) * LANES], vp[p][0].astype(BF16))
        o = o / l
        _store_heads(o_ref, [o[h * nnew:(h + 1) * nnew] for h in range(N_HEADS)], lane < HEAD_DIM)


def _dsa_decode(page_table, ci_t, ck_t, cv_t, iqd, wcol, inew, qd, knew, vnew, *, topk):
    ns, n_pages = page_table.shape
    nnew = knew.shape[1]
    assert ns % SEQ_GROUP == 0 and nnew == SUBLANES
    n_keys = (n_pages + 1) * PAGE_SIZE

    def seq_a(g, t):
        return g * SEQ_GROUP + jnp.minimum(t, SEQ_GROUP - 1)

    def seq_c(g, t):
        return g * SEQ_GROUP + jnp.maximum(t - SEQ_GROUP, 0)

    def page_spec(rows, which, p):
        return pl.BlockSpec((1, rows, PAGE_SIZE), lambda g, t, pt, p=p: (pt[which(g, t), p], 0, 0))

    a3 = lambda g, t, pt: (seq_a(g, t), 0, 0)
    c3 = lambda g, t, pt: (seq_c(g, t), 0, 0)
    in_specs = ([page_spec(D_IDX, seq_a, p) for p in range(n_pages)]
                + [page_spec(D_KV, seq_c, p) for p in range(n_pages)]
                + [page_spec(D_KV, seq_c, p) for p in range(n_pages)]
                + [pl.BlockSpec((1, N_IDX_HEADS * nnew, D_IDX), a3),
                   pl.BlockSpec((1, N_IDX_HEADS * nnew, 1), a3),
                   pl.BlockSpec((1, nnew, D_IDX), a3),
                   pl.BlockSpec((1, N_HEADS * nnew, D_KV), c3),
                   pl.BlockSpec((1, nnew, D_KV), c3),
                   pl.BlockSpec((1, nnew, D_KV), c3)])
    grid_spec = pltpu.PrefetchScalarGridSpec(
        num_scalar_prefetch=1, grid=(ns // SEQ_GROUP, 2 * SEQ_GROUP), in_specs=in_specs,
        out_specs=pl.BlockSpec((nnew, D_ATTN), lambda g, t, pt: (seq_c(g, t), 0)),
        scratch_shapes=[
            pltpu.VMEM((Q_BLOCK, n_keys), I32),
            pltpu.VMEM((Q_BLOCK, 1), I32),
            pltpu.VMEM((Q_BLOCK, 1), I32),
        ])
    return pl.pallas_call(
        functools.partial(_dsa_decode_kernel, n_pages=n_pages, topk=topk, nnew=nnew),
        grid_spec=grid_spec,
        out_shape=jax.ShapeDtypeStruct((ns * nnew, D_ATTN), F32),
        compiler_params=_cparams(("arbitrary", "arbitrary")),
        name="dsa_decode",
    )(page_table, *([ci_t] * n_pages), *([ck_t] * n_pages), *([cv_t] * n_pages),
      iqd, wcol, inew, qd, knew, vnew)


FF_CHUNK = 256


def _ffn_block(h, gpre, gpost, wgu_ref, wd_ref, a_s):
    d_ff = wd_ref.shape[0]
    hn = _rms(h, gpre).astype(BF16)
    for c in range(0, d_ff, FF_CHUNK):
        gate = _dot(hn, wgu_ref[:, c:c + FF_CHUNK])
        up = _dot(hn, wgu_ref[:, d_ff + c:d_ff + c + FF_CHUNK])
        a_s[:, c:c + FF_CHUNK] = (gate * jax.nn.sigmoid(gate) * up).astype(BF16)
    return h + _rms(_dot(a_s[...], wd_ref[...]), gpost)


def _even_out_kernel(x_ref, attn_ref, pool_ref, wo_ref, gmix_ref, gpre_ref, gpost_ref, wgu_ref, wd_ref,
                     o_ref, a_s):
    mix = _dot(attn_ref[...], wo_ref[0:D_ATTN, :]) + _dot(pool_ref[...], wo_ref[D_ATTN:, :])
    h = x_ref[...] + _rms(mix, gmix_ref[...])
    o_ref[...] = _ffn_block(h, gpre_ref[...], gpost_ref[...], wgu_ref, wd_ref, a_s)


def _even_out(x, attn, pool, wo, gmix, gpre, gpost, wgu, wd, *, tm):
    n, d = x.shape
    row = lambda i: (i, 0)
    const = lambda i: (0, 0)
    return pl.pallas_call(
        _even_out_kernel, grid=(n // tm,),
        in_specs=[pl.BlockSpec((tm, d), row), pl.BlockSpec((tm, 512), row), pl.BlockSpec((tm, 512), row),
                  pl.BlockSpec(wo.shape, const), pl.BlockSpec((1, d), const), pl.BlockSpec((1, d), const),
                  pl.BlockSpec((1, d), const), pl.BlockSpec(wgu.shape, const), pl.BlockSpec(wd.shape, const)],
        out_specs=pl.BlockSpec((tm, d), row),
        out_shape=jax.ShapeDtypeStruct((n, d), F32),
        scratch_shapes=[pltpu.VMEM((tm, wd.shape[0]), BF16)],
        compiler_params=_cparams(("arbitrary",)),
        name="even_out_ffn",
    )(x, attn, pool, wo, gmix, gpre, gpost, wgu, wd)


def _odd_kernel(x_ref, g_ref, win_ref, lng_ref, lnb_ref, ws_ref, bs_ref, wout_ref, gmix_ref,
                gpre_ref, gpost_ref, wgu_ref, wd_ref, o_ref, z_ref, y_s, a_s, *, tm, seq_len):
    d_sgu = wout_ref.shape[0]
    dh = d_sgu // N_SGU_HEADS
    x = x_ref[...]
    xn = _rms(x, g_ref[...]).astype(BF16)
    r = lax.broadcasted_iota(I32, (CHUNK, CHUNK), 0)
    c = lax.broadcasted_iota(I32, (CHUNK, CHUNK), 1)
    causal = jnp.logical_and(r // seq_len == c // seq_len, c <= r)
    for hd in range(N_SGU_HEADS):
        cs = slice(hd * dh, (hd + 1) * dh)
        u = _gelu(_dot(xn, win_ref[:, cs]))
        y_s[:, cs] = u.astype(BF16)
    vs = []
    for hd in range(N_SGU_HEADS):
        cs = slice(d_sgu + hd * dh, d_sgu + (hd + 1) * dh)
        vs.append(_gelu(_dot(xn, win_ref[:, cs])))
    v = jnp.concatenate(vs, axis=1)
    mu = jnp.mean(v, axis=-1, keepdims=True)
    vc = v - mu
    z = vc * lax.rsqrt(jnp.mean(vc * vc, axis=-1, keepdims=True) + EPS) * lng_ref[...] + lnb_ref[...]
    z_ref[...] = z
    zb = z.astype(BF16)
    for hd in range(N_SGU_HEADS):
        cs = slice(hd * dh, (hd + 1) * dh)
        wm = jnp.where(causal, ws_ref[hd], 0.0).astype(BF16)
        bias = bs_ref[hd]
        for ch in range(tm // CHUNK):
            rs = slice(ch * CHUNK, (ch + 1) * CHUNK)
            s = _dot(wm, zb[rs, cs]) + bias
            y_s[rs, cs] = (y_s[rs, cs].astype(F32) * s).astype(BF16)
    h = x + _rms(_dot(y_s[...], wout_ref[...]), gmix_ref[...])
    o_ref[...] = _ffn_block(h, gpre_ref[...], gpost_ref[...], wgu_ref, wd_ref, a_s)


def _odd(x, g, win, lng, lnb, ws, bs, wout, gmix, gpre, gpost, wgu, wd, *, tm, seq_len):
    n, d = x.shape
    d_sgu = wout.shape[0]
    row = lambda i: (i, 0)
    const = lambda i: (0, 0)
    const3 = lambda i: (0, 0, 0)
    return pl.pallas_call(
        functools.partial(_odd_kernel, tm=tm, seq_len=seq_len), grid=(n // tm,),
        in_specs=[pl.BlockSpec((tm, d), row), pl.BlockSpec((1, d), const), pl.BlockSpec(win.shape, const),
                  pl.BlockSpec((1, d_sgu), const), pl.BlockSpec((1, d_sgu), const),
                  pl.BlockSpec(ws.shape, const3), pl.BlockSpec(bs.shape, const3),
                  pl.BlockSpec(wout.shape, const), pl.BlockSpec((1, d), const), pl.BlockSpec((1, d), const),
                  pl.BlockSpec((1, d), const), pl.BlockSpec(wgu.shape, const), pl.BlockSpec(wd.shape, const)],
        out_specs=[pl.BlockSpec((tm, d), row), pl.BlockSpec((tm, d_sgu), row)],
        out_shape=[jax.ShapeDtypeStruct((n, d), F32), jax.ShapeDtypeStruct((n, d_sgu), F32)],
        scratch_shapes=[pltpu.VMEM((tm, d_sgu), BF16), pltpu.VMEM((tm, wd.shape[0]), BF16)],
        compiler_params=_cparams(("arbitrary",)),
        name="odd_mixer_ffn",
    )(x, g, win, lng, lnb, ws, bs, wout, gmix, gpre, gpost, wgu, wd)


def _tile_rows(n, want):
    tm = min(want, n)
    while n % tm:
        tm //= 2
    return tm


def _from_feature_major(a, heads):
    b, f, t = a.shape
    if heads is None:
        return jnp.transpose(a, (0, 2, 1))
    return jnp.transpose(a.reshape(b, heads, f // heads, t), (0, 3, 1, 2))


def kernel(x_prompt, x_sample, cache_k, cache_v, cache_idx_k, state_pool, page_table, norm_mix_pre, norm_mix_post, norm_ffn_pre, norm_ffn_post, w_in_even, w_out_even, w_pool_group, pool_scale, w_in_odd, sgu_norm_g, sgu_norm_b, w_spatial, b_spatial, w_out_odd, w_ffn_gate_up, w_ffn_down):
    bp, tp, d = x_prompt.shape
    bs, ts, _ = x_sample.shape
    n_pages = page_table.shape[1]
    past_len = n_pages * PAGE_SIZE
    topk_p = min(TOPK_MAX, tp // 4)
    topk_s = min(TOPK_MAX, (past_len + ts) // 4)
    depth = norm_mix_pre.shape[0]
    assert ts == SUBLANES and tp % Q_BLOCK == 0 and d % LANES == 0

    hp = x_prompt.reshape(bp * tp, d)
    hs = x_sample.reshape(bs * ts, d)
    tm_p = _tile_rows(bp * tp, 512)
    tm_p_seq = _tile_rows(tp, 512)
    tm_s = _tile_rows(bs * ts, 256)
    row = lambda a: a.reshape(1, -1)

    tabs_p = _rope_tables(jnp.tile(jnp.arange(tp, dtype=I32), bp))
    tabs_s = _rope_tables(jnp.tile(past_len + jnp.arange(ts, dtype=I32), bs))

    outs_p, outs_s, sgu_s = [], [], []
    for layer in range(depth):
        li = layer // 2
        gpre, gpost = row(norm_ffn_pre[layer]), row(norm_ffn_post[layer])
        wgu = w_ffn_gate_up[layer].astype(BF16)
        wd = w_ffn_down[layer].astype(BF16)
        gmix_pre, gmix_post = row(norm_mix_pre[layer]), row(norm_mix_post[layer])
        if layer % 2 == 0:
            wmain, wfeat, wiwt = _arrange_w_in_even(w_in_even[li])
            wpg = w_pool_group[li].astype(BF16)
            psc = row(pool_scale[li])
            wo = w_out_even[li].astype(BF16)
            (q, iq, ik2, kb, kt, vt, vtb, ikt, iwt, pool, xp_tail) = _even_in(
                hp, gmix_pre, wmain, wfeat, wiwt, tabs_p, wpg, psc, None, nb=bp, tm=tm_p_seq, past_len=0)
            attn = _dsa_prompt(q, iq, iwt, kb, vtb, ik2, nb=bp, tk=min(512, tp), topk=topk_p)
            hp = _even_out(hp, attn, pool, wo, gmix_post, gpre, gpost, wgu, wd, tm=tm_p)
            outs_p.append((_from_feature_major(kt, N_KV_HEADS), _from_feature_major(vt, N_KV_HEADS),
                           _from_feature_major(ikt, None), xp_tail[:, 1:, :]))
            prefix = jnp.pad(state_pool[li], ((0, 0), (1, 0), (0, 0)))
            (q, iq, ik2, kb, kt, vt, vtb, ikt, iwt, pool, xp) = _even_in(
                hs, gmix_pre, wmain, wfeat, wiwt, tabs_s, wpg, psc, prefix, nb=1, tm=tm_s, past_len=past_len)
            del kb, vtb
            k_new = _from_feature_major(kt.reshape(1, D_KV, bs * ts), None).reshape(bs, ts, D_KV)
            v_new = _from_feature_major(vt.reshape(1, D_KV, bs * ts), None).reshape(bs, ts, D_KV)
            ik_new = _from_feature_major(ikt, None).reshape(bs, ts, D_IDX)
            q3 = q.reshape(bs, ts, 4, 2, HEAD_DIM)
            zq = jnp.zeros_like(q3[:, :, :, 0])
            qg = jnp.concatenate([jnp.concatenate([q3[:, :, :, 0], zq], axis=-1),
                                  jnp.concatenate([zq, q3[:, :, :, 1]], axis=-1)], axis=2)
            qd = jnp.transpose(qg, (0, 2, 1, 3)).reshape(bs, N_HEADS * ts, D_KV)
            iqd = jnp.transpose(iq.reshape(bs, ts, N_IDX_HEADS, D_IDX), (0, 2, 1, 3)).reshape(bs, N_IDX_HEADS * ts, D_IDX)
            wcol = jnp.transpose(iwt[0].reshape(N_IDX_HEADS, bs, ts), (1, 0, 2)).reshape(bs, N_IDX_HEADS * ts, 1)
            ci_t = jnp.transpose(cache_idx_k[li], (0, 2, 1))
            ck_t = jnp.transpose(cache_k[li], (0, 2, 3, 1)).reshape(-1, D_KV, PAGE_SIZE)
            cv_t = jnp.transpose(cache_v[li], (0, 2, 3, 1)).reshape(-1, D_KV, PAGE_SIZE)
            attn_s = _dsa_decode(page_table, ci_t, ck_t, cv_t, iqd, wcol, ik_new, qd, k_new, v_new, topk=topk_s)
            hs = _even_out(hs, attn_s.astype(BF16), pool, wo, gmix_post, gpre, gpost, wgu, wd, tm=tm_s)
            new_pool_s = jnp.concatenate([state_pool[li][:, ts:, :], xp.reshape(bs, ts, -1)], axis=1)
            outs_s.append((k_new.reshape(bs, ts, N_KV_HEADS, HEAD_DIM), v_new.reshape(bs, ts, N_KV_HEADS, HEAD_DIM),
                           ik_new, new_pool_s))
        else:
            win = w_in_odd[li].astype(BF16)
            wout = w_out_odd[li].astype(BF16)
            lng, lnb = row(sgu_norm_g[li]), row(sgu_norm_b[li])
            ws_p = w_spatial[li][:, :CHUNK, :CHUNK]
            bs_p = b_spatial[li][:, :CHUNK, None]
            hp, _ = _odd(hp, gmix_pre, win, lng, lnb, ws_p, bs_p, wout, gmix_post, gpre, gpost, wgu, wd,
                         tm=tm_p, seq_len=CHUNK)
            reps = CHUNK // ts
            ws_s = jnp.tile(w_spatial[li][:, :ts, :ts], (1, reps, reps))
            bs_s = jnp.tile(b_spatial[li][:, :ts], (1, reps))[:, :, None]
            hs, zs = _odd(hs, gmix_pre, win, lng, lnb, ws_s, bs_s, wout, gmix_post, gpre, gpost, wgu, wd,
                          tm=_tile_rows(bs * ts, 256), seq_len=ts)
            sgu_s.append(zs.reshape(bs, ts, -1))

    return (hp.reshape(bp, tp, d), hs.reshape(bs, ts, d),
            jnp.stack([o[0] for o in outs_p]), jnp.stack([o[1] for o in outs_p]),
            jnp.stack([o[2] for o in outs_p]), jnp.stack([o[3] for o in outs_p]),
            jnp.stack([o[0] for o in outs_s]), jnp.stack([o[1] for o in outs_s]),
            jnp.stack([o[2] for o in outs_s]), jnp.stack([o[3] for o in outs_s]),
            jnp.stack(sgu_s))
```

```python
import functools

import jax
import jax.numpy as jnp
import numpy as np
from jax import lax
from jax.experimental import pallas as pl
from jax.experimental.pallas import tpu as pltpu

EPS = 1e-6
N_HEADS = 8
HEAD_DIM = 64
N_KV_HEADS = 2
KV_GROUP = N_HEADS // N_KV_HEADS
D_ATTN = N_HEADS * HEAD_DIM
D_KV = N_KV_HEADS * HEAD_DIM
N_IDX_HEADS = 8
D_IDX = 64
TOPK_MAX = 256
Q_BLOCK = 128
ROPE_THETA = 10000.0
POOL_WINDOWS = (2, 4, 8, 16)
POOL_BUF = 15
PAGE_SIZE = 128
CHUNK = 128
N_SGU_HEADS = 8
D_POOL_GROUP = 128

LANES = 128
SUBLANES = 8
VMEM_LIMIT = 56 * 1024 * 1024
INT_MIN = -2 ** 31
NEG_BIG = -1e30
LOG2E = 1.4426950408889634

F32 = jnp.float32
BF16 = jnp.bfloat16
I32 = jnp.int32

_NT = (((1,), (1,)), ((), ()))


def _cparams(sem):
    return pltpu.CompilerParams(dimension_semantics=sem, vmem_limit_bytes=VMEM_LIMIT)


def _rms(x, g):
    return x * lax.rsqrt(jnp.mean(x * x, axis=-1, keepdims=True) + EPS) * g


def _dot(a, b):
    return jnp.dot(a, b, preferred_element_type=F32)


def _dot_nt(a, b):
    return lax.dot_general(a, b, _NT, preferred_element_type=F32)


def _gelu(x):
    return 0.5 * x * (1.0 + lax.erf(x * (2.0 ** -0.5)))


C_Q, C_IQ, C_IK2, C_K, C_XP, C_END = 0, 512, 1024, 1280, 1408, 1920
R_K, R_V, R_IK, R_END = 0, 128, 256, 320


def _rope_tile(t, cos, sin, first_half):
    partner = jnp.where(first_half, pltpu.roll(t, 96, 1), pltpu.roll(t, 32, 1))
    return t * cos + partner * sin


def _rope_rows(t, cos_t, sin_t):
    half = HEAD_DIM // 2
    x1, x2 = t[0:half], t[half:HEAD_DIM]
    return jnp.concatenate([x1 * cos_t - x2 * sin_t, x2 * cos_t + x1 * sin_t], axis=0)


def _even_in_kernel(*refs, tm, decode, past_len):
    if decode:
        (x_ref, g_ref, w_ref, wt_ref, wiwt_ref, cos_ref, sin_ref, cost_ref, sint_ref, wpg_ref, psc_ref, pre_ref,
         q_ref, iq_ref, ik2_ref, kb_ref, kt_ref, vt_ref, vtb_ref, ikt_ref, iwt_ref, pool_ref, xp_ref,
         ext_s) = refs
    else:
        (x_ref, g_ref, w_ref, wt_ref, wiwt_ref, cos_ref, sin_ref, cost_ref, sint_ref, wpg_ref, psc_ref,
         q_ref, iq_ref, ik2_ref, kb_ref, kt_ref, vt_ref, vtb_ref, ikt_ref, iwt_ref, pool_ref, xp_ref,
         ext_s) = refs
    j = pl.program_id(1)
    xn = _rms(x_ref[...], g_ref[...]).astype(BF16)
    cos = cos_ref[...]
    sin = sin_ref[...]
    lane = lax.broadcasted_iota(I32, (1, LANES), 1)
    first_half = (lane % HEAD_DIM) < (HEAD_DIM // 2)

    def proj_rope(c0, c1):
        t = _dot(xn, w_ref[:, c0:c1])
        return [_rope_tile(t[:, c:c + LANES], cos, sin, first_half) for c in range(0, c1 - c0, LANES)]

    for half in range(2):
        tiles = proj_rope(C_Q + 256 * half, C_Q + 256 * (half + 1))
        for c, t in enumerate(tiles):
            col = 256 * half + LANES * c
            q_ref[:, col:col + LANES] = (t * (HEAD_DIM ** -0.5 * LOG2E)).astype(BF16)
    for half in range(2):
        tiles = proj_rope(C_IQ + 256 * half, C_IQ + 256 * (half + 1))
        for c, t in enumerate(tiles):
            col = 256 * half + LANES * c
            iq_ref[:, col:col + LANES] = t.astype(BF16)
    tiles = proj_rope(C_IK2, C_K)
    ik2_ref[:, 0:LANES] = tiles[0].astype(BF16)
    ik2_ref[:, LANES:2 * LANES] = tiles[1].astype(BF16)
    kb_ref[...] = proj_rope(C_K, C_XP)[0].astype(BF16)

    cos_t = cost_ref[...]
    sin_t = sint_ref[...]
    ft = _dot_nt(wt_ref[...], xn)
    kt = jnp.concatenate([_rope_rows(ft[R_K + h * HEAD_DIM:R_K + (h + 1) * HEAD_DIM], cos_t, sin_t)
                          for h in range(N_KV_HEADS)], axis=0)
    vt = ft[R_V:R_IK]
    kt_ref[0] = kt
    vt_ref[0] = vt
    vtb_ref[0] = vt.astype(BF16)
    ikt_ref[0] = _rope_rows(ft[R_IK:R_END], cos_t, sin_t)
    iwt_ref[0] = _dot_nt(wiwt_ref[...], xn) * (N_IDX_HEADS ** -0.5) * (D_IDX ** -0.5)

    xp = _dot(xn, w_ref[:, C_XP:C_END])
    row = lax.broadcasted_iota(I32, (tm, 1), 0)
    if decode:
        ns = tm // SUBLANES
        ext_s[:, 0:16, :] = pre_ref[...]
        ext_s[:, 16:24, :] = xp.reshape(ns, SUBLANES, 4 * D_POOL_GROUP)
        pos = past_len + (row % SUBLANES)
    else:
        @pl.when(j == 0)
        def _():
            ext_s[0:16, :] = jnp.zeros((16, 4 * D_POOL_GROUP), F32)
        ext_s[16:16 + tm, :] = xp
        pos = j * tm + row
    for g, w in enumerate(POOL_WINDOWS):
        cs = slice(g * D_POOL_GROUP, (g + 1) * D_POOL_GROUP)
        tok = xp[:, cs]
        acc = tok
        for i in range(1, w):
            if decode:
                acc = acc + ext_s[:, 16 - i:24 - i, cs].reshape(tm, D_POOL_GROUP)
            else:
                acc = acc + ext_s[16 - i:16 - i + tm, cs]
        cnt = jnp.minimum(w, pos + 1).astype(F32)
        d = acc / cnt - tok
        y = _dot(d.astype(BF16), wpg_ref[g]) * psc_ref[:, cs]
        pool_ref[:, cs] = y.astype(BF16)
    if decode:
        xp_ref[...] = xp
    else:
        ext_s[0:16, :] = xp[tm - 16:tm, :]
        xp_ref[0] = xp[tm - 16:tm, :]


def _rope_tables(pos):
    half = HEAD_DIM // 2
    inv = ROPE_THETA ** (-np.arange(half, dtype=np.float64) / half)
    ang = pos.astype(np.float64)[:, None] * inv[None, :]
    cos32, sin32 = np.cos(ang).astype(np.float32), np.sin(ang).astype(np.float32)
    cos = np.tile(cos32, (1, LANES // half))
    sin = np.tile(np.concatenate([-sin32, sin32], axis=1), (1, LANES // HEAD_DIM))
    return tuple(jnp.asarray(a) for a in (cos, sin, np.ascontiguousarray(cos32.T), np.ascontiguousarray(sin32.T)))


def _arrange_w_in_even(w):
    q = w[:, 0:512].reshape(-1, N_HEADS, HEAD_DIM)
    q = jnp.stack([q[:, 0:4], q[:, 4:8]], axis=2).reshape(-1, 512)
    k = w[:, 512:640]
    v = w[:, 640:768]
    iq = w[:, 768:1280]
    ik = w[:, 1280:1344]
    iw = w[:, 1344:1352]
    xp = w[:, 1352:1864]
    z = jnp.zeros_like(ik)
    main = jnp.concatenate([q, iq, ik, z, z, ik, k, xp], axis=1).astype(BF16)
    feat = jnp.concatenate([k, v, ik], axis=1).T.astype(BF16)
    return main, feat, iw.T.astype(BF16)


def _even_in(x, g, wmain, wfeat, wiwt, tabs, wpg, psc, prefix, *, nb, tm, past_len):
    n, d = x.shape
    t = n // nb
    nt = t // tm
    decode = prefix is not None
    cos, sin, cos_t, sin_t = tabs
    row = lambda b, j: (b * nt + j, 0)
    tab_row = (lambda b, j: (0, 0)) if decode else (lambda b, j: (j, 0))
    tab_col = (lambda b, j: (0, 0)) if decode else (lambda b, j: (0, j))
    const = lambda b, j: (0, 0)
    feat = lambda b, j: (b, 0, j)
    in_specs = [
        pl.BlockSpec((tm, d), row),
        pl.BlockSpec((1, d), const),
        pl.BlockSpec(wmain.shape, const),
        pl.BlockSpec(wfeat.shape, const),
        pl.BlockSpec(wiwt.shape, const),
        pl.BlockSpec((tm, LANES), tab_row),
        pl.BlockSpec((tm, LANES), tab_row),
        pl.BlockSpec((HEAD_DIM // 2, tm), tab_col),
        pl.BlockSpec((HEAD_DIM // 2, tm), tab_col),
        pl.BlockSpec(wpg.shape, lambda b, j: (0, 0, 0)),
        pl.BlockSpec((1, 512), const),
    ]
    args = [x, g, wmain, wfeat, wiwt, cos, sin, cos_t, sin_t, wpg, psc]
    if decode:
        ns = tm // SUBLANES
        in_specs.append(pl.BlockSpec((ns, 16, 512), lambda b, j: (b * nt + j, 0, 0)))
        args.append(prefix)
        xp_shape = jax.ShapeDtypeStruct((n, 512), F32)
        xp_spec = pl.BlockSpec((tm, 512), row)
        scratch = [pltpu.VMEM((ns, 24, 512), F32)]
    else:
        xp_shape = jax.ShapeDtypeStruct((nb, 16, 512), F32)
        xp_spec = pl.BlockSpec((1, 16, 512), lambda b, j: (b, 0, 0))
        scratch = [pltpu.VMEM((tm + 16, 512), F32)]
    out_shape = [
        jax.ShapeDtypeStruct((n, 512), BF16),
        jax.ShapeDtypeStruct((n, 512), BF16),
        jax.ShapeDtypeStruct((n, 256), BF16),
        jax.ShapeDtypeStruct((n, 128), BF16),
        jax.ShapeDtypeStruct((nb, 128, t), F32),
        jax.ShapeDtypeStruct((nb, 128, t), F32),
        jax.ShapeDtypeStruct((nb, 128, t), BF16),
        jax.ShapeDtypeStruct((nb, D_IDX, t), F32),
        jax.ShapeDtypeStruct((nb, 8, t), F32),
        jax.ShapeDtypeStruct((n, 512), BF16),
        xp_shape,
    ]
    out_specs = [
        pl.BlockSpec((tm, 512), row),
        pl.BlockSpec((tm, 512), row),
        pl.BlockSpec((tm, 256), row),
        pl.BlockSpec((tm, 128), row),
        pl.BlockSpec((1, 128, tm), feat),
        pl.BlockSpec((1, 128, tm), feat),
        pl.BlockSpec((1, 128, tm), feat),
        pl.BlockSpec((1, D_IDX, tm), feat),
        pl.BlockSpec((1, 8, tm), feat),
        pl.BlockSpec((tm, 512), row),
        xp_spec,
    ]
    return pl.pallas_call(
        functools.partial(_even_in_kernel, tm=tm, decode=decode, past_len=past_len),
        grid=(nb, nt), in_specs=in_specs, out_specs=out_specs, out_shape=out_shape,
        scratch_shapes=scratch, compiler_params=_cparams(("arbitrary", "arbitrary")),
        name="even_in_decode" if decode else "even_in_prompt",
    )(*args)


F32_MIN_NORMAL = 2.0 ** -126


def _sort_key(s):
    s = jnp.where(jnp.abs(s) < F32_MIN_NORMAL, 0.0, s)
    b = lax.bitcast_convert_type(s, I32)
    return b ^ ((b >> 31) & 0x7FFFFFFF)


def _key_top16_as_float(key):
    b = key ^ ((key >> 31) & 0x7FFFFFFF)
    return lax.bitcast_convert_type(b & jnp.int32(-65536), F32)


def _class16_as_float(c):
    c = jnp.where(jnp.logical_and(c >= 1, c <= 127), 128, c)
    c = jnp.where(jnp.logical_and(c >= -256, c <= -1), 0, c)
    p = jnp.where(c < 0, c ^ 0x7FFF, c) & 0xFFFF
    return lax.bitcast_convert_type(p << 16, F32)


LOWEST_REAL_CLASS = -32640


def _topk_search_packed(count16, count32, build_digits, shape, n_index_bits, topk):
    def greedy(nbits, start, to_float, k_needed):
        def step(i, prefix):
            cand = prefix + jnp.left_shift(jnp.int32(1), nbits - 1 - i)
            c = count16(to_float(cand), False)
            return jnp.where(c >= k_needed, cand, prefix)
        return lax.fori_loop(0, nbits, step, jnp.full(shape, start, I32))

    as_float = lambda d: d.astype(F32)
    t1 = greedy(16, -32768, _class16_as_float, topk)
    short = t1 < LOWEST_REAL_CLASS
    gt1 = count16(_class16_as_float(t1), True)
    build_digits(8, t1)
    d2 = greedy(8, 0, as_float, topk - gt1)
    gt2 = count16(as_float(d2), True)
    build_digits(0, t1 * 256 + d2)
    d3 = greedy(8, 0, as_float, topk - gt1 - gt2)
    ge3 = count16(as_float(d3), False)
    gt3 = count16(as_float(d3), True)
    thr = jnp.where(short, INT_MIN, t1 * 65536 + d2 * 256 + d3)
    need = topk - (gt1 + gt2 + gt3)
    ambiguous = jnp.logical_and(ge3 - gt3 > need, jnp.logical_not(short))

    def tie_search():
        def step(i, lo):
            cand = lo + jnp.left_shift(jnp.int32(1), n_index_bits - 1 - i)
            c = count32(lambda kk, idx: jnp.logical_and(kk == thr, idx < cand))
            return jnp.where(c < need, cand, lo)
        return lax.fori_loop(0, n_index_bits, step, jnp.zeros(shape, I32))

    big = jnp.full(shape, 2 ** 30, I32)
    any_amb = jnp.max(ambiguous.astype(I32)) > 0
    cut = lax.cond(any_amb, lambda: jnp.where(ambiguous, tie_search(), big), lambda: big)
    return thr, cut


def _topk_search(count, shape, n_index_bits, topk):
    def bit_step(i, carry):
        prefix, c_ge = carry
        cand = prefix + jnp.left_shift(jnp.int32(1), 31 - i)
        c = count(lambda kk, idx: kk >= cand)
        ok = c >= topk
        return jnp.where(ok, cand, prefix), jnp.where(ok, c, c_ge)

    thr, c_ge = lax.fori_loop(0, 32, bit_step, (jnp.full(shape, INT_MIN, I32), jnp.zeros(shape, I32)))
    c_gt = count(lambda kk, idx: kk > thr)
    c_eq = c_ge - c_gt
    need = topk - c_gt
    ambiguous = jnp.logical_and(c_eq > need, thr != INT_MIN)

    def tie_search():
        def step(i, lo):
            cand = lo + jnp.left_shift(jnp.int32(1), n_index_bits - 1 - i)
            c = count(lambda kk, idx: jnp.logical_and(kk == thr, idx < cand))
            return jnp.where(c < need, cand, lo)
        return lax.fori_loop(0, n_index_bits, step, jnp.zeros(shape, I32))

    big = jnp.full(shape, 2 ** 30, I32)
    any_amb = jnp.max(ambiguous.astype(I32)) > 0
    cut = lax.cond(any_amb, lambda: jnp.where(ambiguous, tie_search(), big), lambda: big)
    return thr, cut


def _index_bits(n):
    return max(1, int(np.ceil(np.log2(n))))


def _fold_rows(x, op):
    return op(x.reshape(x.shape[0] // SUBLANES, SUBLANES, x.shape[1]), axis=0)


def _dsa_prompt_kernel(q_ref, iq_ref, iwt_ref, kb_ref, vt_ref, ik2_ref, o_ref,
                       keys_s, tab_s, qm_s, iqs_s, acc_s, *, tk, topk):
    i = pl.program_id(1)
    n_tiles = (i * Q_BLOCK + Q_BLOCK + tk - 1) // tk
    lane = lax.broadcasted_iota(I32, (1, LANES), 1)
    qpos = i * Q_BLOCK + lane
    qpos_l = qpos
    lo_half = lane < HEAD_DIM
    nq = KV_GROUP * Q_BLOCK

    for jj in range(KV_GROUP):
        qt = q_ref[:, jj * LANES:(jj + 1) * LANES]
        zero = jnp.zeros_like(qt)
        qm_s[0, jj * Q_BLOCK:(jj + 1) * Q_BLOCK, :] = jnp.where(lo_half, qt, zero)
        qm_s[1, jj * Q_BLOCK:(jj + 1) * Q_BLOCK, :] = jnp.where(lo_half, zero, qt)
        iqs_s[jj * Q_BLOCK:(jj + 1) * Q_BLOCK, :] = iq_ref[:, jj * LANES:(jj + 1) * LANES]
    w = iwt_ref[0]

    def score_tile(t, carry):
        r0 = pl.multiple_of(t * tk, tk)
        ik2 = ik2_ref[pl.ds(r0, tk), :]
        iqs = iqs_s[...]
        rel_e = _dot_nt(ik2[:, 0:LANES], iqs)
        rel_o = _dot_nt(ik2[:, LANES:2 * LANES], iqs)
        s = jnp.zeros((tk, LANES), F32)
        for jj in range(4):
            cs = slice(jj * LANES, (jj + 1) * LANES)
            s = s + w[2 * jj:2 * jj + 1, :] * jnp.maximum(rel_e[:, cs], 0.0)
            s = s + w[2 * jj + 1:2 * jj + 2, :] * jnp.maximum(rel_o[:, cs], 0.0)
        kpos = r0 + lax.broadcasted_iota(I32, (tk, 1), 0)
        valid = kpos <= qpos_l
        key = _sort_key(s)
        keys_s[pl.ds(r0, tk), :] = jnp.where(valid, key, INT_MIN)
        tab_s[pl.ds(r0, tk), :] = jnp.where(valid, _key_top16_as_float(key), -jnp.inf).astype(BF16)
        return carry

    lax.fori_loop(0, n_tiles, score_tile, 0)

    def count32(pred):
        def body(t, acc):
            r0 = pl.multiple_of(t * tk, tk)
            kk = keys_s[pl.ds(r0, tk), :]
            idx = r0 + lax.broadcasted_iota(I32, (tk, 1), 0)
            return acc + _fold_rows(pred(kk, idx).astype(I32), jnp.sum)
        acc = lax.fori_loop(0, n_tiles, body, jnp.zeros((SUBLANES, LANES), I32))
        return acc.sum(axis=0, keepdims=True)

    pack = 2 * SUBLANES
    one, zero = jnp.ones((), BF16), jnp.zeros((), BF16)

    def count16(cand, strict):
        cb = cand.astype(BF16)
        def body(t, acc):
            r0 = pl.multiple_of(t * tk, tk)
            a = tab_s[pl.ds(r0, tk), :]
            m = jnp.where((a > cb) if strict else (a >= cb), one, zero)
            parts = [m[r:r + pack] for r in range(0, tk, pack)]
            while len(parts) > 1:
                parts = [parts[a2] + parts[a2 + 1] for a2 in range(0, len(parts), 2)]
            return acc + parts[0]
        acc = lax.fori_loop(0, n_tiles, body, jnp.zeros((pack, LANES), BF16))
        return acc.astype(F32).sum(axis=0, keepdims=True).astype(I32)

    def build_digits(shift, prefix):
        def body(t, carry):
            r0 = pl.multiple_of(t * tk, tk)
            kk = keys_s[pl.ds(r0, tk), :]
            digit = ((kk >> shift) & 255).astype(F32)
            tab_s[pl.ds(r0, tk), :] = jnp.where((kk >> (shift + 8)) == prefix, digit, -1.0).astype(BF16)
            return carry
        lax.fori_loop(0, n_tiles, body, 0)

    thr, cut = _topk_search_packed(count16, count32, build_digits, (1, LANES),
                                   _index_bits(keys_s.shape[0]), topk)

    def masked_logits(t):
        r0 = pl.multiple_of(t * tk, tk)
        kk = keys_s[pl.ds(r0, tk), :]
        kpos = r0 + lax.broadcasted_iota(I32, (tk, 1), 0)
        sel = jnp.logical_or(kk > thr, jnp.logical_and(kk == thr, kpos <= cut))
        sel = jnp.logical_and(sel, kpos <= qpos)
        sel = jnp.concatenate([sel] * KV_GROUP, axis=1)
        kt = kb_ref[pl.ds(r0, tk), :]
        return r0, [jnp.where(sel, _dot_nt(kt, qm_s[g]), NEG_BIG) for g in range(N_KV_HEADS)]

    acc_s[...] = jnp.zeros(acc_s.shape, F32)

    def pv_tile(t, carry):
        ms, ls = carry
        r0, st = masked_logits(t)
        ms_new, ls_new = [], []
        for g in range(N_KV_HEADS):
            m_new = jnp.maximum(ms[g], jnp.max(_fold_rows(st[g], jnp.max), axis=0, keepdims=True))
            alpha = jnp.exp2(ms[g] - m_new)
            p = jnp.exp2(st[g] - m_new)
            ls_new.append(ls[g] * alpha + _fold_rows(p, jnp.sum))
            ms_new.append(m_new)
            vt = vt_ref[0, g * HEAD_DIM:(g + 1) * HEAD_DIM, pl.ds(r0, tk)]
            acc_s[g] = acc_s[g] * alpha + _dot(vt, p.astype(BF16))
        return tuple(ms_new), tuple(ls_new)

    _, ls = lax.fori_loop(
        0, n_tiles, pv_tile,
        (tuple(jnp.full((1, nq), NEG_BIG, F32) for _ in range(N_KV_HEADS)),
         tuple(jnp.zeros((SUBLANES, nq), F32) for _ in range(N_KV_HEADS))))
    heads = []
    for g in range(N_KV_HEADS):
        og = acc_s[g] / jnp.sum(ls[g], axis=0, keepdims=True)
        heads += [og[:, jj * Q_BLOCK:(jj + 1) * Q_BLOCK] for jj in range(KV_GROUP)]
    o_ref[...] = jnp.concatenate(heads, axis=0).T.astype(o_ref.dtype)


def _store_heads(o_ref, heads, lo_half):
    for jj in range(N_HEADS // 2):
        a, b = heads[2 * jj], heads[2 * jj + 1]
        if 2 * jj < KV_GROUP:
            tile = jnp.where(lo_half, a, pltpu.roll(b, HEAD_DIM, 1))
        else:
            tile = jnp.where(lo_half, pltpu.roll(a, HEAD_DIM, 1), b)
        o_ref[:, jj * LANES:(jj + 1) * LANES] = tile.astype(o_ref.dtype)


def _dsa_prompt(q, iq, iwt, kb, vtb, ik2, *, nb, tk, topk):
    n = q.shape[0]
    t = n // nb
    nq = t // Q_BLOCK
    assert t % tk == 0 and t // (2 * SUBLANES) <= 256
    blk = lambda b, i: (b * nq + i, 0)
    seq = lambda b, i: (b, 0)
    feat = lambda b, i: (b, 0, 0)
    return pl.pallas_call(
        functools.partial(_dsa_prompt_kernel, tk=tk, topk=topk),
        grid=(nb, nq),
        in_specs=[
            pl.BlockSpec((Q_BLOCK, 512), blk),
            pl.BlockSpec((Q_BLOCK, 512), blk),
            pl.BlockSpec((1, 8, Q_BLOCK), lambda b, i: (b, 0, i)),
            pl.BlockSpec((t, 128), seq),
            pl.BlockSpec((1, 128, t), feat),
            pl.BlockSpec((t, 256), seq),
        ],
        out_specs=pl.BlockSpec((Q_BLOCK, 512), blk),
        out_shape=jax.ShapeDtypeStruct((n, 512), BF16),
        scratch_shapes=[
            pltpu.VMEM((t, LANES), I32),
            pltpu.VMEM((t, LANES), BF16),
            pltpu.VMEM((N_KV_HEADS, KV_GROUP * Q_BLOCK, LANES), BF16),
            pltpu.VMEM((4 * Q_BLOCK, LANES), BF16),
            pltpu.VMEM((N_KV_HEADS, HEAD_DIM, KV_GROUP * Q_BLOCK), F32),
        ],
        compiler_params=_cparams(("arbitrary", "arbitrary")),
        name="dsa_prompt",
    )(q, iq, iwt, kb, vtb, ik2)


SEQ_GROUP = Q_BLOCK // SUBLANES


def _pad_rows(a):
    return jnp.concatenate([a, jnp.zeros((LANES - a.shape[0], a.shape[1]), a.dtype)], axis=0).astype(BF16)


def _decode_select_kernel(*refs, n_pages, topk, nnew):
    ip = refs[1:1 + n_pages]
    iqd_ref, wcol_ref, inew_ref, bias_ref, keys_s = refs[1 + n_pages:]
    t = pl.program_id(1)
    n_keys = keys_s.shape[1]
    past = n_pages * PAGE_SIZE
    n_tiles = n_keys // LANES
    lane = lax.broadcasted_iota(I32, (1, LANES), 1)
    qrow = lax.broadcasted_iota(I32, (nnew, 1), 0)
    new_valid = jnp.logical_and(lane <= qrow, lane < nnew)

    def head_sum(x):
        return x.reshape(N_IDX_HEADS, nnew, x.shape[1]).sum(axis=0)

    r0 = pl.multiple_of(t * nnew, nnew)
    iqd = iqd_ref[0]
    wcol = wcol_ref[0]
    for p in range(n_pages):
        rel = _dot(iqd, ip[p][0].astype(BF16))
        s = head_sum(wcol * jnp.maximum(rel, 0.0))
        keys_s[pl.ds(r0, nnew), p * LANES:(p + 1) * LANES] = _sort_key(s)
    rel = _dot_nt(iqd, _pad_rows(inew_ref[0]))
    s = head_sum(wcol * jnp.maximum(rel, 0.0))
    keys_s[pl.ds(r0, nnew), past:past + LANES] = jnp.where(new_valid, _sort_key(s), INT_MIN)

    @pl.when(t == SEQ_GROUP - 1)
    def _search():
        def count(pred):
            acc = jnp.zeros((Q_BLOCK, LANES), I32)
            for c in range(n_tiles):
                idx = c * LANES + lane
                acc = acc + pred(keys_s[:, c * LANES:(c + 1) * LANES], idx).astype(I32)
            return acc.sum(axis=1, keepdims=True)
        thr, cut = _topk_search(count, (Q_BLOCK, 1), _index_bits(n_keys), topk)
        for c in range(n_tiles):
            kk = keys_s[:, c * LANES:(c + 1) * LANES]
            sel = jnp.logical_or(kk > thr, jnp.logical_and(kk == thr, c * LANES + lane <= cut))
            sel = jnp.logical_and(sel, kk != INT_MIN)
            bias_ref[:, c * LANES:(c + 1) * LANES] = jnp.where(sel, 0.0, NEG_BIG)


def _decode_attend_kernel(*refs, n_pages, nnew):
    kp = refs[1:1 + n_pages]
    vp = refs[1 + n_pages:1 + 2 * n_pages]
    qd_ref, knew_ref, vnew_ref, bias_ref, o_ref = refs[1 + 2 * n_pages:]
    past = n_pages * PAGE_SIZE
    lane = lax.broadcasted_iota(I32, (1, LANES), 1)
    bias = jnp.concatenate([bias_ref[...]] * N_HEADS, axis=0)
    qd = qd_ref[0]
    st = jnp.concatenate([_dot(qd, kp[p][0].astype(BF16)) for p in range(n_pages)]
                         + [_dot_nt(qd, _pad_rows(knew_ref[0]))], axis=1) + bias
    m = jnp.max(st, axis=1, keepdims=True)
    pr = jnp.exp2(st - m)
    l = jnp.sum(pr, axis=1, keepdims=True)
    pb = pr.astype(BF16)
    o = _dot(pb[:, past:past + LANES], _pad_rows(vnew_ref[0]))
    for p in range(n_pages):
        o = o + _dot_nt(pb[:, p * LANES:(p + 1) * LANES], vp[p][0].astype(BF16))
    o = o / l
    _store_heads(o_ref, [o[h * nnew:(h + 1) * nnew] for h in range(N_HEADS)], lane < HEAD_DIM)


def _dsa_decode(page_table, ci_t, ck_t, cv_t, iqd, wcol, inew, qd, knew, vnew, *, topk):
    ns, n_pages = page_table.shape
    nnew = knew.shape[1]
    assert ns % SEQ_GROUP == 0 and nnew == SUBLANES
    n_keys = (n_pages + 1) * PAGE_SIZE

    sel_page = lambda p: pl.BlockSpec((1, D_IDX, PAGE_SIZE),
                                      lambda g, t, pt, p=p: (pt[g * SEQ_GROUP + t, p], 0, 0))
    sel3 = lambda g, t, pt: (g * SEQ_GROUP + t, 0, 0)
    bias = pl.pallas_call(
        functools.partial(_decode_select_kernel, n_pages=n_pages, topk=topk, nnew=nnew),
        grid_spec=pltpu.PrefetchScalarGridSpec(
            num_scalar_prefetch=1, grid=(ns // SEQ_GROUP, SEQ_GROUP),
            in_specs=([sel_page(p) for p in range(n_pages)]
                      + [pl.BlockSpec((1, N_IDX_HEADS * nnew, D_IDX), sel3),
                         pl.BlockSpec((1, N_IDX_HEADS * nnew, 1), sel3),
                         pl.BlockSpec((1, nnew, D_IDX), sel3)]),
            out_specs=pl.BlockSpec((Q_BLOCK, n_keys), lambda g, t, pt: (g, 0)),
            scratch_shapes=[pltpu.VMEM((Q_BLOCK, n_keys), I32)]),
        out_shape=jax.ShapeDtypeStruct((ns * nnew, n_keys), F32),
        compiler_params=_cparams(("arbitrary", "arbitrary")),
        name="dsa_decode_select",
    )(page_table, *([ci_t] * n_pages), iqd, wcol, inew)

    att_page = lambda p: pl.BlockSpec((1, D_KV, PAGE_SIZE), lambda s, pt, p=p: (pt[s, p], 0, 0))
    att3 = lambda s, pt: (s, 0, 0)
    return pl.pallas_call(
        functools.partial(_decode_attend_kernel, n_pages=n_pages, nnew=nnew),
        grid_spec=pltpu.PrefetchScalarGridSpec(
            num_scalar_prefetch=1, grid=(ns,),
            in_specs=([att_page(p) for p in range(n_pages)] + [att_page(p) for p in range(n_pages)]
                      + [pl.BlockSpec((1, N_HEADS * nnew, D_KV), att3),
                         pl.BlockSpec((1, nnew, D_KV), att3),
                         pl.BlockSpec((1, nnew, D_KV), att3),
                         pl.BlockSpec((nnew, n_keys), lambda s, pt: (s, 0))]),
            out_specs=pl.BlockSpec((nnew, D_ATTN), lambda s, pt: (s, 0))),
        out_shape=jax.ShapeDtypeStruct((ns * nnew, D_ATTN), F32),
        compiler_params=_cparams(("arbitrary",)),
        name="dsa_decode_attend",
    )(page_table, *([ck_t] * n_pages), *([cv_t] * n_pages), qd, knew, vnew, bias)


FF_CHUNK = 256


def _ffn_block(h, gpre, gpost, wgu_ref, wd_ref, a_s):
    d_ff = wd_ref.shape[0]
    hn = _rms(h, gpre).astype(BF16)
    for c in range(0, d_ff, FF_CHUNK):
        gate = _dot(hn, wgu_ref[:, c:c + FF_CHUNK])
        up = _dot(hn, wgu_ref[:, d_ff + c:d_ff + c + FF_CHUNK])
        a_s[:, c:c + FF_CHUNK] = (gate * jax.nn.sigmoid(gate) * up).astype(BF16)
    return h + _rms(_dot(a_s[...], wd_ref[...]), gpost)


def _even_out_kernel(x_ref, attn_ref, pool_ref, wo_ref, gmix_ref, gpre_ref, gpost_ref, wgu_ref, wd_ref,
                     o_ref, a_s):
    mix = _dot(attn_ref[...], wo_ref[0:D_ATTN, :]) + _dot(pool_ref[...], wo_ref[D_ATTN:, :])
    h = x_ref[...] + _rms(mix, gmix_ref[...])
    o_ref[...] = _ffn_block(h, gpre_ref[...], gpost_ref[...], wgu_ref, wd_ref, a_s)


def _even_out(x, attn, pool, wo, gmix, gpre, gpost, wgu, wd, *, tm):
    n, d = x.shape
    row = lambda i: (i, 0)
    const = lambda i: (0, 0)
    return pl.pallas_call(
        _even_out_kernel, grid=(n // tm,),
        in_specs=[pl.BlockSpec((tm, d), row), pl.BlockSpec((tm, 512), row), pl.BlockSpec((tm, 512), row),
                  pl.BlockSpec(wo.shape, const), pl.BlockSpec((1, d), const), pl.BlockSpec((1, d), const),
                  pl.BlockSpec((1, d), const), pl.BlockSpec(wgu.shape, const), pl.BlockSpec(wd.shape, const)],
        out_specs=pl.BlockSpec((tm, d), row),
        out_shape=jax.ShapeDtypeStruct((n, d), F32),
        scratch_shapes=[pltpu.VMEM((tm, wd.shape[0]), BF16)],
        compiler_params=_cparams(("arbitrary",)),
        name="even_out_ffn",
    )(x, attn, pool, wo, gmix, gpre, gpost, wgu, wd)


def _odd_kernel(x_ref, g_ref, win_ref, lng_ref, lnb_ref, ws_ref, bs_ref, wout_ref, gmix_ref,
                gpre_ref, gpost_ref, wgu_ref, wd_ref, o_ref, z_ref, y_s, a_s, *, tm, seq_len):
    d_sgu = wout_ref.shape[0]
    dh = d_sgu // N_SGU_HEADS
    x = x_ref[...]
    xn = _rms(x, g_ref[...]).astype(BF16)
    r = lax.broadcasted_iota(I32, (CHUNK, CHUNK), 0)
    c = lax.broadcasted_iota(I32, (CHUNK, CHUNK), 1)
    causal = jnp.logical_and(r // seq_len == c // seq_len, c <= r)
    for hd in range(N_SGU_HEADS):
        cs = slice(hd * dh, (hd + 1) * dh)
        u = _gelu(_dot(xn, win_ref[:, cs]))
        y_s[:, cs] = u.astype(BF16)
    vs = []
    for hd in range(N_SGU_HEADS):
        cs = slice(d_sgu + hd * dh, d_sgu + (hd + 1) * dh)
        vs.append(_gelu(_dot(xn, win_ref[:, cs])))
    v = jnp.concatenate(vs, axis=1)
    mu = jnp.mean(v, axis=-1, keepdims=True)
    vc = v - mu
    z = vc * lax.rsqrt(jnp.mean(vc * vc, axis=-1, keepdims=True) + EPS) * lng_ref[...] + lnb_ref[...]
    z_ref[...] = z
    zb = z.astype(BF16)
    for hd in range(N_SGU_HEADS):
        cs = slice(hd * dh, (hd + 1) * dh)
        wm = jnp.where(causal, ws_ref[hd], 0.0).astype(BF16)
        bias = bs_ref[hd]
        for ch in range(tm // CHUNK):
            rs = slice(ch * CHUNK, (ch + 1) * CHUNK)
            s = _dot(wm, zb[rs, cs]) + bias
            y_s[rs, cs] = (y_s[rs, cs].astype(F32) * s).astype(BF16)
    h = x + _rms(_dot(y_s[...], wout_ref[...]), gmix_ref[...])
    o_ref[...] = _ffn_block(h, gpre_ref[...], gpost_ref[...], wgu_ref, wd_ref, a_s)


def _odd(x, g, win, lng, lnb, ws, bs, wout, gmix, gpre, gpost, wgu, wd, *, tm, seq_len):
    n, d = x.shape
    d_sgu = wout.shape[0]
    row = lambda i: (i, 0)
    const = lambda i: (0, 0)
    const3 = lambda i: (0, 0, 0)
    return pl.pallas_call(
        functools.partial(_odd_kernel, tm=tm, seq_len=seq_len), grid=(n // tm,),
        in_specs=[pl.BlockSpec((tm, d), row), pl.BlockSpec((1, d), const), pl.BlockSpec(win.shape, const),
                  pl.BlockSpec((1, d_sgu), const), pl.BlockSpec((1, d_sgu), const),
                  pl.BlockSpec(ws.shape, const3), pl.BlockSpec(bs.shape, const3),
                  pl.BlockSpec(wout.shape, const), pl.BlockSpec((1, d), const), pl.BlockSpec((1, d), const),
                  pl.BlockSpec((1, d), const), pl.BlockSpec(wgu.shape, const), pl.BlockSpec(wd.shape, const)],
        out_specs=[pl.BlockSpec((tm, d), row), pl.BlockSpec((tm, d_sgu), row)],
        out_shape=[jax.ShapeDtypeStruct((n, d), F32), jax.ShapeDtypeStruct((n, d_sgu), F32)],
        scratch_shapes=[pltpu.VMEM((tm, d_sgu), BF16), pltpu.VMEM((tm, wd.shape[0]), BF16)],
        compiler_params=_cparams(("arbitrary",)),
        name="odd_mixer_ffn",
    )(x, g, win, lng, lnb, ws, bs, wout, gmix, gpre, gpost, wgu, wd)


def _tile_rows(n, want):
    tm = min(want, n)
    while n % tm:
        tm //= 2
    return tm


def _from_feature_major(a, heads):
    b, f, t = a.shape
    if heads is None:
        return jnp.transpose(a, (0, 2, 1))
    return jnp.transpose(a.reshape(b, heads, f // heads, t), (0, 3, 1, 2))


def kernel(x_prompt, x_sample, cache_k, cache_v, cache_idx_k, state_pool, page_table, norm_mix_pre, norm_mix_post, norm_ffn_pre, norm_ffn_post, w_in_even, w_out_even, w_pool_group, pool_scale, w_in_odd, sgu_norm_g, sgu_norm_b, w_spatial, b_spatial, w_out_odd, w_ffn_gate_up, w_ffn_down):
    bp, tp, d = x_prompt.shape
    bs, ts, _ = x_sample.shape
    n_pages = page_table.shape[1]
    past_len = n_pages * PAGE_SIZE
    topk_p = min(TOPK_MAX, tp // 4)
    topk_s = min(TOPK_MAX, (past_len + ts) // 4)
    depth = norm_mix_pre.shape[0]
    assert ts == SUBLANES and tp % Q_BLOCK == 0 and d % LANES == 0

    hp = x_prompt.reshape(bp * tp, d)
    hs = x_sample.reshape(bs * ts, d)
    tm_p = _tile_rows(bp * tp, 512)
    tm_p_seq = _tile_rows(tp, 512)
    tm_s = _tile_rows(bs * ts, 256)
    row = lambda a: a.reshape(1, -1)

    tabs_p = _rope_tables(np.arange(tp))
    tabs_s = _rope_tables(np.tile(past_len + np.arange(ts), tm_s // ts))

    outs_p, outs_s, sgu_s = [], [], []
    for layer in range(depth):
        li = layer // 2
        gpre, gpost = row(norm_ffn_pre[layer]), row(norm_ffn_post[layer])
        wgu = w_ffn_gate_up[layer].astype(BF16)
        wd = w_ffn_down[layer].astype(BF16)
        gmix_pre, gmix_post = row(norm_mix_pre[layer]), row(norm_mix_post[layer])
        if layer % 2 == 0:
            wmain, wfeat, wiwt = _arrange_w_in_even(w_in_even[li])
            wpg = w_pool_group[li].astype(BF16)
            psc = row(pool_scale[li])
            wo = w_out_even[li].astype(BF16)
            (q, iq, ik2, kb, kt, vt, vtb, ikt, iwt, pool, xp_tail) = _even_in(
                hp, gmix_pre, wmain, wfeat, wiwt, tabs_p, wpg, psc, None, nb=bp, tm=tm_p_seq, past_len=0)
            attn = _dsa_prompt(q, iq, iwt, kb, vtb, ik2, nb=bp, tk=min(512, tp), topk=topk_p)
            hp = _even_out(hp, attn, pool, wo, gmix_post, gpre, gpost, wgu, wd, tm=tm_p)
            outs_p.append((_from_feature_major(kt, N_KV_HEADS), _from_feature_major(vt, N_KV_HEADS),
                           _from_feature_major(ikt, None), xp_tail[:, 1:, :]))
            prefix = jnp.pad(state_pool[li], ((0, 0), (1, 0), (0, 0)))
            (q, iq, ik2, kb, kt, vt, vtb, ikt, iwt, pool, xp) = _even_in(
                hs, gmix_pre, wmain, wfeat, wiwt, tabs_s, wpg, psc, prefix, nb=1, tm=tm_s, past_len=past_len)
            del kb, vtb
            k_new = _from_feature_major(kt.reshape(1, D_KV, bs * ts), None).reshape(bs, ts, D_KV)
            v_new = _from_feature_major(vt.reshape(1, D_KV, bs * ts), None).reshape(bs, ts, D_KV)
            ik_new = _from_feature_major(ikt, None).reshape(bs, ts, D_IDX)
            q3 = q.reshape(bs, ts, 4, 2, HEAD_DIM)
            zq = jnp.zeros_like(q3[:, :, :, 0])
            qg = jnp.concatenate([jnp.concatenate([q3[:, :, :, 0], zq], axis=-1),
                                  jnp.concatenate([zq, q3[:, :, :, 1]], axis=-1)], axis=2)
            qd = jnp.transpose(qg, (0, 2, 1, 3)).reshape(bs, N_HEADS * ts, D_KV)
            iqd = jnp.transpose(iq.reshape(bs, ts, N_IDX_HEADS, D_IDX), (0, 2, 1, 3)).reshape(bs, N_IDX_HEADS * ts, D_IDX)
            wcol = jnp.transpose(iwt[0].reshape(N_IDX_HEADS, bs, ts), (1, 0, 2)).reshape(bs, N_IDX_HEADS * ts, 1)
            ci_t = jnp.transpose(cache_idx_k[li], (0, 2, 1))
            ck_t = jnp.transpose(cache_k[li], (0, 2, 3, 1)).reshape(-1, D_KV, PAGE_SIZE)
            cv_t = jnp.transpose(cache_v[li], (0, 2, 3, 1)).reshape(-1, D_KV, PAGE_SIZE)
            attn_s = _dsa_decode(page_table, ci_t, ck_t, cv_t, iqd, wcol, ik_new, qd, k_new, v_new, topk=topk_s)
            hs = _even_out(hs, attn_s.astype(BF16), pool, wo, gmix_post, gpre, gpost, wgu, wd, tm=tm_s)
            new_pool_s = jnp.concatenate([state_pool[li][:, ts:, :], xp.reshape(bs, ts, -1)], axis=1)
            outs_s.append((k_new.reshape(bs, ts, N_KV_HEADS, HEAD_DIM), v_new.reshape(bs, ts, N_KV_HEADS, HEAD_DIM),
                           ik_new, new_pool_s))
        else:
            win = w_in_odd[li].astype(BF16)
            wout = w_out_odd[li].astype(BF16)
            lng, lnb = row(sgu_norm_g[li]), row(sgu_norm_b[li])
            ws_p = w_spatial[li][:, :CHUNK, :CHUNK]
            bs_p = b_spatial[li][:, :CHUNK, None]
            hp, _ = _odd(hp, gmix_pre, win, lng, lnb, ws_p, bs_p, wout, gmix_post, gpre, gpost, wgu, wd,
                         tm=tm_p, seq_len=CHUNK)
            reps = CHUNK // ts
            ws_s = jnp.tile(w_spatial[li][:, :ts, :ts], (1, reps, reps))
            bs_s = jnp.tile(b_spatial[li][:, :ts], (1, reps))[:, :, None]
            hs, zs = _odd(hs, gmix_pre, win, lng, lnb, ws_s, bs_s, wout, gmix_post, gpre, gpost, wgu, wd,
                          tm=_tile_rows(bs * ts, 256), seq_len=ts)
            sgu_s.append(zs.reshape(bs, ts, -1))

    return (hp.reshape(bp, tp, d), hs.reshape(bs, ts, d),
            jnp.stack([o[0] for o in outs_p]), jnp.stack([o[1] for o in outs_p]),
            jnp.stack([o[2] for o in outs_p]), jnp.stack([o[3] for o in outs_p]),
            jnp.stack([o[0] for o in outs_s]), jnp.stack([o[1] for o in outs_s]),
            jnp.stack([o[2] for o in outs_s]), jnp.stack([o[3] for o in outs_s]),
            jnp.stack(sgu_s))
```

```python
import functools

import jax
import jax.numpy as jnp
import numpy as np
from jax import lax
from jax.experimental import pallas as pl
from jax.experimental.pallas import tpu as pltpu

EPS = 1e-6
N_HEADS = 8
HEAD_DIM = 64
N_KV_HEADS = 2
KV_GROUP = N_HEADS // N_KV_HEADS
D_ATTN = N_HEADS * HEAD_DIM
D_KV = N_KV_HEADS * HEAD_DIM
N_IDX_HEADS = 8
D_IDX = 64
TOPK_MAX = 256
Q_BLOCK = 128
ROPE_THETA = 10000.0
POOL_WINDOWS = (2, 4, 8, 16)
POOL_BUF = 15
PAGE_SIZE = 128
CHUNK = 128
N_SGU_HEADS = 8
D_POOL_GROUP = 128
V_ROWS = HEAD_DIM + 16

LANES = 128
SUBLANES = 8
VMEM_LIMIT = 56 * 1024 * 1024
INT_MIN = -2 ** 31
NEG_BIG = -1e30
LOG2E = 1.4426950408889634

F32 = jnp.float32
BF16 = jnp.bfloat16
I32 = jnp.int32

_NT = (((1,), (1,)), ((), ()))


def _cparams(sem):
    return pltpu.CompilerParams(dimension_semantics=sem, vmem_limit_bytes=VMEM_LIMIT)


def _rms(x, g):
    return x * lax.rsqrt(jnp.mean(x * x, axis=-1, keepdims=True) + EPS) * g


def _dot(a, b):
    return jnp.dot(a, b, preferred_element_type=F32)


def _dot_nt(a, b):
    return lax.dot_general(a, b, _NT, preferred_element_type=F32)


def _gelu(x):
    return 0.5 * x * (1.0 + lax.erf(x * (2.0 ** -0.5)))


C_Q, C_IQ, C_IK2, C_K, C_XP, C_END = 0, 512, 1024, 1280, 1408, 1920
R_K, R_V, R_IK, R_END = 0, 128, 256, 320


def _rope_tile(t, cos, sin, first_half):
    partner = jnp.where(first_half, pltpu.roll(t, 96, 1), pltpu.roll(t, 32, 1))
    return t * cos + partner * sin


def _rope_rows(t, cos_t, sin_t):
    half = HEAD_DIM // 2
    x1, x2 = t[0:half], t[half:HEAD_DIM]
    return jnp.concatenate([x1 * cos_t - x2 * sin_t, x2 * cos_t + x1 * sin_t], axis=0)


def _even_in_kernel(*refs, tm, decode, past_len):
    if decode:
        (x_ref, g_ref, w_ref, wt_ref, wiwt_ref, cos_ref, sin_ref, cost_ref, sint_ref, wpg_ref, psc_ref, pre_ref,
         q_ref, iq_ref, ik2_ref, kb_ref, kt_ref, vt_ref, vtb_ref, ikt_ref, iwt_ref, pool_ref, xp_ref,
         ext_s) = refs
    else:
        (x_ref, g_ref, w_ref, wt_ref, wiwt_ref, cos_ref, sin_ref, cost_ref, sint_ref, wpg_ref, psc_ref,
         q_ref, iq_ref, ik2_ref, kb_ref, kt_ref, vt_ref, vtb_ref, ikt_ref, iwt_ref, pool_ref, xp_ref,
         ext_s) = refs
    j = pl.program_id(1)
    xn = _rms(x_ref[...], g_ref[...]).astype(BF16)
    cos = cos_ref[...]
    sin = sin_ref[...]
    lane = lax.broadcasted_iota(I32, (1, LANES), 1)
    first_half = (lane % HEAD_DIM) < (HEAD_DIM // 2)

    def proj_rope(c0, c1):
        t = _dot(xn, w_ref[:, c0:c1])
        return [_rope_tile(t[:, c:c + LANES], cos, sin, first_half) for c in range(0, c1 - c0, LANES)]

    for half in range(2):
        tiles = proj_rope(C_Q + 256 * half, C_Q + 256 * (half + 1))
        for c, t in enumerate(tiles):
            col = 256 * half + LANES * c
            q_ref[:, col:col + LANES] = (t * (HEAD_DIM ** -0.5 * LOG2E)).astype(BF16)
    for half in range(2):
        tiles = proj_rope(C_IQ + 256 * half, C_IQ + 256 * (half + 1))
        for c, t in enumerate(tiles):
            col = 256 * half + LANES * c
            iq_ref[:, col:col + LANES] = t.astype(BF16)
    tiles = proj_rope(C_IK2, C_K)
    ik2_ref[:, 0:LANES] = tiles[0].astype(BF16)
    ik2_ref[:, LANES:2 * LANES] = tiles[1].astype(BF16)
    kb_ref[...] = proj_rope(C_K, C_XP)[0].astype(BF16)

    cos_t = cost_ref[...]
    sin_t = sint_ref[...]
    ft = _dot_nt(wt_ref[...], xn)
    kt = jnp.concatenate([_rope_rows(ft[R_K + h * HEAD_DIM:R_K + (h + 1) * HEAD_DIM], cos_t, sin_t)
                          for h in range(N_KV_HEADS)], axis=0)
    vt = ft[R_V:R_IK]
    kt_ref[0] = kt
    vt_ref[0] = vt
    pad = jnp.concatenate([jnp.ones((1, tm), F32), jnp.zeros((V_ROWS - HEAD_DIM - 1, tm), F32)], axis=0)
    for h in range(N_KV_HEADS):
        vtb_ref[0, h * V_ROWS:h * V_ROWS + HEAD_DIM, :] = vt[h * HEAD_DIM:(h + 1) * HEAD_DIM].astype(BF16)
        vtb_ref[0, h * V_ROWS + HEAD_DIM:(h + 1) * V_ROWS, :] = pad.astype(BF16)
    ikt_ref[0] = _rope_rows(ft[R_IK:R_END], cos_t, sin_t)
    iwt_ref[0] = _dot_nt(wiwt_ref[...], xn) * (N_IDX_HEADS ** -0.5) * (D_IDX ** -0.5)

    xp = _dot(xn, w_ref[:, C_XP:C_END])
    row = lax.broadcasted_iota(I32, (tm, 1), 0)
    if decode:
        ns = tm // SUBLANES
        ext_s[:, 0:16, :] = pre_ref[...]
        ext_s[:, 16:24, :] = xp.reshape(ns, SUBLANES, 4 * D_POOL_GROUP)
        pos = past_len + (row % SUBLANES)
    else:
        @pl.when(j == 0)
        def _():
            ext_s[0:16, :] = jnp.zeros((16, 4 * D_POOL_GROUP), F32)
        ext_s[16:16 + tm, :] = xp
        pos = j * tm + row
    for g, w in enumerate(POOL_WINDOWS):
        cs = slice(g * D_POOL_GROUP, (g + 1) * D_POOL_GROUP)
        tok = xp[:, cs]
        acc = tok
        for i in range(1, w):
            if decode:
                acc = acc + ext_s[:, 16 - i:24 - i, cs].reshape(tm, D_POOL_GROUP)
            else:
                acc = acc + ext_s[16 - i:16 - i + tm, cs]
        cnt = jnp.minimum(w, pos + 1).astype(F32)
        d = acc / cnt - tok
        y = _dot(d.astype(BF16), wpg_ref[g]) * psc_ref[:, cs]
        pool_ref[:, cs] = y.astype(BF16)
    if decode:
        xp_ref[...] = xp
    else:
        ext_s[0:16, :] = xp[tm - 16:tm, :]
        xp_ref[0] = xp[tm - 16:tm, :]


def _rope_tables(pos):
    half = HEAD_DIM // 2
    inv = ROPE_THETA ** (-np.arange(half, dtype=np.float64) / half)
    ang = pos.astype(np.float64)[:, None] * inv[None, :]
    cos32, sin32 = np.cos(ang).astype(np.float32), np.sin(ang).astype(np.float32)
    cos = np.tile(cos32, (1, LANES // half))
    sin = np.tile(np.concatenate([-sin32, sin32], axis=1), (1, LANES // HEAD_DIM))
    return tuple(jnp.asarray(a) for a in (cos, sin, np.ascontiguousarray(cos32.T), np.ascontiguousarray(sin32.T)))


def _arrange_w_in_even(w):
    q = w[:, 0:512].reshape(-1, N_HEADS, HEAD_DIM)
    q = jnp.stack([q[:, 0:4], q[:, 4:8]], axis=2).reshape(-1, 512)
    k = w[:, 512:640]
    v = w[:, 640:768]
    iq = w[:, 768:1280]
    ik = w[:, 1280:1344]
    iw = w[:, 1344:1352]
    xp = w[:, 1352:1864]
    z = jnp.zeros_like(ik)
    main = jnp.concatenate([q, iq, ik, z, z, ik, k, xp], axis=1).astype(BF16)
    feat = jnp.concatenate([k, v, ik], axis=1).T.astype(BF16)
    return main, feat, iw.T.astype(BF16)


def _even_in(x, g, wmain, wfeat, wiwt, tabs, wpg, psc, prefix, *, nb, tm, past_len):
    n, d = x.shape
    t = n // nb
    nt = t // tm
    decode = prefix is not None
    cos, sin, cos_t, sin_t = tabs
    row = lambda b, j: (b * nt + j, 0)
    tab_row = (lambda b, j: (0, 0)) if decode else (lambda b, j: (j, 0))
    tab_col = (lambda b, j: (0, 0)) if decode else (lambda b, j: (0, j))
    const = lambda b, j: (0, 0)
    feat = lambda b, j: (b, 0, j)
    in_specs = [
        pl.BlockSpec((tm, d), row),
        pl.BlockSpec((1, d), const),
        pl.BlockSpec(wmain.shape, const),
        pl.BlockSpec(wfeat.shape, const),
        pl.BlockSpec(wiwt.shape, const),
        pl.BlockSpec((tm, LANES), tab_row),
        pl.BlockSpec((tm, LANES), tab_row),
        pl.BlockSpec((HEAD_DIM // 2, tm), tab_col),
        pl.BlockSpec((HEAD_DIM // 2, tm), tab_col),
        pl.BlockSpec(wpg.shape, lambda b, j: (0, 0, 0)),
        pl.BlockSpec((1, 512), const),
    ]
    args = [x, g, wmain, wfeat, wiwt, cos, sin, cos_t, sin_t, wpg, psc]
    if decode:
        ns = tm // SUBLANES
        in_specs.append(pl.BlockSpec((ns, 16, 512), lambda b, j: (b * nt + j, 0, 0)))
        args.append(prefix)
        xp_shape = jax.ShapeDtypeStruct((n, 512), F32)
        xp_spec = pl.BlockSpec((tm, 512), row)
        scratch = [pltpu.VMEM((ns, 24, 512), F32)]
    else:
        xp_shape = jax.ShapeDtypeStruct((nb, 16, 512), F32)
        xp_spec = pl.BlockSpec((1, 16, 512), lambda b, j: (b, 0, 0))
        scratch = [pltpu.VMEM((tm + 16, 512), F32)]
    out_shape = [
        jax.ShapeDtypeStruct((n, 512), BF16),
        jax.ShapeDtypeStruct((n, 512), BF16),
        jax.ShapeDtypeStruct((n, 256), BF16),
        jax.ShapeDtypeStruct((n, 128), BF16),
        jax.ShapeDtypeStruct((nb, 128, t), F32),
        jax.ShapeDtypeStruct((nb, 128, t), F32),
        jax.ShapeDtypeStruct((nb, N_KV_HEADS * V_ROWS, t), BF16),
        jax.ShapeDtypeStruct((nb, D_IDX, t), F32),
        jax.ShapeDtypeStruct((nb, 8, t), F32),
        jax.ShapeDtypeStruct((n, 512), BF16),
        xp_shape,
    ]
    out_specs = [
        pl.BlockSpec((tm, 512), row),
        pl.BlockSpec((tm, 512), row),
        pl.BlockSpec((tm, 256), row),
        pl.BlockSpec((tm, 128), row),
        pl.BlockSpec((1, 128, tm), feat),
        pl.BlockSpec((1, 128, tm), feat),
        pl.BlockSpec((1, N_KV_HEADS * V_ROWS, tm), feat),
        pl.BlockSpec((1, D_IDX, tm), feat),
        pl.BlockSpec((1, 8, tm), feat),
        pl.BlockSpec((tm, 512), row),
        xp_spec,
    ]
    return pl.pallas_call(
        functools.partial(_even_in_kernel, tm=tm, decode=decode, past_len=past_len),
        grid=(nb, nt), in_specs=in_specs, out_specs=out_specs, out_shape=out_shape,
        scratch_shapes=scratch, compiler_params=_cparams(("arbitrary", "arbitrary")),
        name="even_in_decode" if decode else "even_in_prompt",
    )(*args)


F32_MIN_NORMAL = 2.0 ** -126
F32_LOWEST = -3.4028234663852886e38
LOWEST_FINITE_CODE = -2139095040


def _score_for_search(s, valid):
    return jnp.where(valid, jnp.where(jnp.abs(s) < F32_MIN_NORMAL, 0.0, s), -jnp.inf)


def _code_to_float(code):
    b = code ^ ((code >> 31) & 0x7FFFFFFF)
    f = lax.bitcast_convert_type(b, F32)
    tiny = ((b >> 23) & 0xFF) == 0
    return jnp.where(tiny, jnp.where(b > 0, F32_MIN_NORMAL, 0.0), f)


FRACTION_BITS = 24


def _topk_search(count, shape, n_index_bits, topk):
    def bit_step(i, carry):
        code, c_ge = carry
        cand = code + jnp.left_shift(jnp.int32(1), 31 - i)
        cf = _code_to_float(cand)
        c = count(lambda sv, idx: sv >= cf)
        ok = c >= topk
        return jnp.where(ok, cand, code), jnp.where(ok, c, c_ge)

    code, c_ge = lax.fori_loop(0, 32, bit_step, (jnp.full(shape, INT_MIN, I32), jnp.zeros(shape, I32)))
    short = code < LOWEST_FINITE_CODE
    thr = jnp.where(short, -jnp.inf, _code_to_float(code))
    nxt = jnp.where(short, F32_LOWEST, _code_to_float(code + 1))
    c_above = count(lambda sv, idx: sv >= nxt)
    need = topk - c_above
    ambiguous = jnp.logical_and(c_ge - c_above > need, jnp.logical_not(short))
    width = nxt - thr
    inv_width = jnp.where(jnp.logical_and(width >= F32_MIN_NORMAL, width < jnp.inf), 1.0 / width, 0.0)

    def frac(sv):
        return (sv - thr) * inv_width

    def in_band(sv):
        return jnp.logical_and(sv >= thr, sv < nxt)

    def split_band():
        def frac_step(i, m):
            cand = m + jnp.left_shift(jnp.int32(1), FRACTION_BITS - 1 - i)
            cf = cand.astype(F32) * (2.0 ** -FRACTION_BITS)
            c = count(lambda sv, idx: jnp.logical_and(in_band(sv), frac(sv) >= cf))
            return jnp.where(c >= need, cand, m)
        m = lax.fori_loop(0, FRACTION_BITS, frac_step, jnp.zeros(shape, I32))
        rstar = m.astype(F32) * (2.0 ** -FRACTION_BITS)
        need_eq = need - count(lambda sv, idx: jnp.logical_and(in_band(sv), frac(sv) > rstar))

        def idx_step(i, lo):
            cand = lo + jnp.left_shift(jnp.int32(1), n_index_bits - 1 - i)
            c = count(lambda sv, idx: jnp.logical_and(jnp.logical_and(in_band(sv), frac(sv) == rstar), idx < cand))
            return jnp.where(c < need_eq, cand, lo)
        cut = lax.fori_loop(0, n_index_bits, idx_step, jnp.zeros(shape, I32))
        return jnp.where(ambiguous, rstar, -1.0), jnp.where(ambiguous, cut, 2 ** 30)

    any_amb = jnp.max(ambiguous.astype(I32)) > 0
    rstar, cut = lax.cond(any_amb, split_band,
                          lambda: (jnp.full(shape, -1.0, F32), jnp.full(shape, 2 ** 30, I32)))

    def select(sv, idx):
        r = frac(sv)
        taken = jnp.logical_or(r > rstar, jnp.logical_and(r == rstar, idx <= cut))
        return jnp.logical_or(sv >= nxt, jnp.logical_and(in_band(sv), taken))

    return select, jnp.maximum(thr, F32_LOWEST), any_amb


def _index_bits(n):
    return max(1, int(np.ceil(np.log2(n))))


def _fold_rows(x, op):
    return op(x.reshape(x.shape[0] // SUBLANES, SUBLANES, x.shape[1]), axis=0)


def _dsa_prompt_kernel(q_ref, iq_ref, iwt_ref, kb_ref, vt_ref, ik2_ref, o_ref,
                       keys_s, qm_s, iqs_s, acc_s, *, tk, topk):
    i = pl.program_id(1)
    n_tiles = (i * Q_BLOCK + Q_BLOCK + tk - 1) // tk
    lane = lax.broadcasted_iota(I32, (1, LANES), 1)
    qpos = i * Q_BLOCK + lane
    qpos_l = qpos
    lo_half = lane < HEAD_DIM
    nq = KV_GROUP * Q_BLOCK

    for jj in range(KV_GROUP):
        qt = q_ref[:, jj * LANES:(jj + 1) * LANES]
        zero = jnp.zeros_like(qt)
        qm_s[0, jj * Q_BLOCK:(jj + 1) * Q_BLOCK, :] = jnp.where(lo_half, qt, zero)
        qm_s[1, jj * Q_BLOCK:(jj + 1) * Q_BLOCK, :] = jnp.where(lo_half, zero, qt)
        iqs_s[jj * Q_BLOCK:(jj + 1) * Q_BLOCK, :] = iq_ref[:, jj * LANES:(jj + 1) * LANES]
    w = iwt_ref[0]

    def score_tile(t, carry):
        r0 = pl.multiple_of(t * tk, tk)
        ik2 = ik2_ref[pl.ds(r0, tk), :]
        iqs = iqs_s[...]
        rel_e = _dot_nt(ik2[:, 0:LANES], iqs)
        rel_o = _dot_nt(ik2[:, LANES:2 * LANES], iqs)
        s = jnp.zeros((tk, LANES), F32)
        for jj in range(4):
            cs = slice(jj * LANES, (jj + 1) * LANES)
            s = s + w[2 * jj:2 * jj + 1, :] * jnp.maximum(rel_e[:, cs], 0.0)
            s = s + w[2 * jj + 1:2 * jj + 2, :] * jnp.maximum(rel_o[:, cs], 0.0)
        kpos = r0 + lax.broadcasted_iota(I32, (tk, 1), 0)
        keys_s[pl.ds(r0, tk), :] = _score_for_search(s, kpos <= qpos_l)
        return carry

    lax.fori_loop(0, n_tiles, score_tile, 0)

    def count(pred):
        def body(t, acc):
            r0 = pl.multiple_of(t * tk, tk)
            sv = keys_s[pl.ds(r0, tk), :]
            idx = r0 + lax.broadcasted_iota(I32, (tk, 1), 0)
            return acc + _fold_rows(pred(sv, idx).astype(I32), jnp.sum)
        acc = lax.fori_loop(0, n_tiles, body, jnp.zeros((SUBLANES, LANES), I32))
        return acc.sum(axis=0, keepdims=True)

    select, thr_ge, any_cut = _topk_search(count, (1, LANES), _index_bits(keys_s.shape[0]), topk)

    def attend(with_cuts):
        def pv_tile(t, ms):
            r0 = pl.multiple_of(t * tk, tk)
            sv = keys_s[pl.ds(r0, tk), :]
            if with_cuts:
                sel = select(sv, r0 + lax.broadcasted_iota(I32, (tk, 1), 0))
            else:
                sel = sv >= thr_ge
            sel = jnp.concatenate([sel] * KV_GROUP, axis=1)
            kt = kb_ref[pl.ds(r0, tk), :]
            sts = [jnp.where(sel, _dot_nt(kt, qm_s[g]), NEG_BIG) for g in range(N_KV_HEADS)]
            ms_new = []
            for g, st in enumerate(sts):
                m_new = jnp.maximum(ms[g], jnp.max(_fold_rows(st, jnp.max), axis=0, keepdims=True))
                alpha = jnp.exp2(ms[g] - m_new)
                p = jnp.exp2(st - m_new)
                ms_new.append(m_new)
                vt = vt_ref[0, g * V_ROWS:(g + 1) * V_ROWS, pl.ds(r0, tk)]
                acc_s[g] = acc_s[g] * alpha + _dot(vt, p.astype(BF16))
            return tuple(ms_new)

        lax.fori_loop(0, n_tiles, pv_tile, tuple(jnp.full((1, nq), NEG_BIG, F32) for _ in range(N_KV_HEADS)))

    acc_s[...] = jnp.zeros(acc_s.shape, F32)
    lax.cond(any_cut, lambda: attend(True), lambda: attend(False))
    heads = []
    for g in range(N_KV_HEADS):
        og = acc_s[g, 0:HEAD_DIM, :] / acc_s[g, HEAD_DIM:HEAD_DIM + 1, :]
        heads += [og[:, jj * Q_BLOCK:(jj + 1) * Q_BLOCK] for jj in range(KV_GROUP)]
    o_ref[...] = jnp.concatenate(heads, axis=0).T.astype(o_ref.dtype)


def _store_heads(o_ref, heads, lo_half):
    for jj in range(N_HEADS // 2):
        a, b = heads[2 * jj], heads[2 * jj + 1]
        if 2 * jj < KV_GROUP:
            tile = jnp.where(lo_half, a, pltpu.roll(b, HEAD_DIM, 1))
        else:
            tile = jnp.where(lo_half, pltpu.roll(a, HEAD_DIM, 1), b)
        o_ref[:, jj * LANES:(jj + 1) * LANES] = tile.astype(o_ref.dtype)


def _dsa_prompt(q, iq, iwt, kb, vtb, ik2, *, nb, tk, topk):
    n = q.shape[0]
    t = n // nb
    nq = t // Q_BLOCK
    assert t % tk == 0
    blk = lambda b, i: (b * nq + i, 0)
    seq = lambda b, i: (b, 0)
    feat = lambda b, i: (b, 0, 0)
    return pl.pallas_call(
        functools.partial(_dsa_prompt_kernel, tk=tk, topk=topk),
        grid=(nb, nq),
        in_specs=[
            pl.BlockSpec((Q_BLOCK, 512), blk),
            pl.BlockSpec((Q_BLOCK, 512), blk),
            pl.BlockSpec((1, 8, Q_BLOCK), lambda b, i: (b, 0, i)),
            pl.BlockSpec((t, 128), seq),
            pl.BlockSpec((1, N_KV_HEADS * V_ROWS, t), feat),
            pl.BlockSpec((t, 256), seq),
        ],
        out_specs=pl.BlockSpec((Q_BLOCK, 512), blk),
        out_shape=jax.ShapeDtypeStruct((n, 512), BF16),
        scratch_shapes=[
            pltpu.VMEM((t, LANES), F32),
            pltpu.VMEM((N_KV_HEADS, KV_GROUP * Q_BLOCK, LANES), BF16),
            pltpu.VMEM((4 * Q_BLOCK, LANES), BF16),
            pltpu.VMEM((N_KV_HEADS, V_ROWS, KV_GROUP * Q_BLOCK), F32),
        ],
        compiler_params=_cparams(("arbitrary", "arbitrary")),
        name="dsa_prompt",
    )(q, iq, iwt, kb, vtb, ik2)


SEQ_GROUP = Q_BLOCK // SUBLANES


def _pad_rows(a):
    return jnp.concatenate([a, jnp.zeros((LANES - a.shape[0], a.shape[1]), a.dtype)], axis=0).astype(BF16)


def _decode_select_kernel(*refs, n_pages, topk, nnew):
    ip = refs[1:1 + n_pages]
    iqd_ref, wcol_ref, inew_ref, bias_ref, keys_s = refs[1 + n_pages:]
    t = pl.program_id(1)
    n_keys = keys_s.shape[1]
    past = n_pages * PAGE_SIZE
    n_tiles = n_keys // LANES
    lane = lax.broadcasted_iota(I32, (1, LANES), 1)
    qrow = lax.broadcasted_iota(I32, (nnew, 1), 0)
    new_valid = jnp.logical_and(lane <= qrow, lane < nnew)

    def head_sum(x):
        return x.reshape(N_IDX_HEADS, nnew, x.shape[1]).sum(axis=0)

    r0 = pl.multiple_of(t * nnew, nnew)
    iqd = iqd_ref[0]
    wcol = wcol_ref[0]
    for p in range(n_pages):
        rel = _dot(iqd, ip[p][0].astype(BF16))
        s = head_sum(wcol * jnp.maximum(rel, 0.0))
        keys_s[pl.ds(r0, nnew), p * LANES:(p + 1) * LANES] = _score_for_search(s, True)
    rel = _dot_nt(iqd, _pad_rows(inew_ref[0]))
    s = head_sum(wcol * jnp.maximum(rel, 0.0))
    keys_s[pl.ds(r0, nnew), past:past + LANES] = _score_for_search(s, new_valid)

    @pl.when(t == SEQ_GROUP - 1)
    def _search():
        def count(pred):
            acc = jnp.zeros((Q_BLOCK, LANES), I32)
            for c in range(n_tiles):
                idx = c * LANES + lane
                acc = acc + pred(keys_s[:, c * LANES:(c + 1) * LANES], idx).astype(I32)
            return acc.sum(axis=1, keepdims=True)
        select, _, _ = _topk_search(count, (Q_BLOCK, 1), _index_bits(n_keys), topk)
        for c in range(n_tiles):
            sel = select(keys_s[:, c * LANES:(c + 1) * LANES], c * LANES + lane)
            bias_ref[:, c * LANES:(c + 1) * LANES] = jnp.where(sel, 0.0, NEG_BIG)


def _decode_attend_kernel(*refs, n_pages, nnew):
    kp = refs[1:1 + n_pages]
    vp = refs[1 + n_pages:1 + 2 * n_pages]
    qd_ref, knew_ref, vnew_ref, bias_ref, o_ref = refs[1 + 2 * n_pages:]
    past = n_pages * PAGE_SIZE
    lane = lax.broadcasted_iota(I32, (1, LANES), 1)
    bias = jnp.concatenate([bias_ref[...]] * N_HEADS, axis=0)
    qd = qd_ref[0]
    st = jnp.concatenate([_dot(qd, kp[p][0].astype(BF16)) for p in range(n_pages)]
                         + [_dot_nt(qd, _pad_rows(knew_ref[0]))], axis=1) + bias
    m = jnp.max(st, axis=1, keepdims=True)
    pr = jnp.exp2(st - m)
    l = jnp.sum(pr, axis=1, keepdims=True)
    pb = pr.astype(BF16)
    o = _dot(pb[:, past:past + LANES], _pad_rows(vnew_ref[0]))
    for p in range(n_pages):
        o = o + _dot_nt(pb[:, p * LANES:(p + 1) * LANES], vp[p][0].astype(BF16))
    o = o / l
    _store_heads(o_ref, [o[h * nnew:(h + 1) * nnew] for h in range(N_HEADS)], lane < HEAD_DIM)


def _dsa_decode(page_table, ci_t, ck_t, cv_t, iqd, wcol, inew, qd, knew, vnew, *, topk):
    ns, n_pages = page_table.shape
    nnew = knew.shape[1]
    assert ns % SEQ_GROUP == 0 and nnew == SUBLANES
    n_keys = (n_pages + 1) * PAGE_SIZE

    sel_page = lambda p: pl.BlockSpec((1, D_IDX, PAGE_SIZE),
                                      lambda g, t, pt, p=p: (pt[g * SEQ_GROUP + t, p], 0, 0))
    sel3 = lambda g, t, pt: (g * SEQ_GROUP + t, 0, 0)
    bias = pl.pallas_call(
        functools.partial(_decode_select_kernel, n_pages=n_pages, topk=topk, nnew=nnew),
        grid_spec=pltpu.PrefetchScalarGridSpec(
            num_scalar_prefetch=1, grid=(ns // SEQ_GROUP, SEQ_GROUP),
            in_specs=([sel_page(p) for p in range(n_pages)]
                      + [pl.BlockSpec((1, N_IDX_HEADS * nnew, D_IDX), sel3),
                         pl.BlockSpec((1, N_IDX_HEADS * nnew, 1), sel3),
                         pl.BlockSpec((1, nnew, D_IDX), sel3)]),
            out_specs=pl.BlockSpec((Q_BLOCK, n_keys), lambda g, t, pt: (g, 0)),
            scratch_shapes=[pltpu.VMEM((Q_BLOCK, n_keys), F32)]),
        out_shape=jax.ShapeDtypeStruct((ns * nnew, n_keys), F32),
        compiler_params=_cparams(("arbitrary", "arbitrary")),
        name="dsa_decode_select",
    )(page_table, *([ci_t] * n_pages), iqd, wcol, inew)

    att_page = lambda p: pl.BlockSpec((1, D_KV, PAGE_SIZE), lambda s, pt, p=p: (pt[s, p], 0, 0))
    att3 = lambda s, pt: (s, 0, 0)
    return pl.pallas_call(
        functools.partial(_decode_attend_kernel, n_pages=n_pages, nnew=nnew),
        grid_spec=pltpu.PrefetchScalarGridSpec(
            num_scalar_prefetch=1, grid=(ns,),
            in_specs=([att_page(p) for p in range(n_pages)] + [att_page(p) for p in range(n_pages)]
                      + [pl.BlockSpec((1, N_HEADS * nnew, D_KV), att3),
                         pl.BlockSpec((1, nnew, D_KV), att3),
                         pl.BlockSpec((1, nnew, D_KV), att3),
                         pl.BlockSpec((nnew, n_keys), lambda s, pt: (s, 0))]),
            out_specs=pl.BlockSpec((nnew, D_ATTN), lambda s, pt: (s, 0))),
        out_shape=jax.ShapeDtypeStruct((ns * nnew, D_ATTN), F32),
        compiler_params=_cparams(("arbitrary",)),
        name="dsa_decode_attend",
    )(page_table, *([ck_t] * n_pages), *([cv_t] * n_pages), qd, knew, vnew, bias)


FF_CHUNK = 256


def _ffn_block(h, gpre, gpost, wgu_ref, wd_ref, a_s):
    d_ff = wd_ref.shape[0]
    hn = _rms(h, gpre).astype(BF16)
    for c in range(0, d_ff, FF_CHUNK):
        gate = _dot(hn, wgu_ref[:, c:c + FF_CHUNK])
        up = _dot(hn, wgu_ref[:, d_ff + c:d_ff + c + FF_CHUNK])
        a_s[:, c:c + FF_CHUNK] = (gate * jax.nn.sigmoid(gate) * up).astype(BF16)
    return h + _rms(_dot(a_s[...], wd_ref[...]), gpost)


def _even_out_kernel(x_ref, attn_ref, pool_ref, wo_ref, gmix_ref, gpre_ref, gpost_ref, wgu_ref, wd_ref,
                     o_ref, a_s):
    mix = _dot(attn_ref[...], wo_ref[0:D_ATTN, :]) + _dot(pool_ref[...], wo_ref[D_ATTN:, :])
    h = x_ref[...] + _rms(mix, gmix_ref[...])
    o_ref[...] = _ffn_block(h, gpre_ref[...], gpost_ref[...], wgu_ref, wd_ref, a_s)


def _even_out(x, attn, pool, wo, gmix, gpre, gpost, wgu, wd, *, tm):
    n, d = x.shape
    row = lambda i: (i, 0)
    const = lambda i: (0, 0)
    return pl.pallas_call(
        _even_out_kernel, grid=(n // tm,),
        in_specs=[pl.BlockSpec((tm, d), row), pl.BlockSpec((tm, 512), row), pl.BlockSpec((tm, 512), row),
                  pl.BlockSpec(wo.shape, const), pl.BlockSpec((1, d), const), pl.BlockSpec((1, d), const),
                  pl.BlockSpec((1, d), const), pl.BlockSpec(wgu.shape, const), pl.BlockSpec(wd.shape, const)],
        out_specs=pl.BlockSpec((tm, d), row),
        out_shape=jax.ShapeDtypeStruct((n, d), F32),
        scratch_shapes=[pltpu.VMEM((tm, wd.shape[0]), BF16)],
        compiler_params=_cparams(("arbitrary",)),
        name="even_out_ffn",
    )(x, attn, pool, wo, gmix, gpre, gpost, wgu, wd)


def _odd_kernel(x_ref, g_ref, win_ref, lng_ref, lnb_ref, ws_ref, bs_ref, wout_ref, gmix_ref,
                gpre_ref, gpost_ref, wgu_ref, wd_ref, o_ref, z_ref, y_s, a_s, *, tm, seq_len):
    d_sgu = wout_ref.shape[0]
    dh = d_sgu // N_SGU_HEADS
    x = x_ref[...]
    xn = _rms(x, g_ref[...]).astype(BF16)
    r = lax.broadcasted_iota(I32, (CHUNK, CHUNK), 0)
    c = lax.broadcasted_iota(I32, (CHUNK, CHUNK), 1)
    causal = jnp.logical_and(r // seq_len == c // seq_len, c <= r)
    for hd in range(N_SGU_HEADS):
        cs = slice(hd * dh, (hd + 1) * dh)
        u = _gelu(_dot(xn, win_ref[:, cs]))
        y_s[:, cs] = u.astype(BF16)
    vs = []
    for hd in range(N_SGU_HEADS):
        cs = slice(d_sgu + hd * dh, d_sgu + (hd + 1) * dh)
        vs.append(_gelu(_dot(xn, win_ref[:, cs])))
    v = jnp.concatenate(vs, axis=1)
    mu = jnp.mean(v, axis=-1, keepdims=True)
    vc = v - mu
    z = vc * lax.rsqrt(jnp.mean(vc * vc, axis=-1, keepdims=True) + EPS) * lng_ref[...] + lnb_ref[...]
    z_ref[...] = z
    zb = z.astype(BF16)
    for hd in range(N_SGU_HEADS):
        cs = slice(hd * dh, (hd + 1) * dh)
        wm = jnp.where(causal, ws_ref[hd], 0.0).astype(BF16)
        bias = bs_ref[hd]
        for ch in range(tm // CHUNK):
            rs = slice(ch * CHUNK, (ch + 1) * CHUNK)
            s = _dot(wm, zb[rs, cs]) + bias
            y_s[rs, cs] = (y_s[rs, cs].astype(F32) * s).astype(BF16)
    h = x + _rms(_dot(y_s[...], wout_ref[...]), gmix_ref[...])
    o_ref[...] = _ffn_block(h, gpre_ref[...], gpost_ref[...], wgu_ref, wd_ref, a_s)


def _odd(x, g, win, lng, lnb, ws, bs, wout, gmix, gpre, gpost, wgu, wd, *, tm, seq_len):
    n, d = x.shape
    d_sgu = wout.shape[0]
    row = lambda i: (i, 0)
    const = lambda i: (0, 0)
    const3 = lambda i: (0, 0, 0)
    return pl.pallas_call(
        functools.partial(_odd_kernel, tm=tm, seq_len=seq_len), grid=(n // tm,),
        in_specs=[pl.BlockSpec((tm, d), row), pl.BlockSpec((1, d), const), pl.BlockSpec(win.shape, const),
                  pl.BlockSpec((1, d_sgu), const), pl.BlockSpec((1, d_sgu), const),
                  pl.BlockSpec(ws.shape, const3), pl.BlockSpec(bs.shape, const3),
                  pl.BlockSpec(wout.shape, const), pl.BlockSpec((1, d), const), pl.BlockSpec((1, d), const),
                  pl.BlockSpec((1, d), const), pl.BlockSpec(wgu.shape, const), pl.BlockSpec(wd.shape, const)],
        out_specs=[pl.BlockSpec((tm, d), row), pl.BlockSpec((tm, d_sgu), row)],
        out_shape=[jax.ShapeDtypeStruct((n, d), F32), jax.ShapeDtypeStruct((n, d_sgu), F32)],
        scratch_shapes=[pltpu.VMEM((tm, d_sgu), BF16), pltpu.VMEM((tm, wd.shape[0]), BF16)],
        compiler_params=_cparams(("arbitrary",)),
        name="odd_mixer_ffn",
    )(x, g, win, lng, lnb, ws, bs, wout, gmix, gpre, gpost, wgu, wd)


def _tile_rows(n, want):
    tm = min(want, n)
    while n % tm:
        tm //= 2
    return tm


def _from_feature_major(a, heads):
    b, f, t = a.shape
    if heads is None:
        return jnp.transpose(a, (0, 2, 1))
    return jnp.transpose(a.reshape(b, heads, f // heads, t), (0, 3, 1, 2))


def kernel(x_prompt, x_sample, cache_k, cache_v, cache_idx_k, state_pool, page_table, norm_mix_pre, norm_mix_post, norm_ffn_pre, norm_ffn_post, w_in_even, w_out_even, w_pool_group, pool_scale, w_in_odd, sgu_norm_g, sgu_norm_b, w_spatial, b_spatial, w_out_odd, w_ffn_gate_up, w_ffn_down):
    bp, tp, d = x_prompt.shape
    bs, ts, _ = x_sample.shape
    n_pages = page_table.shape[1]
    past_len = n_pages * PAGE_SIZE
    topk_p = min(TOPK_MAX, tp // 4)
    topk_s = min(TOPK_MAX, (past_len + ts) // 4)
    depth = norm_mix_pre.shape[0]
    assert ts == SUBLANES and tp % Q_BLOCK == 0 and d % LANES == 0

    hp = x_prompt.reshape(bp * tp, d)
    hs = x_sample.reshape(bs * ts, d)
    tm_p = _tile_rows(bp * tp, 512)
    tm_p_seq = _tile_rows(tp, 512)
    tm_s = _tile_rows(bs * ts, 256)
    row = lambda a: a.reshape(1, -1)

    tabs_p = _rope_tables(np.arange(tp))
    tabs_s = _rope_tables(np.tile(past_len + np.arange(ts), tm_s // ts))

    outs_p, outs_s, sgu_s = [], [], []
    for layer in range(depth):
        li = layer // 2
        gpre, gpost = row(norm_ffn_pre[layer]), row(norm_ffn_post[layer])
        wgu = w_ffn_gate_up[layer].astype(BF16)
        wd = w_ffn_down[layer].astype(BF16)
        gmix_pre, gmix_post = row(norm_mix_pre[layer]), row(norm_mix_post[layer])
        if layer % 2 == 0:
            wmain, wfeat, wiwt = _arrange_w_in_even(w_in_even[li])
            wpg = w_pool_group[li].astype(BF16)
            psc = row(pool_scale[li])
            wo = w_out_even[li].astype(BF16)
            (q, iq, ik2, kb, kt, vt, vtb, ikt, iwt, pool, xp_tail) = _even_in(
                hp, gmix_pre, wmain, wfeat, wiwt, tabs_p, wpg, psc, None, nb=bp, tm=tm_p_seq, past_len=0)
            attn = _dsa_prompt(q, iq, iwt, kb, vtb, ik2, nb=bp, tk=min(512, tp), topk=topk_p)
            hp = _even_out(hp, attn, pool, wo, gmix_post, gpre, gpost, wgu, wd, tm=tm_p)
            outs_p.append((_from_feature_major(kt, N_KV_HEADS), _from_feature_major(vt, N_KV_HEADS),
                           _from_feature_major(ikt, None), xp_tail[:, 1:, :]))
            prefix = jnp.pad(state_pool[li], ((0, 0), (1, 0), (0, 0)))
            (q, iq, ik2, kb, kt, vt, vtb, ikt, iwt, pool, xp) = _even_in(
                hs, gmix_pre, wmain, wfeat, wiwt, tabs_s, wpg, psc, prefix, nb=1, tm=tm_s, past_len=past_len)
            del kb, vtb
            k_new = _from_feature_major(kt.reshape(1, D_KV, bs * ts), None).reshape(bs, ts, D_KV)
            v_new = _from_feature_major(vt.reshape(1, D_KV, bs * ts), None).reshape(bs, ts, D_KV)
            ik_new = _from_feature_major(ikt, None).reshape(bs, ts, D_IDX)
            q3 = q.reshape(bs, ts, 4, 2, HEAD_DIM)
            zq = jnp.zeros_like(q3[:, :, :, 0])
            qg = jnp.concatenate([jnp.concatenate([q3[:, :, :, 0], zq], axis=-1),
                                  jnp.concatenate([zq, q3[:, :, :, 1]], axis=-1)], axis=2)
            qd = jnp.transpose(qg, (0, 2, 1, 3)).reshape(bs, N_HEADS * ts, D_KV)
            iqd = jnp.transpose(iq.reshape(bs, ts, N_IDX_HEADS, D_IDX), (0, 2, 1, 3)).reshape(bs, N_IDX_HEADS * ts, D_IDX)
            wcol = jnp.transpose(iwt[0].reshape(N_IDX_HEADS, bs, ts), (1, 0, 2)).reshape(bs, N_IDX_HEADS * ts, 1)
            ci_t = jnp.transpose(cache_idx_k[li], (0, 2, 1))
            ck_t = jnp.transpose(cache_k[li], (0, 2, 3, 1)).reshape(-1, D_KV, PAGE_SIZE)
            cv_t = jnp.transpose(cache_v[li], (0, 2, 3, 1)).reshape(-1, D_KV, PAGE_SIZE)
            attn_s = _dsa_decode(page_table, ci_t, ck_t, cv_t, iqd, wcol, ik_new, qd, k_new, v_new, topk=topk_s)
            hs = _even_out(hs, attn_s.astype(BF16), pool, wo, gmix_post, gpre, gpost, wgu, wd, tm=tm_s)
            new_pool_s = jnp.concatenate([state_pool[li][:, ts:, :], xp.reshape(bs, ts, -1)], axis=1)
            outs_s.append((k_new.reshape(bs, ts, N_KV_HEADS, HEAD_DIM), v_new.reshape(bs, ts, N_KV_HEADS, HEAD_DIM),
                           ik_new, new_pool_s))
        else:
            win = w_in_odd[li].astype(BF16)
            wout = w_out_odd[li].astype(BF16)
            lng, lnb = row(sgu_norm_g[li]), row(sgu_norm_b[li])
            ws_p = w_spatial[li][:, :CHUNK, :CHUNK]
            bs_p = b_spatial[li][:, :CHUNK, None]
            hp, _ = _odd(hp, gmix_pre, win, lng, lnb, ws_p, bs_p, wout, gmix_post, gpre, gpost, wgu, wd,
                         tm=tm_p, seq_len=CHUNK)
            reps = CHUNK // ts
            ws_s = jnp.tile(w_spatial[li][:, :ts, :ts], (1, reps, reps))
            bs_s = jnp.tile(b_spatial[li][:, :ts], (1, reps))[:, :, None]
            hs, zs = _odd(hs, gmix_pre, win, lng, lnb, ws_s, bs_s, wout, gmix_post, gpre, gpost, wgu, wd,
                          tm=_tile_rows(bs * ts, 256), seq_len=ts)
            sgu_s.append(zs.reshape(bs, ts, -1))

    return (hp.reshape(bp, tp, d), hs.reshape(bs, ts, d),
            jnp.stack([o[0] for o in outs_p]), jnp.stack([o[1] for o in outs_p]),
            jnp.stack([o[2] for o in outs_p]), jnp.stack([o[3] for o in outs_p]),
            jnp.stack([o[0] for o in outs_s]), jnp.stack([o[1] for o in outs_s]),
            jnp.stack([o[2] for o in outs_s]), jnp.stack([o[3] for o in outs_s]),
            jnp.stack(sgu_s))
```

```python
import functools

import jax
import jax.numpy as jnp
import numpy as np
from jax import lax
from jax.experimental import pallas as pl
from jax.experimental.pallas import tpu as pltpu

EPS = 1e-6
N_HEADS = 8
HEAD_DIM = 64
N_KV_HEADS = 2
KV_GROUP = N_HEADS // N_KV_HEADS
D_ATTN = N_HEADS * HEAD_DIM
D_KV = N_KV_HEADS * HEAD_DIM
N_IDX_HEADS = 8
D_IDX = 64
TOPK_MAX = 256
Q_BLOCK = 128
ROPE_THETA = 10000.0
POOL_WINDOWS = (2, 4, 8, 16)
POOL_BUF = 15
PAGE_SIZE = 128
CHUNK = 128
N_SGU_HEADS = 8
D_POOL_GROUP = 128
V_ROWS = HEAD_DIM + 16

LANES = 128
SUBLANES = 8
VMEM_LIMIT = 56 * 1024 * 1024
INT_MIN = -2 ** 31
NEG_BIG = -1e30
LOG2E = 1.4426950408889634

F32 = jnp.float32
BF16 = jnp.bfloat16
I32 = jnp.int32

_NT = (((1,), (1,)), ((), ()))


def _cparams(sem):
    return pltpu.CompilerParams(dimension_semantics=sem, vmem_limit_bytes=VMEM_LIMIT)


def _rms(x, g):
    return x * lax.rsqrt(jnp.mean(x * x, axis=-1, keepdims=True) + EPS) * g


def _dot(a, b):
    return jnp.dot(a, b, preferred_element_type=F32)


def _dot_nt(a, b):
    return lax.dot_general(a, b, _NT, preferred_element_type=F32)


def _gelu(x):
    return 0.5 * x * (1.0 + lax.erf(x * (2.0 ** -0.5)))


C_Q, C_IQ, C_IK2, C_K, C_XP, C_END = 0, 512, 1024, 1280, 1408, 1920
R_K, R_V, R_IK, R_END = 0, 128, 256, 320


def _rope_tile(t, cos, sin, first_half):
    partner = jnp.where(first_half, pltpu.roll(t, 96, 1), pltpu.roll(t, 32, 1))
    return t * cos + partner * sin


def _rope_rows(t, cos_t, sin_t):
    half = HEAD_DIM // 2
    x1, x2 = t[0:half], t[half:HEAD_DIM]
    return jnp.concatenate([x1 * cos_t - x2 * sin_t, x2 * cos_t + x1 * sin_t], axis=0)


def _even_in_kernel(*refs, tm, decode, past_len):
    if decode:
        (x_ref, g_ref, w_ref, wt_ref, wiwt_ref, cos_ref, sin_ref, cost_ref, sint_ref, wpg_ref, psc_ref, pre_ref,
         q_ref, iq_ref, ik2_ref, kb_ref, kt_ref, vt_ref, vtb_ref, ikt_ref, iwt_ref, pool_ref, xp_ref,
         ext_s) = refs
    else:
        (x_ref, g_ref, w_ref, wt_ref, wiwt_ref, cos_ref, sin_ref, cost_ref, sint_ref, wpg_ref, psc_ref,
         q_ref, iq_ref, ik2_ref, kb_ref, kt_ref, vt_ref, vtb_ref, ikt_ref, iwt_ref, pool_ref, xp_ref,
         ext_s) = refs
    j = pl.program_id(1)
    xn = _rms(x_ref[...], g_ref[...]).astype(BF16)
    cos = cos_ref[...]
    sin = sin_ref[...]
    lane = lax.broadcasted_iota(I32, (1, LANES), 1)
    first_half = (lane % HEAD_DIM) < (HEAD_DIM // 2)

    def proj_rope(c0, c1):
        t = _dot(xn, w_ref[:, c0:c1])
        return [_rope_tile(t[:, c:c + LANES], cos, sin, first_half) for c in range(0, c1 - c0, LANES)]

    for half in range(2):
        tiles = proj_rope(C_Q + 256 * half, C_Q + 256 * (half + 1))
        for c, t in enumerate(tiles):
            col = 256 * half + LANES * c
            q_ref[:, col:col + LANES] = (t * (HEAD_DIM ** -0.5 * LOG2E)).astype(BF16)
    for half in range(2):
        tiles = proj_rope(C_IQ + 256 * half, C_IQ + 256 * (half + 1))
        for c, t in enumerate(tiles):
            col = 256 * half + LANES * c
            iq_ref[:, col:col + LANES] = t.astype(BF16)
    tiles = proj_rope(C_IK2, C_K)
    ik2_ref[:, 0:LANES] = tiles[0].astype(BF16)
    ik2_ref[:, LANES:2 * LANES] = tiles[1].astype(BF16)
    kb_ref[...] = proj_rope(C_K, C_XP)[0].astype(BF16)

    cos_t = cost_ref[...]
    sin_t = sint_ref[...]
    ft = _dot_nt(wt_ref[...], xn)
    kt = jnp.concatenate([_rope_rows(ft[R_K + h * HEAD_DIM:R_K + (h + 1) * HEAD_DIM], cos_t, sin_t)
                          for h in range(N_KV_HEADS)], axis=0)
    vt = ft[R_V:R_IK]
    kt_ref[0] = kt
    vt_ref[0] = vt
    pad = jnp.concatenate([jnp.ones((1, tm), F32), jnp.zeros((V_ROWS - HEAD_DIM - 1, tm), F32)], axis=0)
    for h in range(N_KV_HEADS):
        vtb_ref[0, h * V_ROWS:h * V_ROWS + HEAD_DIM, :] = vt[h * HEAD_DIM:(h + 1) * HEAD_DIM].astype(BF16)
        vtb_ref[0, h * V_ROWS + HEAD_DIM:(h + 1) * V_ROWS, :] = pad.astype(BF16)
    ikt_ref[0] = _rope_rows(ft[R_IK:R_END], cos_t, sin_t)
    iwt_ref[0] = _dot_nt(wiwt_ref[...], xn) * (N_IDX_HEADS ** -0.5) * (D_IDX ** -0.5)

    xp = _dot(xn, w_ref[:, C_XP:C_END])
    row = lax.broadcasted_iota(I32, (tm, 1), 0)
    if decode:
        ns = tm // SUBLANES
        ext_s[:, 0:16, :] = pre_ref[...]
        ext_s[:, 16:24, :] = xp.reshape(ns, SUBLANES, 4 * D_POOL_GROUP)
        pos = past_len + (row % SUBLANES)
    else:
        @pl.when(j == 0)
        def _():
            ext_s[0:16, :] = jnp.zeros((16, 4 * D_POOL_GROUP), F32)
        ext_s[16:16 + tm, :] = xp
        pos = j * tm + row
    for g, w in enumerate(POOL_WINDOWS):
        cs = slice(g * D_POOL_GROUP, (g + 1) * D_POOL_GROUP)
        tok = xp[:, cs]
        acc = tok
        for i in range(1, w):
            if decode:
                acc = acc + ext_s[:, 16 - i:24 - i, cs].reshape(tm, D_POOL_GROUP)
            else:
                acc = acc + ext_s[16 - i:16 - i + tm, cs]
        cnt = jnp.minimum(w, pos + 1).astype(F32)
        d = acc / cnt - tok
        y = _dot(d.astype(BF16), wpg_ref[g]) * psc_ref[:, cs]
        pool_ref[:, cs] = y.astype(BF16)
    if decode:
        xp_ref[...] = xp
    else:
        ext_s[0:16, :] = xp[tm - 16:tm, :]
        xp_ref[0] = xp[tm - 16:tm, :]


def _rope_tables(pos):
    half = HEAD_DIM // 2
    inv = ROPE_THETA ** (-np.arange(half, dtype=np.float64) / half)
    ang = pos.astype(np.float64)[:, None] * inv[None, :]
    cos32, sin32 = np.cos(ang).astype(np.float32), np.sin(ang).astype(np.float32)
    cos = np.tile(cos32, (1, LANES // half))
    sin = np.tile(np.concatenate([-sin32, sin32], axis=1), (1, LANES // HEAD_DIM))
    return tuple(jnp.asarray(a) for a in (cos, sin, np.ascontiguousarray(cos32.T), np.ascontiguousarray(sin32.T)))


def _arrange_w_in_even(w):
    q = w[:, 0:512].reshape(-1, N_HEADS, HEAD_DIM)
    q = jnp.stack([q[:, 0:4], q[:, 4:8]], axis=2).reshape(-1, 512)
    k = w[:, 512:640]
    v = w[:, 640:768]
    iq = w[:, 768:1280]
    ik = w[:, 1280:1344]
    iw = w[:, 1344:1352]
    xp = w[:, 1352:1864]
    z = jnp.zeros_like(ik)
    main = jnp.concatenate([q, iq, ik, z, z, ik, k, xp], axis=1).astype(BF16)
    feat = jnp.concatenate([k, v, ik], axis=1).T.astype(BF16)
    return main, feat, iw.T.astype(BF16)


def _even_in(x, g, wmain, wfeat, wiwt, tabs, wpg, psc, prefix, *, nb, tm, past_len):
    n, d = x.shape
    t = n // nb
    nt = t // tm
    decode = prefix is not None
    cos, sin, cos_t, sin_t = tabs
    row = lambda b, j: (b * nt + j, 0)
    tab_row = (lambda b, j: (0, 0)) if decode else (lambda b, j: (j, 0))
    tab_col = (lambda b, j: (0, 0)) if decode else (lambda b, j: (0, j))
    const = lambda b, j: (0, 0)
    feat = lambda b, j: (b, 0, j)
    in_specs = [
        pl.BlockSpec((tm, d), row),
        pl.BlockSpec((1, d), const),
        pl.BlockSpec(wmain.shape, const),
        pl.BlockSpec(wfeat.shape, const),
        pl.BlockSpec(wiwt.shape, const),
        pl.BlockSpec((tm, LANES), tab_row),
        pl.BlockSpec((tm, LANES), tab_row),
        pl.BlockSpec((HEAD_DIM // 2, tm), tab_col),
        pl.BlockSpec((HEAD_DIM // 2, tm), tab_col),
        pl.BlockSpec(wpg.shape, lambda b, j: (0, 0, 0)),
        pl.BlockSpec((1, 512), const),
    ]
    args = [x, g, wmain, wfeat, wiwt, cos, sin, cos_t, sin_t, wpg, psc]
    if decode:
        ns = tm // SUBLANES
        in_specs.append(pl.BlockSpec((ns, 16, 512), lambda b, j: (b * nt + j, 0, 0)))
        args.append(prefix)
        xp_shape = jax.ShapeDtypeStruct((n, 512), F32)
        xp_spec = pl.BlockSpec((tm, 512), row)
        scratch = [pltpu.VMEM((ns, 24, 512), F32)]
    else:
        xp_shape = jax.ShapeDtypeStruct((nb, 16, 512), F32)
        xp_spec = pl.BlockSpec((1, 16, 512), lambda b, j: (b, 0, 0))
        scratch = [pltpu.VMEM((tm + 16, 512), F32)]
    out_shape = [
        jax.ShapeDtypeStruct((n, 512), BF16),
        jax.ShapeDtypeStruct((n, 512), BF16),
        jax.ShapeDtypeStruct((n, 256), BF16),
        jax.ShapeDtypeStruct((n, 128), BF16),
        jax.ShapeDtypeStruct((nb, 128, t), F32),
        jax.ShapeDtypeStruct((nb, 128, t), F32),
        jax.ShapeDtypeStruct((nb, N_KV_HEADS * V_ROWS, t), BF16),
        jax.ShapeDtypeStruct((nb, D_IDX, t), F32),
        jax.ShapeDtypeStruct((nb, 8, t), F32),
        jax.ShapeDtypeStruct((n, 512), BF16),
        xp_shape,
    ]
    out_specs = [
        pl.BlockSpec((tm, 512), row),
        pl.BlockSpec((tm, 512), row),
        pl.BlockSpec((tm, 256), row),
        pl.BlockSpec((tm, 128), row),
        pl.BlockSpec((1, 128, tm), feat),
        pl.BlockSpec((1, 128, tm), feat),
        pl.BlockSpec((1, N_KV_HEADS * V_ROWS, tm), feat),
        pl.BlockSpec((1, D_IDX, tm), feat),
        pl.BlockSpec((1, 8, tm), feat),
        pl.BlockSpec((tm, 512), row),
        xp_spec,
    ]
    return pl.pallas_call(
        functools.partial(_even_in_kernel, tm=tm, decode=decode, past_len=past_len),
        grid=(nb, nt), in_specs=in_specs, out_specs=out_specs, out_shape=out_shape,
        scratch_shapes=scratch, compiler_params=_cparams(("arbitrary", "arbitrary")),
        name="even_in_decode" if decode else "even_in_prompt",
    )(*args)


F32_MIN_NORMAL = 2.0 ** -126
F32_LOWEST = -3.4028234663852886e38
LOWEST_FINITE_CODE = -2139095040


def _score_for_search(s, valid):
    return jnp.where(valid, jnp.where(jnp.abs(s) < F32_MIN_NORMAL, 0.0, s), -jnp.inf)


def _code_to_float(code):
    b = code ^ ((code >> 31) & 0x7FFFFFFF)
    f = lax.bitcast_convert_type(b, F32)
    tiny = ((b >> 23) & 0xFF) == 0
    return jnp.where(tiny, jnp.where(b > 0, F32_MIN_NORMAL, 0.0), f)


FRACTION_BITS = 24


def _topk_search(count, shape, n_index_bits, topk):
    def bit_step(i, carry):
        code, c_ge = carry
        cand = code + jnp.left_shift(jnp.int32(1), 31 - i)
        cf = _code_to_float(cand)
        c = count(lambda sv, idx: sv >= cf)
        ok = c >= topk
        return jnp.where(ok, cand, code), jnp.where(ok, c, c_ge)

    code, c_ge = lax.fori_loop(0, 32, bit_step, (jnp.full(shape, INT_MIN, I32), jnp.zeros(shape, I32)))
    short = code < LOWEST_FINITE_CODE
    thr = jnp.where(short, -jnp.inf, _code_to_float(code))
    nxt = jnp.where(short, F32_LOWEST, _code_to_float(code + 1))
    c_above = count(lambda sv, idx: sv >= nxt)
    need = topk - c_above
    ambiguous = jnp.logical_and(c_ge - c_above > need, jnp.logical_not(short))
    width = nxt - thr
    inv_width = jnp.where(jnp.logical_and(width >= F32_MIN_NORMAL, width < jnp.inf), 1.0 / width, 0.0)

    def frac(sv):
        return (sv - thr) * inv_width

    def in_band(sv):
        return jnp.logical_and(sv >= thr, sv < nxt)

    def split_band():
        c_pos = count(lambda sv, idx: jnp.logical_and(in_band(sv), frac(sv) > 0.0))

        def bisect_fraction():
            def frac_step(i, m):
                cand = m + jnp.left_shift(jnp.int32(1), FRACTION_BITS - 1 - i)
                cf = cand.astype(F32) * (2.0 ** -FRACTION_BITS)
                c = count(lambda sv, idx: jnp.logical_and(in_band(sv), frac(sv) >= cf))
                return jnp.where(c >= need, cand, m)
            m = lax.fori_loop(0, FRACTION_BITS, frac_step, jnp.zeros(shape, I32))
            r = m.astype(F32) * (2.0 ** -FRACTION_BITS)
            return r, count(lambda sv, idx: jnp.logical_and(in_band(sv), frac(sv) > r))

        any_frac = jnp.max(jnp.where(ambiguous, c_pos, 0)) > 0
        rstar, c_gtr = lax.cond(any_frac, bisect_fraction, lambda: (jnp.zeros(shape, F32), c_pos))
        need_eq = need - c_gtr

        def idx_step(i, lo):
            cand = lo + jnp.left_shift(jnp.int32(1), n_index_bits - 1 - i)
            c = count(lambda sv, idx: jnp.logical_and(jnp.logical_and(in_band(sv), frac(sv) == rstar), idx < cand))
            return jnp.where(c < need_eq, cand, lo)
        cut = lax.fori_loop(0, n_index_bits, idx_step, jnp.zeros(shape, I32))
        return jnp.where(ambiguous, rstar, -1.0), jnp.where(ambiguous, cut, 2 ** 30)

    any_amb = jnp.max(ambiguous.astype(I32)) > 0
    rstar, cut = lax.cond(any_amb, split_band,
                          lambda: (jnp.full(shape, -1.0, F32), jnp.full(shape, 2 ** 30, I32)))

    def select(sv, idx):
        r = frac(sv)
        taken = jnp.logical_or(r > rstar, jnp.logical_and(r == rstar, idx <= cut))
        return jnp.logical_or(sv >= nxt, jnp.logical_and(in_band(sv), taken))

    return select, jnp.maximum(thr, F32_LOWEST), any_amb


def _index_bits(n):
    return max(1, int(np.ceil(np.log2(n))))


def _fold_rows(x, op):
    return op(x.reshape(x.shape[0] // SUBLANES, SUBLANES, x.shape[1]), axis=0)


def _dsa_prompt_kernel(q_ref, iq_ref, iwt_ref, kb_ref, vt_ref, ik2_ref, o_ref,
                       keys_s, qm_s, iqs_s, acc_s, *, tk, topk):
    i = pl.program_id(1)
    n_tiles = (i * Q_BLOCK + Q_BLOCK + tk - 1) // tk
    lane = lax.broadcasted_iota(I32, (1, LANES), 1)
    qpos = i * Q_BLOCK + lane
    qpos_l = qpos
    lo_half = lane < HEAD_DIM
    nq = KV_GROUP * Q_BLOCK

    for jj in range(KV_GROUP):
        qt = q_ref[:, jj * LANES:(jj + 1) * LANES]
        zero = jnp.zeros_like(qt)
        qm_s[0, jj * Q_BLOCK:(jj + 1) * Q_BLOCK, :] = jnp.where(lo_half, qt, zero)
        qm_s[1, jj * Q_BLOCK:(jj + 1) * Q_BLOCK, :] = jnp.where(lo_half, zero, qt)
        iqs_s[jj * Q_BLOCK:(jj + 1) * Q_BLOCK, :] = iq_ref[:, jj * LANES:(jj + 1) * LANES]
    w = iwt_ref[0]

    def score_tile(t, carry):
        r0 = pl.multiple_of(t * tk, tk)
        ik2 = ik2_ref[pl.ds(r0, tk), :]
        iqs = iqs_s[...]
        rel_e = _dot_nt(ik2[:, 0:LANES], iqs)
        rel_o = _dot_nt(ik2[:, LANES:2 * LANES], iqs)
        s = jnp.zeros((tk, LANES), F32)
        for jj in range(4):
            cs = slice(jj * LANES, (jj + 1) * LANES)
            s = s + w[2 * jj:2 * jj + 1, :] * jnp.maximum(rel_e[:, cs], 0.0)
            s = s + w[2 * jj + 1:2 * jj + 2, :] * jnp.maximum(rel_o[:, cs], 0.0)
        kpos = r0 + lax.broadcasted_iota(I32, (tk, 1), 0)
        keys_s[pl.ds(r0, tk), :] = _score_for_search(s, kpos <= qpos_l)
        return carry

    lax.fori_loop(0, n_tiles, score_tile, 0)

    def count(pred):
        def body(t, acc):
            r0 = pl.multiple_of(t * tk, tk)
            sv = keys_s[pl.ds(r0, tk), :]
            idx = r0 + lax.broadcasted_iota(I32, (tk, 1), 0)
            return acc + _fold_rows(pred(sv, idx).astype(I32), jnp.sum)
        acc = lax.fori_loop(0, n_tiles, body, jnp.zeros((SUBLANES, LANES), I32))
        return acc.sum(axis=0, keepdims=True)

    select, thr_ge, any_cut = _topk_search(count, (1, LANES), _index_bits(keys_s.shape[0]), topk)

    def attend(with_cuts):
        def pv_tile(t, ms):
            r0 = pl.multiple_of(t * tk, tk)
            sv = keys_s[pl.ds(r0, tk), :]
            if with_cuts:
                sel = select(sv, r0 + lax.broadcasted_iota(I32, (tk, 1), 0))
            else:
                sel = sv >= thr_ge
            sel = jnp.concatenate([sel] * KV_GROUP, axis=1)
            kt = kb_ref[pl.ds(r0, tk), :]
            sts = [jnp.where(sel, _dot_nt(kt, qm_s[g]), NEG_BIG) for g in range(N_KV_HEADS)]
            ms_new = []
            for g, st in enumerate(sts):
                m_new = jnp.maximum(ms[g], jnp.max(_fold_rows(st, jnp.max), axis=0, keepdims=True))
                alpha = jnp.exp2(ms[g] - m_new)
                p = jnp.exp2(st - m_new)
                ms_new.append(m_new)
                vt = vt_ref[0, g * V_ROWS:(g + 1) * V_ROWS, pl.ds(r0, tk)]
                acc_s[g] = acc_s[g] * alpha + _dot(vt, p.astype(BF16))
            return tuple(ms_new)

        lax.fori_loop(0, n_tiles, pv_tile, tuple(jnp.full((1, nq), NEG_BIG, F32) for _ in range(N_KV_HEADS)))

    acc_s[...] = jnp.zeros(acc_s.shape, F32)
    lax.cond(any_cut, lambda: attend(True), lambda: attend(False))
    heads = []
    for g in range(N_KV_HEADS):
        og = acc_s[g, 0:HEAD_DIM, :] / acc_s[g, HEAD_DIM:HEAD_DIM + 1, :]
        heads += [og[:, jj * Q_BLOCK:(jj + 1) * Q_BLOCK] for jj in range(KV_GROUP)]
    o_ref[...] = jnp.concatenate(heads, axis=0).T.astype(o_ref.dtype)


def _store_heads(o_ref, heads, lo_half):
    for jj in range(N_HEADS // 2):
        a, b = heads[2 * jj], heads[2 * jj + 1]
        if 2 * jj < KV_GROUP:
            tile = jnp.where(lo_half, a, pltpu.roll(b, HEAD_DIM, 1))
        else:
            tile = jnp.where(lo_half, pltpu.roll(a, HEAD_DIM, 1), b)
        o_ref[:, jj * LANES:(jj + 1) * LANES] = tile.astype(o_ref.dtype)


def _dsa_prompt(q, iq, iwt, kb, vtb, ik2, *, nb, tk, topk):
    n = q.shape[0]
    t = n // nb
    nq = t // Q_BLOCK
    assert t % tk == 0
    blk = lambda b, i: (b * nq + i, 0)
    seq = lambda b, i: (b, 0)
    feat = lambda b, i: (b, 0, 0)
    return pl.pallas_call(
        functools.partial(_dsa_prompt_kernel, tk=tk, topk=topk),
        grid=(nb, nq),
        in_specs=[
            pl.BlockSpec((Q_BLOCK, 512), blk),
            pl.BlockSpec((Q_BLOCK, 512), blk),
            pl.BlockSpec((1, 8, Q_BLOCK), lambda b, i: (b, 0, i)),
            pl.BlockSpec((t, 128), seq),
            pl.BlockSpec((1, N_KV_HEADS * V_ROWS, t), feat),
            pl.BlockSpec((t, 256), seq),
        ],
        out_specs=pl.BlockSpec((Q_BLOCK, 512), blk),
        out_shape=jax.ShapeDtypeStruct((n, 512), BF16),
        scratch_shapes=[
            pltpu.VMEM((t, LANES), F32),
            pltpu.VMEM((N_KV_HEADS, KV_GROUP * Q_BLOCK, LANES), BF16),
            pltpu.VMEM((4 * Q_BLOCK, LANES), BF16),
            pltpu.VMEM((N_KV_HEADS, V_ROWS, KV_GROUP * Q_BLOCK), F32),
        ],
        compiler_params=_cparams(("arbitrary", "arbitrary")),
        name="dsa_prompt",
    )(q, iq, iwt, kb, vtb, ik2)


SEQ_GROUP = Q_BLOCK // SUBLANES


def _pad_rows(a):
    return jnp.concatenate([a, jnp.zeros((LANES - a.shape[0], a.shape[1]), a.dtype)], axis=0).astype(BF16)


def _decode_select_kernel(*refs, n_pages, topk, nnew):
    ip = refs[1:1 + n_pages]
    iqd_ref, wcol_ref, inew_ref, bias_ref, keys_s = refs[1 + n_pages:]
    t = pl.program_id(1)
    n_keys = keys_s.shape[1]
    past = n_pages * PAGE_SIZE
    n_tiles = n_keys // LANES
    lane = lax.broadcasted_iota(I32, (1, LANES), 1)
    qrow = lax.broadcasted_iota(I32, (nnew, 1), 0)
    new_valid = jnp.logical_and(lane <= qrow, lane < nnew)

    def head_sum(x):
        return x.reshape(N_IDX_HEADS, nnew, x.shape[1]).sum(axis=0)

    r0 = pl.multiple_of(t * nnew, nnew)
    iqd = iqd_ref[0]
    wcol = wcol_ref[0]
    for p in range(n_pages):
        rel = _dot(iqd, ip[p][0].astype(BF16))
        s = head_sum(wcol * jnp.maximum(rel, 0.0))
        keys_s[pl.ds(r0, nnew), p * LANES:(p + 1) * LANES] = _score_for_search(s, True)
    rel = _dot_nt(iqd, _pad_rows(inew_ref[0]))
    s = head_sum(wcol * jnp.maximum(rel, 0.0))
    keys_s[pl.ds(r0, nnew), past:past + LANES] = _score_for_search(s, new_valid)

    @pl.when(t == SEQ_GROUP - 1)
    def _search():
        def count(pred):
            acc = jnp.zeros((Q_BLOCK, LANES), I32)
            for c in range(n_tiles):
                idx = c * LANES + lane
                acc = acc + pred(keys_s[:, c * LANES:(c + 1) * LANES], idx).astype(I32)
            return acc.sum(axis=1, keepdims=True)
        select, _, _ = _topk_search(count, (Q_BLOCK, 1), _index_bits(n_keys), topk)
        for c in range(n_tiles):
            sel = select(keys_s[:, c * LANES:(c + 1) * LANES], c * LANES + lane)
            bias_ref[:, c * LANES:(c + 1) * LANES] = jnp.where(sel, 0.0, NEG_BIG)


def _decode_attend_kernel(*refs, n_pages, nnew):
    kp = refs[1:1 + n_pages]
    vp = refs[1 + n_pages:1 + 2 * n_pages]
    qd_ref, knew_ref, vnew_ref, bias_ref, o_ref = refs[1 + 2 * n_pages:]
    past = n_pages * PAGE_SIZE
    lane = lax.broadcasted_iota(I32, (1, LANES), 1)
    bias = jnp.concatenate([bias_ref[...]] * N_HEADS, axis=0)
    qd = qd_ref[0]
    st = jnp.concatenate([_dot(qd, kp[p][0].astype(BF16)) for p in range(n_pages)]
                         + [_dot_nt(qd, _pad_rows(knew_ref[0]))], axis=1) + bias
    m = jnp.max(st, axis=1, keepdims=True)
    pr = jnp.exp2(st - m)
    l = jnp.sum(pr, axis=1, keepdims=True)
    pb = pr.astype(BF16)
    o = _dot(pb[:, past:past + LANES], _pad_rows(vnew_ref[0]))
    for p in range(n_pages):
        o = o + _dot_nt(pb[:, p * LANES:(p + 1) * LANES], vp[p][0].astype(BF16))
    o = o / l
    _store_heads(o_ref, [o[h * nnew:(h + 1) * nnew] for h in range(N_HEADS)], lane < HEAD_DIM)


def _dsa_decode(page_table, ci_t, ck_t, cv_t, iqd, wcol, inew, qd, knew, vnew, *, topk):
    ns, n_pages = page_table.shape
    nnew = knew.shape[1]
    assert ns % SEQ_GROUP == 0 and nnew == SUBLANES
    n_keys = (n_pages + 1) * PAGE_SIZE

    sel_page = lambda p: pl.BlockSpec((1, D_IDX, PAGE_SIZE),
                                      lambda g, t, pt, p=p: (pt[g * SEQ_GROUP + t, p], 0, 0))
    sel3 = lambda g, t, pt: (g * SEQ_GROUP + t, 0, 0)
    bias = pl.pallas_call(
        functools.partial(_decode_select_kernel, n_pages=n_pages, topk=topk, nnew=nnew),
        grid_spec=pltpu.PrefetchScalarGridSpec(
            num_scalar_prefetch=1, grid=(ns // SEQ_GROUP, SEQ_GROUP),
            in_specs=([sel_page(p) for p in range(n_pages)]
                      + [pl.BlockSpec((1, N_IDX_HEADS * nnew, D_IDX), sel3),
                         pl.BlockSpec((1, N_IDX_HEADS * nnew, 1), sel3),
                         pl.BlockSpec((1, nnew, D_IDX), sel3)]),
            out_specs=pl.BlockSpec((Q_BLOCK, n_keys), lambda g, t, pt: (g, 0)),
            scratch_shapes=[pltpu.VMEM((Q_BLOCK, n_keys), F32)]),
        out_shape=jax.ShapeDtypeStruct((ns * nnew, n_keys), F32),
        compiler_params=_cparams(("arbitrary", "arbitrary")),
        name="dsa_decode_select",
    )(page_table, *([ci_t] * n_pages), iqd, wcol, inew)

    att_page = lambda p: pl.BlockSpec((1, D_KV, PAGE_SIZE), lambda s, pt, p=p: (pt[s, p], 0, 0))
    att3 = lambda s, pt: (s, 0, 0)
    return pl.pallas_call(
        functools.partial(_decode_attend_kernel, n_pages=n_pages, nnew=nnew),
        grid_spec=pltpu.PrefetchScalarGridSpec(
            num_scalar_prefetch=1, grid=(ns,),
            in_specs=([att_page(p) for p in range(n_pages)] + [att_page(p) for p in range(n_pages)]
                      + [pl.BlockSpec((1, N_HEADS * nnew, D_KV), att3),
                         pl.BlockSpec((1, nnew, D_KV), att3),
                         pl.BlockSpec((1, nnew, D_KV), att3),
                         pl.BlockSpec((nnew, n_keys), lambda s, pt: (s, 0))]),
            out_specs=pl.BlockSpec((nnew, D_ATTN), lambda s, pt: (s, 0))),
        out_shape=jax.ShapeDtypeStruct((ns * nnew, D_ATTN), F32),
        compiler_params=_cparams(("arbitrary",)),
        name="dsa_decode_attend",
    )(page_table, *([ck_t] * n_pages), *([cv_t] * n_pages), qd, knew, vnew, bias)


FF_CHUNK = 256


def _ffn_block(h, gpre, gpost, wgu_ref, wd_ref, a_s):
    d_ff = wd_ref.shape[0]
    hn = _rms(h, gpre).astype(BF16)
    for c in range(0, d_ff, FF_CHUNK):
        gate = _dot(hn, wgu_ref[:, c:c + FF_CHUNK])
        up = _dot(hn, wgu_ref[:, d_ff + c:d_ff + c + FF_CHUNK])
        a_s[:, c:c + FF_CHUNK] = (gate * jax.nn.sigmoid(gate) * up).astype(BF16)
    return h + _rms(_dot(a_s[...], wd_ref[...]), gpost)


def _even_out_kernel(x_ref, attn_ref, pool_ref, wo_ref, gmix_ref, gpre_ref, gpost_ref, wgu_ref, wd_ref,
                     o_ref, a_s):
    mix = _dot(attn_ref[...], wo_ref[0:D_ATTN, :]) + _dot(pool_ref[...], wo_ref[D_ATTN:, :])
    h = x_ref[...] + _rms(mix, gmix_ref[...])
    o_ref[...] = _ffn_block(h, gpre_ref[...], gpost_ref[...], wgu_ref, wd_ref, a_s)


def _even_out(x, attn, pool, wo, gmix, gpre, gpost, wgu, wd, *, tm):
    n, d = x.shape
    row = lambda i: (i, 0)
    const = lambda i: (0, 0)
    return pl.pallas_call(
        _even_out_kernel, grid=(n // tm,),
        in_specs=[pl.BlockSpec((tm, d), row), pl.BlockSpec((tm, 512), row), pl.BlockSpec((tm, 512), row),
                  pl.BlockSpec(wo.shape, const), pl.BlockSpec((1, d), const), pl.BlockSpec((1, d), const),
                  pl.BlockSpec((1, d), const), pl.BlockSpec(wgu.shape, const), pl.BlockSpec(wd.shape, const)],
        out_specs=pl.BlockSpec((tm, d), row),
        out_shape=jax.ShapeDtypeStruct((n, d), F32),
        scratch_shapes=[pltpu.VMEM((tm, wd.shape[0]), BF16)],
        compiler_params=_cparams(("arbitrary",)),
        name="even_out_ffn",
    )(x, attn, pool, wo, gmix, gpre, gpost, wgu, wd)


def _odd_kernel(x_ref, g_ref, win_ref, lng_ref, lnb_ref, ws_ref, bs_ref, wout_ref, gmix_ref,
                gpre_ref, gpost_ref, wgu_ref, wd_ref, o_ref, z_ref, y_s, a_s, *, tm, seq_len):
    d_sgu = wout_ref.shape[0]
    dh = d_sgu // N_SGU_HEADS
    x = x_ref[...]
    xn = _rms(x, g_ref[...]).astype(BF16)
    r = lax.broadcasted_iota(I32, (CHUNK, CHUNK), 0)
    c = lax.broadcasted_iota(I32, (CHUNK, CHUNK), 1)
    causal = jnp.logical_and(r // seq_len == c // seq_len, c <= r)
    for hd in range(N_SGU_HEADS):
        cs = slice(hd * dh, (hd + 1) * dh)
        u = _gelu(_dot(xn, win_ref[:, cs]))
        y_s[:, cs] = u.astype(BF16)
    vs = []
    for hd in range(N_SGU_HEADS):
        cs = slice(d_sgu + hd * dh, d_sgu + (hd + 1) * dh)
        vs.append(_gelu(_dot(xn, win_ref[:, cs])))
    v = jnp.concatenate(vs, axis=1)
    mu = jnp.mean(v, axis=-1, keepdims=True)
    vc = v - mu
    z = vc * lax.rsqrt(jnp.mean(vc * vc, axis=-1, keepdims=True) + EPS) * lng_ref[...] + lnb_ref[...]
    z_ref[...] = z
    zb = z.astype(BF16)
    for hd in range(N_SGU_HEADS):
        cs = slice(hd * dh, (hd + 1) * dh)
        wm = jnp.where(causal, ws_ref[hd], 0.0).astype(BF16)
        bias = bs_ref[hd]
        for ch in range(tm // CHUNK):
            rs = slice(ch * CHUNK, (ch + 1) * CHUNK)
            s = _dot(wm, zb[rs, cs]) + bias
            y_s[rs, cs] = (y_s[rs, cs].astype(F32) * s).astype(BF16)
    h = x + _rms(_dot(y_s[...], wout_ref[...]), gmix_ref[...])
    o_ref[...] = _ffn_block(h, gpre_ref[...], gpost_ref[...], wgu_ref, wd_ref, a_s)


def _odd(x, g, win, lng, lnb, ws, bs, wout, gmix, gpre, gpost, wgu, wd, *, tm, seq_len):
    n, d = x.shape
    d_sgu = wout.shape[0]
    row = lambda i: (i, 0)
    const = lambda i: (0, 0)
    const3 = lambda i: (0, 0, 0)
    return pl.pallas_call(
        functools.partial(_odd_kernel, tm=tm, seq_len=seq_len), grid=(n // tm,),
        in_specs=[pl.BlockSpec((tm, d), row), pl.BlockSpec((1, d), const), pl.BlockSpec(win.shape, const),
                  pl.BlockSpec((1, d_sgu), const), pl.BlockSpec((1, d_sgu), const),
                  pl.BlockSpec(ws.shape, const3), pl.BlockSpec(bs.shape, const3),
                  pl.BlockSpec(wout.shape, const), pl.BlockSpec((1, d), const), pl.BlockSpec((1, d), const),
                  pl.BlockSpec((1, d), const), pl.BlockSpec(wgu.shape, const), pl.BlockSpec(wd.shape, const)],
        out_specs=[pl.BlockSpec((tm, d), row), pl.BlockSpec((tm, d_sgu), row)],
        out_shape=[jax.ShapeDtypeStruct((n, d), F32), jax.ShapeDtypeStruct((n, d_sgu), F32)],
        scratch_shapes=[pltpu.VMEM((tm, d_sgu), BF16), pltpu.VMEM((tm, wd.shape[0]), BF16)],
        compiler_params=_cparams(("arbitrary",)),
        name="odd_mixer_ffn",
    )(x, g, win, lng, lnb, ws, bs, wout, gmix, gpre, gpost, wgu, wd)


def _tile_rows(n, want):
    tm = min(want, n)
    while n % tm:
        tm //= 2
    return tm


def _from_feature_major(a, heads):
    b, f, t = a.shape
    if heads is None:
        return jnp.transpose(a, (0, 2, 1))
    return jnp.transpose(a.reshape(b, heads, f // heads, t), (0, 3, 1, 2))


def kernel(x_prompt, x_sample, cache_k, cache_v, cache_idx_k, state_pool, page_table, norm_mix_pre, norm_mix_post, norm_ffn_pre, norm_ffn_post, w_in_even, w_out_even, w_pool_group, pool_scale, w_in_odd, sgu_norm_g, sgu_norm_b, w_spatial, b_spatial, w_out_odd, w_ffn_gate_up, w_ffn_down):
    bp, tp, d = x_prompt.shape
    bs, ts, _ = x_sample.shape
    n_pages = page_table.shape[1]
    past_len = n_pages * PAGE_SIZE
    topk_p = min(TOPK_MAX, tp // 4)
    topk_s = min(TOPK_MAX, (past_len + ts) // 4)
    depth = norm_mix_pre.shape[0]
    assert ts == SUBLANES and tp % Q_BLOCK == 0 and d % LANES == 0

    hp = x_prompt.reshape(bp * tp, d)
    hs = x_sample.reshape(bs * ts, d)
    tm_p = _tile_rows(bp * tp, 512)
    tm_p_seq = _tile_rows(tp, 512)
    tm_s = _tile_rows(bs * ts, 256)
    row = lambda a: a.reshape(1, -1)

    tabs_p = _rope_tables(np.arange(tp))
    tabs_s = _rope_tables(np.tile(past_len + np.arange(ts), tm_s // ts))

    outs_p, outs_s, sgu_s = [], [], []
    for layer in range(depth):
        li = layer // 2
        gpre, gpost = row(norm_ffn_pre[layer]), row(norm_ffn_post[layer])
        wgu = w_ffn_gate_up[layer].astype(BF16)
        wd = w_ffn_down[layer].astype(BF16)
        gmix_pre, gmix_post = row(norm_mix_pre[layer]), row(norm_mix_post[layer])
        if layer % 2 == 0:
            wmain, wfeat, wiwt = _arrange_w_in_even(w_in_even[li])
            wpg = w_pool_group[li].astype(BF16)
            psc = row(pool_scale[li])
            wo = w_out_even[li].astype(BF16)
            (q, iq, ik2, kb, kt, vt, vtb, ikt, iwt, pool, xp_tail) = _even_in(
                hp, gmix_pre, wmain, wfeat, wiwt, tabs_p, wpg, psc, None, nb=bp, tm=tm_p_seq, past_len=0)
            attn = _dsa_prompt(q, iq, iwt, kb, vtb, ik2, nb=bp, tk=min(512, tp), topk=topk_p)
            hp = _even_out(hp, attn, pool, wo, gmix_post, gpre, gpost, wgu, wd, tm=tm_p)
            outs_p.append((_from_feature_major(kt, N_KV_HEADS), _from_feature_major(vt, N_KV_HEADS),
                           _from_feature_major(ikt, None), xp_tail[:, 1:, :]))
            prefix = jnp.pad(state_pool[li], ((0, 0), (1, 0), (0, 0)))
            (q, iq, ik2, kb, kt, vt, vtb, ikt, iwt, pool, xp) = _even_in(
                hs, gmix_pre, wmain, wfeat, wiwt, tabs_s, wpg, psc, prefix, nb=1, tm=tm_s, past_len=past_len)
            del kb, vtb
            k_new = _from_feature_major(kt.reshape(1, D_KV, bs * ts), None).reshape(bs, ts, D_KV)
            v_new = _from_feature_major(vt.reshape(1, D_KV, bs * ts), None).reshape(bs, ts, D_KV)
            ik_new = _from_feature_major(ikt, None).reshape(bs, ts, D_IDX)
            q3 = q.reshape(bs, ts, 4, 2, HEAD_DIM)
            zq = jnp.zeros_like(q3[:, :, :, 0])
            qg = jnp.concatenate([jnp.concatenate([q3[:, :, :, 0], zq], axis=-1),
                                  jnp.concatenate([zq, q3[:, :, :, 1]], axis=-1)], axis=2)
            qd = jnp.transpose(qg, (0, 2, 1, 3)).reshape(bs, N_HEADS * ts, D_KV)
            iqd = jnp.transpose(iq.reshape(bs, ts, N_IDX_HEADS, D_IDX), (0, 2, 1, 3)).reshape(bs, N_IDX_HEADS * ts, D_IDX)
            wcol = jnp.transpose(iwt[0].reshape(N_IDX_HEADS, bs, ts), (1, 0, 2)).reshape(bs, N_IDX_HEADS * ts, 1)
            ci_t = jnp.transpose(cache_idx_k[li], (0, 2, 1))
            ck_t = jnp.transpose(cache_k[li], (0, 2, 3, 1)).reshape(-1, D_KV, PAGE_SIZE)
            cv_t = jnp.transpose(cache_v[li], (0, 2, 3, 1)).reshape(-1, D_KV, PAGE_SIZE)
            attn_s = _dsa_decode(page_table, ci_t, ck_t, cv_t, iqd, wcol, ik_new, qd, k_new, v_new, topk=topk_s)
            hs = _even_out(hs, attn_s.astype(BF16), pool, wo, gmix_post, gpre, gpost, wgu, wd, tm=tm_s)
            new_pool_s = jnp.concatenate([state_pool[li][:, ts:, :], xp.reshape(bs, ts, -1)], axis=1)
            outs_s.append((k_new.reshape(bs, ts, N_KV_HEADS, HEAD_DIM), v_new.reshape(bs, ts, N_KV_HEADS, HEAD_DIM),
                           ik_new, new_pool_s))
        else:
            win = w_in_odd[li].astype(BF16)
            wout = w_out_odd[li].astype(BF16)
            lng, lnb = row(sgu_norm_g[li]), row(sgu_norm_b[li])
            ws_p = w_spatial[li][:, :CHUNK, :CHUNK]
            bs_p = b_spatial[li][:, :CHUNK, None]
            hp, _ = _odd(hp, gmix_pre, win, lng, lnb, ws_p, bs_p, wout, gmix_post, gpre, gpost, wgu, wd,
                         tm=tm_p, seq_len=CHUNK)
            reps = CHUNK // ts
            ws_s = jnp.tile(w_spatial[li][:, :ts, :ts], (1, reps, reps))
            bs_s = jnp.tile(b_spatial[li][:, :ts], (1, reps))[:, :, None]
            hs, zs = _odd(hs, gmix_pre, win, lng, lnb, ws_s, bs_s, wout, gmix_post, gpre, gpost, wgu, wd,
                          tm=_tile_rows(bs * ts, 256), seq_len=ts)
            sgu_s.append(zs.reshape(bs, ts, -1))

    return (hp.reshape(bp, tp, d), hs.reshape(bs, ts, d),
            jnp.stack([o[0] for o in outs_p]), jnp.stack([o[1] for o in outs_p]),
            jnp.stack([o[2] for o in outs_p]), jnp.stack([o[3] for o in outs_p]),
            jnp.stack([o[0] for o in outs_s]), jnp.stack([o[1] for o in outs_s]),
            jnp.stack([o[2] for o in outs_s]), jnp.stack([o[3] for o in outs_s]),
            jnp.stack(sgu_s))
```

```python
import functools

import jax
import jax.numpy as jnp
import numpy as np
from jax import lax
from jax.experimental import pallas as pl
from jax.experimental.pallas import tpu as pltpu

EPS = 1e-6
N_HEADS = 8
HEAD_DIM = 64
N_KV_HEADS = 2
KV_GROUP = N_HEADS // N_KV_HEADS
D_ATTN = N_HEADS * HEAD_DIM
D_KV = N_KV_HEADS * HEAD_DIM
N_IDX_HEADS = 8
D_IDX = 64
TOPK_MAX = 256
Q_BLOCK = 128
ROPE_THETA = 10000.0
POOL_WINDOWS = (2, 4, 8, 16)
POOL_BUF = 15
PAGE_SIZE = 128
CHUNK = 128
N_SGU_HEADS = 8
D_POOL_GROUP = 128
V_ROWS = HEAD_DIM + 16

LANES = 128
SUBLANES = 8
VMEM_LIMIT = 56 * 1024 * 1024
INT_MIN = -2 ** 31
NEG_BIG = -1e30
LOG2E = 1.4426950408889634

F32 = jnp.float32
BF16 = jnp.bfloat16
I32 = jnp.int32

_NT = (((1,), (1,)), ((), ()))


def _cparams(sem):
    return pltpu.CompilerParams(dimension_semantics=sem, vmem_limit_bytes=VMEM_LIMIT)


def _rms(x, g):
    return x * lax.rsqrt(jnp.mean(x * x, axis=-1, keepdims=True) + EPS) * g


def _dot(a, b):
    return jnp.dot(a, b, preferred_element_type=F32)


def _dot_nt(a, b):
    return lax.dot_general(a, b, _NT, preferred_element_type=F32)


def _gelu(x):
    return 0.5 * x * (1.0 + lax.erf(x * (2.0 ** -0.5)))


C_Q, C_IQ, C_IK2, C_K, C_XP, C_END = 0, 512, 1024, 1280, 1408, 1920
R_K, R_V, R_IK, R_END = 0, 128, 256, 320


def _rope_tile(t, cos, sin, first_half):
    partner = jnp.where(first_half, pltpu.roll(t, 96, 1), pltpu.roll(t, 32, 1))
    return t * cos + partner * sin


def _rope_rows(t, cos_t, sin_t):
    half = HEAD_DIM // 2
    x1, x2 = t[0:half], t[half:HEAD_DIM]
    return jnp.concatenate([x1 * cos_t - x2 * sin_t, x2 * cos_t + x1 * sin_t], axis=0)


def _even_in_kernel(*refs, tm, decode, past_len):
    if decode:
        (x_ref, g_ref, w_ref, wt_ref, wiwt_ref, cos_ref, sin_ref, cost_ref, sint_ref, wpg_ref, psc_ref, pre_ref,
         q_ref, iq_ref, ik2_ref, kb_ref, kt_ref, vt_ref, vtb_ref, ikt_ref, iwt_ref, pool_ref, xp_ref,
         ext_s) = refs
    else:
        (x_ref, g_ref, w_ref, wt_ref, wiwt_ref, cos_ref, sin_ref, cost_ref, sint_ref, wpg_ref, psc_ref,
         q_ref, iq_ref, ik2_ref, kb_ref, kt_ref, vt_ref, vtb_ref, ikt_ref, iwt_ref, pool_ref, xp_ref,
         ext_s) = refs
    j = pl.program_id(1)
    xn = _rms(x_ref[...], g_ref[...]).astype(BF16)
    cos = cos_ref[...]
    sin = sin_ref[...]
    lane = lax.broadcasted_iota(I32, (1, LANES), 1)
    first_half = (lane % HEAD_DIM) < (HEAD_DIM // 2)

    def proj_rope(c0, c1):
        t = _dot(xn, w_ref[:, c0:c1])
        return [_rope_tile(t[:, c:c + LANES], cos, sin, first_half) for c in range(0, c1 - c0, LANES)]

    for half in range(2):
        tiles = proj_rope(C_Q + 256 * half, C_Q + 256 * (half + 1))
        for c, t in enumerate(tiles):
            col = 256 * half + LANES * c
            q_ref[:, col:col + LANES] = (t * (HEAD_DIM ** -0.5 * LOG2E)).astype(BF16)
    for half in range(2):
        tiles = proj_rope(C_IQ + 256 * half, C_IQ + 256 * (half + 1))
        for c, t in enumerate(tiles):
            col = 256 * half + LANES * c
            iq_ref[:, col:col + LANES] = t.astype(BF16)
    tiles = proj_rope(C_IK2, C_K)
    ik2_ref[:, 0:LANES] = tiles[0].astype(BF16)
    ik2_ref[:, LANES:2 * LANES] = tiles[1].astype(BF16)
    kb_ref[...] = proj_rope(C_K, C_XP)[0].astype(BF16)

    cos_t = cost_ref[...]
    sin_t = sint_ref[...]
    ft = _dot_nt(wt_ref[...], xn)
    kt = jnp.concatenate([_rope_rows(ft[R_K + h * HEAD_DIM:R_K + (h + 1) * HEAD_DIM], cos_t, sin_t)
                          for h in range(N_KV_HEADS)], axis=0)
    vt = ft[R_V:R_IK]
    kt_ref[0] = kt
    vt_ref[0] = vt
    pad = jnp.concatenate([jnp.ones((1, tm), F32), jnp.zeros((V_ROWS - HEAD_DIM - 1, tm), F32)], axis=0)
    for h in range(N_KV_HEADS):
        vtb_ref[0, h * V_ROWS:h * V_ROWS + HEAD_DIM, :] = vt[h * HEAD_DIM:(h + 1) * HEAD_DIM].astype(BF16)
        vtb_ref[0, h * V_ROWS + HEAD_DIM:(h + 1) * V_ROWS, :] = pad.astype(BF16)
    ikt_ref[0] = _rope_rows(ft[R_IK:R_END], cos_t, sin_t)
    iwt_ref[0] = _dot_nt(wiwt_ref[...], xn) * (N_IDX_HEADS ** -0.5) * (D_IDX ** -0.5)

    xp = _dot(xn, w_ref[:, C_XP:C_END])
    row = lax.broadcasted_iota(I32, (tm, 1), 0)
    if decode:
        ns = tm // SUBLANES
        ext_s[:, 0:16, :] = pre_ref[...]
        ext_s[:, 16:24, :] = xp.reshape(ns, SUBLANES, 4 * D_POOL_GROUP)
        pos = past_len + (row % SUBLANES)
    else:
        @pl.when(j == 0)
        def _():
            ext_s[0:16, :] = jnp.zeros((16, 4 * D_POOL_GROUP), F32)
        ext_s[16:16 + tm, :] = xp
        pos = j * tm + row
    for g, w in enumerate(POOL_WINDOWS):
        cs = slice(g * D_POOL_GROUP, (g + 1) * D_POOL_GROUP)
        tok = xp[:, cs]
        acc = tok
        for i in range(1, w):
            if decode:
                acc = acc + ext_s[:, 16 - i:24 - i, cs].reshape(tm, D_POOL_GROUP)
            else:
                acc = acc + ext_s[16 - i:16 - i + tm, cs]
        cnt = jnp.minimum(w, pos + 1).astype(F32)
        d = acc / cnt - tok
        y = _dot(d.astype(BF16), wpg_ref[g]) * psc_ref[:, cs]
        pool_ref[:, cs] = y.astype(BF16)
    if decode:
        xp_ref[...] = xp
    else:
        ext_s[0:16, :] = xp[tm - 16:tm, :]
        xp_ref[0] = xp[tm - 16:tm, :]


def _rope_tables(pos):
    half = HEAD_DIM // 2
    inv = ROPE_THETA ** (-np.arange(half, dtype=np.float64) / half)
    ang = pos.astype(np.float64)[:, None] * inv[None, :]
    cos32, sin32 = np.cos(ang).astype(np.float32), np.sin(ang).astype(np.float32)
    cos = np.tile(cos32, (1, LANES // half))
    sin = np.tile(np.concatenate([-sin32, sin32], axis=1), (1, LANES // HEAD_DIM))
    return tuple(jnp.asarray(a) for a in (cos, sin, np.ascontiguousarray(cos32.T), np.ascontiguousarray(sin32.T)))


def _arrange_w_in_even(w):
    q = w[:, 0:512].reshape(-1, N_HEADS, HEAD_DIM)
    q = jnp.stack([q[:, 0:4], q[:, 4:8]], axis=2).reshape(-1, 512)
    k = w[:, 512:640]
    v = w[:, 640:768]
    iq = w[:, 768:1280]
    ik = w[:, 1280:1344]
    iw = w[:, 1344:1352]
    xp = w[:, 1352:1864]
    z = jnp.zeros_like(ik)
    main = jnp.concatenate([q, iq, ik, z, z, ik, k, xp], axis=1).astype(BF16)
    feat = jnp.concatenate([k, v, ik], axis=1).T.astype(BF16)
    return main, feat, iw.T.astype(BF16)


def _even_in(x, g, wmain, wfeat, wiwt, tabs, wpg, psc, prefix, *, nb, tm, past_len):
    n, d = x.shape
    t = n // nb
    nt = t // tm
    decode = prefix is not None
    cos, sin, cos_t, sin_t = tabs
    row = lambda b, j: (b * nt + j, 0)
    tab_row = (lambda b, j: (0, 0)) if decode else (lambda b, j: (j, 0))
    tab_col = (lambda b, j: (0, 0)) if decode else (lambda b, j: (0, j))
    const = lambda b, j: (0, 0)
    feat = lambda b, j: (b, 0, j)
    in_specs = [
        pl.BlockSpec((tm, d), row),
        pl.BlockSpec((1, d), const),
        pl.BlockSpec(wmain.shape, const),
        pl.BlockSpec(wfeat.shape, const),
        pl.BlockSpec(wiwt.shape, const),
        pl.BlockSpec((tm, LANES), tab_row),
        pl.BlockSpec((tm, LANES), tab_row),
        pl.BlockSpec((HEAD_DIM // 2, tm), tab_col),
        pl.BlockSpec((HEAD_DIM // 2, tm), tab_col),
        pl.BlockSpec(wpg.shape, lambda b, j: (0, 0, 0)),
        pl.BlockSpec((1, 512), const),
    ]
    args = [x, g, wmain, wfeat, wiwt, cos, sin, cos_t, sin_t, wpg, psc]
    if decode:
        ns = tm // SUBLANES
        in_specs.append(pl.BlockSpec((ns, 16, 512), lambda b, j: (b * nt + j, 0, 0)))
        args.append(prefix)
        xp_shape = jax.ShapeDtypeStruct((n, 512), F32)
        xp_spec = pl.BlockSpec((tm, 512), row)
        scratch = [pltpu.VMEM((ns, 24, 512), F32)]
    else:
        xp_shape = jax.ShapeDtypeStruct((nb, 16, 512), F32)
        xp_spec = pl.BlockSpec((1, 16, 512), lambda b, j: (b, 0, 0))
        scratch = [pltpu.VMEM((tm + 16, 512), F32)]
    out_shape = [
        jax.ShapeDtypeStruct((n, 512), BF16),
        jax.ShapeDtypeStruct((n, 512), BF16),
        jax.ShapeDtypeStruct((n, 256), BF16),
        jax.ShapeDtypeStruct((n, 128), BF16),
        jax.ShapeDtypeStruct((nb, 128, t), F32),
        jax.ShapeDtypeStruct((nb, 128, t), F32),
        jax.ShapeDtypeStruct((nb, N_KV_HEADS * V_ROWS, t), BF16),
        jax.ShapeDtypeStruct((nb, D_IDX, t), F32),
        jax.ShapeDtypeStruct((nb, 8, t), F32),
        jax.ShapeDtypeStruct((n, 512), BF16),
        xp_shape,
    ]
    out_specs = [
        pl.BlockSpec((tm, 512), row),
        pl.BlockSpec((tm, 512), row),
        pl.BlockSpec((tm, 256), row),
        pl.BlockSpec((tm, 128), row),
        pl.BlockSpec((1, 128, tm), feat),
        pl.BlockSpec((1, 128, tm), feat),
        pl.BlockSpec((1, N_KV_HEADS * V_ROWS, tm), feat),
        pl.BlockSpec((1, D_IDX, tm), feat),
        pl.BlockSpec((1, 8, tm), feat),
        pl.BlockSpec((tm, 512), row),
        xp_spec,
    ]
    return pl.pallas_call(
        functools.partial(_even_in_kernel, tm=tm, decode=decode, past_len=past_len),
        grid=(nb, nt), in_specs=in_specs, out_specs=out_specs, out_shape=out_shape,
        scratch_shapes=scratch, compiler_params=_cparams(("arbitrary", "arbitrary")),
        name="even_in_decode" if decode else "even_in_prompt",
    )(*args)


F32_MIN_NORMAL = 2.0 ** -126
F32_LOWEST = -3.4028234663852886e38
LOWEST_FINITE_CODE = -2139095040


def _score_for_search(s, valid):
    return jnp.where(valid, jnp.where(jnp.abs(s) < F32_MIN_NORMAL, 0.0, s), -jnp.inf)


def _code_to_float(code):
    b = code ^ ((code >> 31) & 0x7FFFFFFF)
    f = lax.bitcast_convert_type(b, F32)
    tiny = ((b >> 23) & 0xFF) == 0
    return jnp.where(tiny, jnp.where(b > 0, F32_MIN_NORMAL, 0.0), f)


FRACTION_BITS = 24


def _topk_search(count, shape, n_index_bits, topk):
    def bit_step(i, carry):
        code, c_ge = carry
        cand = code + jnp.left_shift(jnp.int32(1), 31 - i)
        cf = _code_to_float(cand)
        c = count(lambda sv, idx: sv >= cf)
        ok = c >= topk
        return jnp.where(ok, cand, code), jnp.where(ok, c, c_ge)

    code, c_ge = lax.fori_loop(0, 32, bit_step, (jnp.full(shape, INT_MIN, I32), jnp.zeros(shape, I32)))
    short = code < LOWEST_FINITE_CODE
    thr = jnp.where(short, -jnp.inf, _code_to_float(code))
    nxt = jnp.where(short, F32_LOWEST, _code_to_float(code + 1))
    c_above = count(lambda sv, idx: sv >= nxt)
    need = topk - c_above
    ambiguous = jnp.logical_and(c_ge - c_above > need, jnp.logical_not(short))
    width = nxt - thr
    inv_width = jnp.where(jnp.logical_and(width >= F32_MIN_NORMAL, width < jnp.inf), 1.0 / width, 0.0)

    def frac(sv):
        return (sv - thr) * inv_width

    def in_band(sv):
        return jnp.logical_and(sv >= thr, sv < nxt)

    def split_band():
        c_pos = count(lambda sv, idx: jnp.logical_and(in_band(sv), frac(sv) > 0.0))

        def bisect_fraction():
            def frac_step(i, m):
                cand = m + jnp.left_shift(jnp.int32(1), FRACTION_BITS - 1 - i)
                cf = cand.astype(F32) * (2.0 ** -FRACTION_BITS)
                c = count(lambda sv, idx: jnp.logical_and(in_band(sv), frac(sv) >= cf))
                return jnp.where(c >= need, cand, m)
            m = lax.fori_loop(0, FRACTION_BITS, frac_step, jnp.zeros(shape, I32))
            r = m.astype(F32) * (2.0 ** -FRACTION_BITS)
            return r, count(lambda sv, idx: jnp.logical_and(in_band(sv), frac(sv) > r))

        any_frac = jnp.max(jnp.where(ambiguous, c_pos, 0)) > 0
        rstar, c_gtr = lax.cond(any_frac, bisect_fraction, lambda: (jnp.zeros(shape, F32), c_pos))
        need_eq = need - c_gtr

        def index_cut(tied):
            def idx_step(i, lo):
                cand = lo + jnp.left_shift(jnp.int32(1), n_index_bits - 1 - i)
                c = count(lambda sv, idx: jnp.logical_and(tied(sv), idx < cand))
                return jnp.where(c < need_eq, cand, lo)
            return lax.fori_loop(0, n_index_bits, idx_step, jnp.zeros(shape, I32))

        cut = lax.cond(any_frac,
                       lambda: index_cut(lambda sv: jnp.logical_and(in_band(sv), frac(sv) == rstar)),
                       lambda: index_cut(lambda sv: sv == thr))
        return jnp.where(ambiguous, rstar, -1.0), jnp.where(ambiguous, cut, take_all), any_frac

    take_all = jnp.where(short, -1, 2 ** 30).astype(I32)
    any_amb = jnp.max(ambiguous.astype(I32)) > 0
    rstar, cut, any_frac = lax.cond(any_amb, split_band,
                                    lambda: (jnp.full(shape, -1.0, F32), take_all, jnp.zeros((), jnp.bool_)))

    def select(sv, idx):
        r = frac(sv)
        taken = jnp.logical_or(r > rstar, jnp.logical_and(r == rstar, idx <= cut))
        return jnp.logical_or(sv >= nxt, jnp.logical_and(in_band(sv), taken))

    def select_no_fraction(sv, idx):
        return jnp.logical_or(sv > thr, jnp.logical_and(sv == thr, idx <= cut))

    return select, select_no_fraction, jnp.maximum(thr, F32_LOWEST), any_amb, any_frac


def _index_bits(n):
    return max(1, int(np.ceil(np.log2(n))))


def _fold_rows(x, op):
    return op(x.reshape(x.shape[0] // SUBLANES, SUBLANES, x.shape[1]), axis=0)


def _dsa_prompt_kernel(q_ref, iq_ref, iwt_ref, kb_ref, vt_ref, ik2_ref, o_ref,
                       keys_s, qm_s, iqs_s, acc_s, *, tk, topk):
    i = pl.program_id(1)
    n_tiles = (i * Q_BLOCK + Q_BLOCK + tk - 1) // tk
    lane = lax.broadcasted_iota(I32, (1, LANES), 1)
    qpos = i * Q_BLOCK + lane
    qpos_l = qpos
    lo_half = lane < HEAD_DIM
    nq = KV_GROUP * Q_BLOCK

    for jj in range(KV_GROUP):
        qt = q_ref[:, jj * LANES:(jj + 1) * LANES]
        zero = jnp.zeros_like(qt)
        qm_s[0, jj * Q_BLOCK:(jj + 1) * Q_BLOCK, :] = jnp.where(lo_half, qt, zero)
        qm_s[1, jj * Q_BLOCK:(jj + 1) * Q_BLOCK, :] = jnp.where(lo_half, zero, qt)
        iqs_s[jj * Q_BLOCK:(jj + 1) * Q_BLOCK, :] = iq_ref[:, jj * LANES:(jj + 1) * LANES]
    w = iwt_ref[0]

    def score_tile(t, carry):
        r0 = pl.multiple_of(t * tk, tk)
        ik2 = ik2_ref[pl.ds(r0, tk), :]
        iqs = iqs_s[...]
        rel_e = _dot_nt(ik2[:, 0:LANES], iqs)
        rel_o = _dot_nt(ik2[:, LANES:2 * LANES], iqs)
        s = jnp.zeros((tk, LANES), F32)
        for jj in range(4):
            cs = slice(jj * LANES, (jj + 1) * LANES)
            s = s + w[2 * jj:2 * jj + 1, :] * jnp.maximum(rel_e[:, cs], 0.0)
            s = s + w[2 * jj + 1:2 * jj + 2, :] * jnp.maximum(rel_o[:, cs], 0.0)
        kpos = r0 + lax.broadcasted_iota(I32, (tk, 1), 0)
        keys_s[pl.ds(r0, tk), :] = _score_for_search(s, kpos <= qpos_l)
        return carry

    lax.fori_loop(0, n_tiles, score_tile, 0)

    def count(pred):
        def body(t, acc):
            r0 = pl.multiple_of(t * tk, tk)
            sv = keys_s[pl.ds(r0, tk), :]
            idx = r0 + lax.broadcasted_iota(I32, (tk, 1), 0)
            return acc + _fold_rows(pred(sv, idx).astype(I32), jnp.sum)
        acc = lax.fori_loop(0, n_tiles, body, jnp.zeros((SUBLANES, LANES), I32))
        return acc.sum(axis=0, keepdims=True)

    select, select_no_fraction, thr_ge, any_cut, any_frac = _topk_search(
        count, (1, LANES), _index_bits(keys_s.shape[0]), topk)

    def attend(selector):
        def pv_tile(t, ms):
            r0 = pl.multiple_of(t * tk, tk)
            sv = keys_s[pl.ds(r0, tk), :]
            if selector is None:
                sel = sv >= thr_ge
            else:
                sel = selector(sv, r0 + lax.broadcasted_iota(I32, (tk, 1), 0))
            sel = jnp.concatenate([sel] * KV_GROUP, axis=1)
            kt = kb_ref[pl.ds(r0, tk), :]
            sts = [jnp.where(sel, _dot_nt(kt, qm_s[g]), NEG_BIG) for g in range(N_KV_HEADS)]
            ms_new = []
            for g, st in enumerate(sts):
                m_new = jnp.maximum(ms[g], jnp.max(_fold_rows(st, jnp.max), axis=0, keepdims=True))
                alpha = jnp.exp2(ms[g] - m_new)
                p = jnp.exp2(st - m_new)
                ms_new.append(m_new)
                vt = vt_ref[0, g * V_ROWS:(g + 1) * V_ROWS, pl.ds(r0, tk)]
                acc_s[g] = acc_s[g] * alpha + _dot(vt, p.astype(BF16))
            return tuple(ms_new)

        lax.fori_loop(0, n_tiles, pv_tile, tuple(jnp.full((1, nq), NEG_BIG, F32) for _ in range(N_KV_HEADS)))

    acc_s[...] = jnp.zeros(acc_s.shape, F32)
    lax.cond(any_cut,
             lambda: lax.cond(any_frac, lambda: attend(select), lambda: attend(select_no_fraction)),
             lambda: attend(None))
    heads = []
    for g in range(N_KV_HEADS):
        og = acc_s[g, 0:HEAD_DIM, :] / acc_s[g, HEAD_DIM:HEAD_DIM + 1, :]
        heads += [og[:, jj * Q_BLOCK:(jj + 1) * Q_BLOCK] for jj in range(KV_GROUP)]
    o_ref[...] = jnp.concatenate(heads, axis=0).T.astype(o_ref.dtype)


def _store_heads(o_ref, heads, lo_half):
    for jj in range(N_HEADS // 2):
        a, b = heads[2 * jj], heads[2 * jj + 1]
        if 2 * jj < KV_GROUP:
            tile = jnp.where(lo_half, a, pltpu.roll(b, HEAD_DIM, 1))
        else:
            tile = jnp.where(lo_half, pltpu.roll(a, HEAD_DIM, 1), b)
        o_ref[:, jj * LANES:(jj + 1) * LANES] = tile.astype(o_ref.dtype)


def _dsa_prompt(q, iq, iwt, kb, vtb, ik2, *, nb, tk, topk):
    n = q.shape[0]
    t = n // nb
    nq = t // Q_BLOCK
    assert t % tk == 0
    blk = lambda b, i: (b * nq + i, 0)
    seq = lambda b, i: (b, 0)
    feat = lambda b, i: (b, 0, 0)
    return pl.pallas_call(
        functools.partial(_dsa_prompt_kernel, tk=tk, topk=topk),
        grid=(nb, nq),
        in_specs=[
            pl.BlockSpec((Q_BLOCK, 512), blk),
            pl.BlockSpec((Q_BLOCK, 512), blk),
            pl.BlockSpec((1, 8, Q_BLOCK), lambda b, i: (b, 0, i)),
            pl.BlockSpec((t, 128), seq),
            pl.BlockSpec((1, N_KV_HEADS * V_ROWS, t), feat),
            pl.BlockSpec((t, 256), seq),
        ],
        out_specs=pl.BlockSpec((Q_BLOCK, 512), blk),
        out_shape=jax.ShapeDtypeStruct((n, 512), BF16),
        scratch_shapes=[
            pltpu.VMEM((t, LANES), F32),
            pltpu.VMEM((N_KV_HEADS, KV_GROUP * Q_BLOCK, LANES), BF16),
            pltpu.VMEM((4 * Q_BLOCK, LANES), BF16),
            pltpu.VMEM((N_KV_HEADS, V_ROWS, KV_GROUP * Q_BLOCK), F32),
        ],
        compiler_params=_cparams(("arbitrary", "arbitrary")),
        name="dsa_prompt",
    )(q, iq, iwt, kb, vtb, ik2)


SEQ_GROUP = Q_BLOCK // SUBLANES


def _pad_rows(a):
    return jnp.concatenate([a, jnp.zeros((LANES - a.shape[0], a.shape[1]), a.dtype)], axis=0).astype(BF16)


def _decode_select_kernel(*refs, n_pages, topk, nnew):
    ip = refs[1:1 + n_pages]
    iqd_ref, wcol_ref, inew_ref, bias_ref, keys_s = refs[1 + n_pages:]
    t = pl.program_id(1)
    n_keys = keys_s.shape[1]
    past = n_pages * PAGE_SIZE
    n_tiles = n_keys // LANES
    lane = lax.broadcasted_iota(I32, (1, LANES), 1)
    qrow = lax.broadcasted_iota(I32, (nnew, 1), 0)
    new_valid = jnp.logical_and(lane <= qrow, lane < nnew)

    def head_sum(x):
        return x.reshape(N_IDX_HEADS, nnew, x.shape[1]).sum(axis=0)

    r0 = pl.multiple_of(t * nnew, nnew)
    iqd = iqd_ref[0]
    wcol = wcol_ref[0]
    for p in range(n_pages):
        rel = _dot(iqd, ip[p][0].astype(BF16))
        s = head_sum(wcol * jnp.maximum(rel, 0.0))
        keys_s[pl.ds(r0, nnew), p * LANES:(p + 1) * LANES] = _score_for_search(s, True)
    rel = _dot_nt(iqd, _pad_rows(inew_ref[0]))
    s = head_sum(wcol * jnp.maximum(rel, 0.0))
    keys_s[pl.ds(r0, nnew), past:past + LANES] = _score_for_search(s, new_valid)

    @pl.when(t == SEQ_GROUP - 1)
    def _search():
        def count(pred):
            acc = jnp.zeros((Q_BLOCK, LANES), I32)
            for c in range(n_tiles):
                idx = c * LANES + lane
                acc = acc + pred(keys_s[:, c * LANES:(c + 1) * LANES], idx).astype(I32)
            return acc.sum(axis=1, keepdims=True)
        select, _, _, _, _ = _topk_search(count, (Q_BLOCK, 1), _index_bits(n_keys), topk)
        for c in range(n_tiles):
            sel = select(keys_s[:, c * LANES:(c + 1) * LANES], c * LANES + lane)
            bias_ref[:, c * LANES:(c + 1) * LANES] = jnp.where(sel, 0.0, NEG_BIG)


def _decode_attend_kernel(*refs, n_pages, nnew):
    kp = refs[1:1 + n_pages]
    vp = refs[1 + n_pages:1 + 2 * n_pages]
    qd_ref, knew_ref, vnew_ref, bias_ref, o_ref = refs[1 + 2 * n_pages:]
    past = n_pages * PAGE_SIZE
    lane = lax.broadcasted_iota(I32, (1, LANES), 1)
    bias = jnp.concatenate([bias_ref[...]] * N_HEADS, axis=0)
    qd = qd_ref[0]
    st = jnp.concatenate([_dot(qd, kp[p][0].astype(BF16)) for p in range(n_pages)]
                         + [_dot_nt(qd, _pad_rows(knew_ref[0]))], axis=1) + bias
    m = jnp.max(st, axis=1, keepdims=True)
    pr = jnp.exp2(st - m)
    l = jnp.sum(pr, axis=1, keepdims=True)
    pb = pr.astype(BF16)
    o = _dot(pb[:, past:past + LANES], _pad_rows(vnew_ref[0]))
    for p in range(n_pages):
        o = o + _dot_nt(pb[:, p * LANES:(p + 1) * LANES], vp[p][0].astype(BF16))
    o = o / l
    _store_heads(o_ref, [o[h * nnew:(h + 1) * nnew] for h in range(N_HEADS)], lane < HEAD_DIM)


def _dsa_decode(page_table, ci_t, ck_t, cv_t, iqd, wcol, inew, qd, knew, vnew, *, topk):
    ns, n_pages = page_table.shape
    nnew = knew.shape[1]
    assert ns % SEQ_GROUP == 0 and nnew == SUBLANES
    n_keys = (n_pages + 1) * PAGE_SIZE

    sel_page = lambda p: pl.BlockSpec((1, D_IDX, PAGE_SIZE),
                                      lambda g, t, pt, p=p: (pt[g * SEQ_GROUP + t, p], 0, 0))
    sel3 = lambda g, t, pt: (g * SEQ_GROUP + t, 0, 0)
    bias = pl.pallas_call(
        functools.partial(_decode_select_kernel, n_pages=n_pages, topk=topk, nnew=nnew),
        grid_spec=pltpu.PrefetchScalarGridSpec(
            num_scalar_prefetch=1, grid=(ns // SEQ_GROUP, SEQ_GROUP),
            in_specs=([sel_page(p) for p in range(n_pages)]
                      + [pl.BlockSpec((1, N_IDX_HEADS * nnew, D_IDX), sel3),
                         pl.BlockSpec((1, N_IDX_HEADS * nnew, 1), sel3),
                         pl.BlockSpec((1, nnew, D_IDX), sel3)]),
            out_specs=pl.BlockSpec((Q_BLOCK, n_keys), lambda g, t, pt: (g, 0)),
            scratch_shapes=[pltpu.VMEM((Q_BLOCK, n_keys), F32)]),
        out_shape=jax.ShapeDtypeStruct((ns * nnew, n_keys), F32),
        compiler_params=_cparams(("arbitrary", "arbitrary")),
        name="dsa_decode_select",
    )(page_table, *([ci_t] * n_pages), iqd, wcol, inew)

    att_page = lambda p: pl.BlockSpec((1, D_KV, PAGE_SIZE), lambda s, pt, p=p: (pt[s, p], 0, 0))
    att3 = lambda s, pt: (s, 0, 0)
    return pl.pallas_call(
        functools.partial(_decode_attend_kernel, n_pages=n_pages, nnew=nnew),
        grid_spec=pltpu.PrefetchScalarGridSpec(
            num_scalar_prefetch=1, grid=(ns,),
            in_specs=([att_page(p) for p in range(n_pages)] + [att_page(p) for p in range(n_pages)]
                      + [pl.BlockSpec((1, N_HEADS * nnew, D_KV), att3),
                         pl.BlockSpec((1, nnew, D_KV), att3),
                         pl.BlockSpec((1, nnew, D_KV), att3),
                         pl.BlockSpec((nnew, n_keys), lambda s, pt: (s, 0))]),
            out_specs=pl.BlockSpec((nnew, D_ATTN), lambda s, pt: (s, 0))),
        out_shape=jax.ShapeDtypeStruct((ns * nnew, D_ATTN), F32),
        compiler_params=_cparams(("arbitrary",)),
        name="dsa_decode_attend",
    )(page_table, *([ck_t] * n_pages), *([cv_t] * n_pages), qd, knew, vnew, bias)


FF_CHUNK = 256


def _ffn_block(h, gpre, gpost, wgu_ref, wd_ref, a_s):
    d_ff = wd_ref.shape[0]
    hn = _rms(h, gpre).astype(BF16)
    for c in range(0, d_ff, FF_CHUNK):
        gate = _dot(hn, wgu_ref[:, c:c + FF_CHUNK])
        up = _dot(hn, wgu_ref[:, d_ff + c:d_ff + c + FF_CHUNK])
        a_s[:, c:c + FF_CHUNK] = (gate * jax.nn.sigmoid(gate) * up).astype(BF16)
    return h + _rms(_dot(a_s[...], wd_ref[...]), gpost)


def _even_out_kernel(x_ref, attn_ref, pool_ref, wo_ref, gmix_ref, gpre_ref, gpost_ref, wgu_ref, wd_ref,
                     o_ref, a_s):
    mix = _dot(attn_ref[...], wo_ref[0:D_ATTN, :]) + _dot(pool_ref[...], wo_ref[D_ATTN:, :])
    h = x_ref[...] + _rms(mix, gmix_ref[...])
    o_ref[...] = _ffn_block(h, gpre_ref[...], gpost_ref[...], wgu_ref, wd_ref, a_s)


def _even_out(x, attn, pool, wo, gmix, gpre, gpost, wgu, wd, *, tm):
    n, d = x.shape
    row = lambda i: (i, 0)
    const = lambda i: (0, 0)
    return pl.pallas_call(
        _even_out_kernel, grid=(n // tm,),
        in_specs=[pl.BlockSpec((tm, d), row), pl.BlockSpec((tm, 512), row), pl.BlockSpec((tm, 512), row),
                  pl.BlockSpec(wo.shape, const), pl.BlockSpec((1, d), const), pl.BlockSpec((1, d), const),
                  pl.BlockSpec((1, d), const), pl.BlockSpec(wgu.shape, const), pl.BlockSpec(wd.shape, const)],
        out_specs=pl.BlockSpec((tm, d), row),
        out_shape=jax.ShapeDtypeStruct((n, d), F32),
        scratch_shapes=[pltpu.VMEM((tm, wd.shape[0]), BF16)],
        compiler_params=_cparams(("arbitrary",)),
        name="even_out_ffn",
    )(x, attn, pool, wo, gmix, gpre, gpost, wgu, wd)


def _odd_kernel(x_ref, g_ref, win_ref, lng_ref, lnb_ref, ws_ref, bs_ref, wout_ref, gmix_ref,
                gpre_ref, gpost_ref, wgu_ref, wd_ref, o_ref, z_ref, y_s, a_s, *, tm, seq_len):
    d_sgu = wout_ref.shape[0]
    dh = d_sgu // N_SGU_HEADS
    x = x_ref[...]
    xn = _rms(x, g_ref[...]).astype(BF16)
    r = lax.broadcasted_iota(I32, (CHUNK, CHUNK), 0)
    c = lax.broadcasted_iota(I32, (CHUNK, CHUNK), 1)
    causal = jnp.logical_and(r // seq_len == c // seq_len, c <= r)
    for hd in range(N_SGU_HEADS):
        cs = slice(hd * dh, (hd + 1) * dh)
        u = _gelu(_dot(xn, win_ref[:, cs]))
        y_s[:, cs] = u.astype(BF16)
    vs = []
    for hd in range(N_SGU_HEADS):
        cs = slice(d_sgu + hd * dh, d_sgu + (hd + 1) * dh)
        vs.append(_gelu(_dot(xn, win_ref[:, cs])))
    v = jnp.concatenate(vs, axis=1)
    mu = jnp.mean(v, axis=-1, keepdims=True)
    vc = v - mu
    z = vc * lax.rsqrt(jnp.mean(vc * vc, axis=-1, keepdims=True) + EPS) * lng_ref[...] + lnb_ref[...]
    z_ref[...] = z
    zb = z.astype(BF16)
    for hd in range(N_SGU_HEADS):
        cs = slice(hd * dh, (hd + 1) * dh)
        wm = jnp.where(causal, ws_ref[hd], 0.0).astype(BF16)
        bias = bs_ref[hd]
        for ch in range(tm // CHUNK):
            rs = slice(ch * CHUNK, (ch + 1) * CHUNK)
            s = _dot(wm, zb[rs, cs]) + bias
            y_s[rs, cs] = (y_s[rs, cs].astype(F32) * s).astype(BF16)
    h = x + _rms(_dot(y_s[...], wout_ref[...]), gmix_ref[...])
    o_ref[...] = _ffn_block(h, gpre_ref[...], gpost_ref[...], wgu_ref, wd_ref, a_s)


def _odd(x, g, win, lng, lnb, ws, bs, wout, gmix, gpre, gpost, wgu, wd, *, tm, seq_len):
    n, d = x.shape
    d_sgu = wout.shape[0]
    row = lambda i: (i, 0)
    const = lambda i: (0, 0)
    const3 = lambda i: (0, 0, 0)
    return pl.pallas_call(
        functools.partial(_odd_kernel, tm=tm, seq_len=seq_len), grid=(n // tm,),
        in_specs=[pl.BlockSpec((tm, d), row), pl.BlockSpec((1, d), const), pl.BlockSpec(win.shape, const),
                  pl.BlockSpec((1, d_sgu), const), pl.BlockSpec((1, d_sgu), const),
                  pl.BlockSpec(ws.shape, const3), pl.BlockSpec(bs.shape, const3),
                  pl.BlockSpec(wout.shape, const), pl.BlockSpec((1, d), const), pl.BlockSpec((1, d), const),
                  pl.BlockSpec((1, d), const), pl.BlockSpec(wgu.shape, const), pl.BlockSpec(wd.shape, const)],
        out_specs=[pl.BlockSpec((tm, d), row), pl.BlockSpec((tm, d_sgu), row)],
        out_shape=[jax.ShapeDtypeStruct((n, d), F32), jax.ShapeDtypeStruct((n, d_sgu), F32)],
        scratch_shapes=[pltpu.VMEM((tm, d_sgu), BF16), pltpu.VMEM((tm, wd.shape[0]), BF16)],
        compiler_params=_cparams(("arbitrary",)),
        name="odd_mixer_ffn",
    )(x, g, win, lng, lnb, ws, bs, wout, gmix, gpre, gpost, wgu, wd)


def _tile_rows(n, want):
    tm = min(want, n)
    while n % tm:
        tm //= 2
    return tm


def _from_feature_major(a, heads):
    b, f, t = a.shape
    if heads is None:
        return jnp.transpose(a, (0, 2, 1))
    return jnp.transpose(a.reshape(b, heads, f // heads, t), (0, 3, 1, 2))


def kernel(x_prompt, x_sample, cache_k, cache_v, cache_idx_k, state_pool, page_table, norm_mix_pre, norm_mix_post, norm_ffn_pre, norm_ffn_post, w_in_even, w_out_even, w_pool_group, pool_scale, w_in_odd, sgu_norm_g, sgu_norm_b, w_spatial, b_spatial, w_out_odd, w_ffn_gate_up, w_ffn_down):
    bp, tp, d = x_prompt.shape
    bs, ts, _ = x_sample.shape
    n_pages = page_table.shape[1]
    past_len = n_pages * PAGE_SIZE
    topk_p = min(TOPK_MAX, tp // 4)
    topk_s = min(TOPK_MAX, (past_len + ts) // 4)
    depth = norm_mix_pre.shape[0]
    assert ts == SUBLANES and tp % Q_BLOCK == 0 and d % LANES == 0

    hp = x_prompt.reshape(bp * tp, d)
    hs = x_sample.reshape(bs * ts, d)
    tm_p = _tile_rows(bp * tp, 512)
    tm_p_seq = _tile_rows(tp, 512)
    tm_s = _tile_rows(bs * ts, 256)
    row = lambda a: a.reshape(1, -1)

    tabs_p = _rope_tables(np.arange(tp))
    tabs_s = _rope_tables(np.tile(past_len + np.arange(ts), tm_s // ts))

    outs_p, outs_s, sgu_s = [], [], []
    for layer in range(depth):
        li = layer // 2
        gpre, gpost = row(norm_ffn_pre[layer]), row(norm_ffn_post[layer])
        wgu = w_ffn_gate_up[layer].astype(BF16)
        wd = w_ffn_down[layer].astype(BF16)
        gmix_pre, gmix_post = row(norm_mix_pre[layer]), row(norm_mix_post[layer])
        if layer % 2 == 0:
            wmain, wfeat, wiwt = _arrange_w_in_even(w_in_even[li])
            wpg = w_pool_group[li].astype(BF16)
            psc = row(pool_scale[li])
            wo = w_out_even[li].astype(BF16)
            (q, iq, ik2, kb, kt, vt, vtb, ikt, iwt, pool, xp_tail) = _even_in(
                hp, gmix_pre, wmain, wfeat, wiwt, tabs_p, wpg, psc, None, nb=bp, tm=tm_p_seq, past_len=0)
            attn = _dsa_prompt(q, iq, iwt, kb, vtb, ik2, nb=bp, tk=min(512, tp), topk=topk_p)
            hp = _even_out(hp, attn, pool, wo, gmix_post, gpre, gpost, wgu, wd, tm=tm_p)
            outs_p.append((_from_feature_major(kt, N_KV_HEADS), _from_feature_major(vt, N_KV_HEADS),
                           _from_feature_major(ikt, None), xp_tail[:, 1:, :]))
            prefix = jnp.pad(state_pool[li], ((0, 0), (1, 0), (0, 0)))
            (q, iq, ik2, kb, kt, vt, vtb, ikt, iwt, pool, xp) = _even_in(
                hs, gmix_pre, wmain, wfeat, wiwt, tabs_s, wpg, psc, prefix, nb=1, tm=tm_s, past_len=past_len)
            del kb, vtb
            k_new = _from_feature_major(kt.reshape(1, D_KV, bs * ts), None).reshape(bs, ts, D_KV)
            v_new = _from_feature_major(vt.reshape(1, D_KV, bs * ts), None).reshape(bs, ts, D_KV)
            ik_new = _from_feature_major(ikt, None).reshape(bs, ts, D_IDX)
            q3 = q.reshape(bs, ts, 4, 2, HEAD_DIM)
            zq = jnp.zeros_like(q3[:, :, :, 0])
            qg = jnp.concatenate([jnp.concatenate([q3[:, :, :, 0], zq], axis=-1),
                                  jnp.concatenate([zq, q3[:, :, :, 1]], axis=-1)], axis=2)
            qd = jnp.transpose(qg, (0, 2, 1, 3)).reshape(bs, N_HEADS * ts, D_KV)
            iqd = jnp.transpose(iq.reshape(bs, ts, N_IDX_HEADS, D_IDX), (0, 2, 1, 3)).reshape(bs, N_IDX_HEADS * ts, D_IDX)
            wcol = jnp.transpose(iwt[0].reshape(N_IDX_HEADS, bs, ts), (1, 0, 2)).reshape(bs, N_IDX_HEADS * ts, 1)
            ci_t = jnp.transpose(cache_idx_k[li], (0, 2, 1))
            ck_t = jnp.transpose(cache_k[li], (0, 2, 3, 1)).reshape(-1, D_KV, PAGE_SIZE)
            cv_t = jnp.transpose(cache_v[li], (0, 2, 3, 1)).reshape(-1, D_KV, PAGE_SIZE)
            attn_s = _dsa_decode(page_table, ci_t, ck_t, cv_t, iqd, wcol, ik_new, qd, k_new, v_new, topk=topk_s)
            hs = _even_out(hs, attn_s.astype(BF16), pool, wo, gmix_post, gpre, gpost, wgu, wd, tm=tm_s)
            new_pool_s = jnp.concatenate([state_pool[li][:, ts:, :], xp.reshape(bs, ts, -1)], axis=1)
            outs_s.append((k_new.reshape(bs, ts, N_KV_HEADS, HEAD_DIM), v_new.reshape(bs, ts, N_KV_HEADS, HEAD_DIM),
                           ik_new, new_pool_s))
        else:
            win = w_in_odd[li].astype(BF16)
            wout = w_out_odd[li].astype(BF16)
            lng, lnb = row(sgu_norm_g[li]), row(sgu_norm_b[li])
            ws_p = w_spatial[li][:, :CHUNK, :CHUNK]
            bs_p = b_spatial[li][:, :CHUNK, None]
            hp, _ = _odd(hp, gmix_pre, win, lng, lnb, ws_p, bs_p, wout, gmix_post, gpre, gpost, wgu, wd,
                         tm=tm_p, seq_len=CHUNK)
            reps = CHUNK // ts
            ws_s = jnp.tile(w_spatial[li][:, :ts, :ts], (1, reps, reps))
            bs_s = jnp.tile(b_spatial[li][:, :ts], (1, reps))[:, :, None]
            hs, zs = _odd(hs, gmix_pre, win, lng, lnb, ws_s, bs_s, wout, gmix_post, gpre, gpost, wgu, wd,
                          tm=_tile_rows(bs * ts, 256), seq_len=ts)
            sgu_s.append(zs.reshape(bs, ts, -1))

    return (hp.reshape(bp, tp, d), hs.reshape(bs, ts, d),
            jnp.stack([o[0] for o in outs_p]), jnp.stack([o[1] for o in outs_p]),
            jnp.stack([o[2] for o in outs_p]), jnp.stack([o[3] for o in outs_p]),
            jnp.stack([o[0] for o in outs_s]), jnp.stack([o[1] for o in outs_s]),
            jnp.stack([o[2] for o in outs_s]), jnp.stack([o[3] for o in outs_s]),
            jnp.stack(sgu_s))
```

```python
import functools

import jax
import jax.numpy as jnp
import numpy as np
from jax import lax
from jax.experimental import pallas as pl
from jax.experimental.pallas import tpu as pltpu

EPS = 1e-6
N_HEADS = 8
HEAD_DIM = 64
N_KV_HEADS = 2
KV_GROUP = N_HEADS // N_KV_HEADS
D_ATTN = N_HEADS * HEAD_DIM
D_KV = N_KV_HEADS * HEAD_DIM
N_IDX_HEADS = 8
D_IDX = 64
TOPK_MAX = 256
Q_BLOCK = 128
ROPE_THETA = 10000.0
POOL_WINDOWS = (2, 4, 8, 16)
POOL_BUF = 15
PAGE_SIZE = 128
CHUNK = 128
N_SGU_HEADS = 8
D_POOL_GROUP = 128
V_ROWS = HEAD_DIM + 16

LANES = 128
SUBLANES = 8
VMEM_LIMIT = 56 * 1024 * 1024
INT_MIN = -2 ** 31
NEG_BIG = -1e30
LOG2E = 1.4426950408889634

F32 = jnp.float32
BF16 = jnp.bfloat16
I32 = jnp.int32

_NT = (((1,), (1,)), ((), ()))


def _cparams(sem):
    return pltpu.CompilerParams(dimension_semantics=sem, vmem_limit_bytes=VMEM_LIMIT)


def _rms(x, g):
    return x * lax.rsqrt(jnp.mean(x * x, axis=-1, keepdims=True) + EPS) * g


def _dot(a, b):
    return jnp.dot(a, b, preferred_element_type=F32)


def _dot_nt(a, b):
    return lax.dot_general(a, b, _NT, preferred_element_type=F32)


def _gelu(x):
    return 0.5 * x * (1.0 + lax.erf(x * (2.0 ** -0.5)))


C_Q, C_IQ, C_IK2, C_K, C_XP, C_END = 0, 512, 1024, 1280, 1408, 1920
R_K, R_V, R_IK, R_END = 0, 128, 256, 320


def _rope_tile(t, cos, sin, first_half):
    partner = jnp.where(first_half, pltpu.roll(t, 96, 1), pltpu.roll(t, 32, 1))
    return t * cos + partner * sin


def _rope_rows(t, cos_t, sin_t):
    half = HEAD_DIM // 2
    x1, x2 = t[0:half], t[half:HEAD_DIM]
    return jnp.concatenate([x1 * cos_t - x2 * sin_t, x2 * cos_t + x1 * sin_t], axis=0)


def _even_in_kernel(*refs, tm, decode, past_len):
    if decode:
        (x_ref, g_ref, w_ref, wt_ref, wiwt_ref, cos_ref, sin_ref, cost_ref, sint_ref, wpg_ref, psc_ref, pre_ref,
         q_ref, iq_ref, ik2_ref, kb_ref, kt_ref, vt_ref, vtb_ref, ikt_ref, iwt_ref, pool_ref, xp_ref,
         ext_s) = refs
    else:
        (x_ref, g_ref, w_ref, wt_ref, wiwt_ref, cos_ref, sin_ref, cost_ref, sint_ref, wpg_ref, psc_ref,
         q_ref, iq_ref, ik2_ref, kb_ref, kt_ref, vt_ref, vtb_ref, ikt_ref, iwt_ref, pool_ref, xp_ref,
         ext_s) = refs
    j = pl.program_id(1)
    xn = _rms(x_ref[...], g_ref[...]).astype(BF16)
    cos = cos_ref[...]
    sin = sin_ref[...]
    lane = lax.broadcasted_iota(I32, (1, LANES), 1)
    first_half = (lane % HEAD_DIM) < (HEAD_DIM // 2)

    def proj_rope(c0, c1):
        t = _dot(xn, w_ref[:, c0:c1])
        return [_rope_tile(t[:, c:c + LANES], cos, sin, first_half) for c in range(0, c1 - c0, LANES)]

    for half in range(2):
        tiles = proj_rope(C_Q + 256 * half, C_Q + 256 * (half + 1))
        for c, t in enumerate(tiles):
            col = 256 * half + LANES * c
            q_ref[:, col:col + LANES] = (t * (HEAD_DIM ** -0.5 * LOG2E)).astype(BF16)
    for half in range(2):
        tiles = proj_rope(C_IQ + 256 * half, C_IQ + 256 * (half + 1))
        for c, t in enumerate(tiles):
            col = 256 * half + LANES * c
            iq_ref[:, col:col + LANES] = t.astype(BF16)
    tiles = proj_rope(C_IK2, C_K)
    ik2_ref[:, 0:LANES] = tiles[0].astype(BF16)
    ik2_ref[:, LANES:2 * LANES] = tiles[1].astype(BF16)
    kb_ref[...] = proj_rope(C_K, C_XP)[0].astype(BF16)

    cos_t = cost_ref[...]
    sin_t = sint_ref[...]
    ft = _dot_nt(wt_ref[...], xn)
    kt = jnp.concatenate([_rope_rows(ft[R_K + h * HEAD_DIM:R_K + (h + 1) * HEAD_DIM], cos_t, sin_t)
                          for h in range(N_KV_HEADS)], axis=0)
    vt = ft[R_V:R_IK]
    kt_ref[0] = kt
    vt_ref[0] = vt
    pad = jnp.concatenate([jnp.ones((1, tm), F32), jnp.zeros((V_ROWS - HEAD_DIM - 1, tm), F32)], axis=0)
    for h in range(N_KV_HEADS):
        vtb_ref[0, h * V_ROWS:h * V_ROWS + HEAD_DIM, :] = vt[h * HEAD_DIM:(h + 1) * HEAD_DIM].astype(BF16)
        vtb_ref[0, h * V_ROWS + HEAD_DIM:(h + 1) * V_ROWS, :] = pad.astype(BF16)
    ikt_ref[0] = _rope_rows(ft[R_IK:R_END], cos_t, sin_t)
    iwt_ref[0] = _dot_nt(wiwt_ref[...], xn) * (N_IDX_HEADS ** -0.5) * (D_IDX ** -0.5)

    xp = _dot(xn, w_ref[:, C_XP:C_END])
    row = lax.broadcasted_iota(I32, (tm, 1), 0)
    if decode:
        ns = tm // SUBLANES
        ext_s[:, 0:16, :] = pre_ref[...]
        ext_s[:, 16:24, :] = xp.reshape(ns, SUBLANES, 4 * D_POOL_GROUP)
        pos = past_len + (row % SUBLANES)
    else:
        @pl.when(j == 0)
        def _():
            ext_s[0:16, :] = jnp.zeros((16, 4 * D_POOL_GROUP), F32)
        ext_s[16:16 + tm, :] = xp
        pos = j * tm + row
    for g, w in enumerate(POOL_WINDOWS):
        cs = slice(g * D_POOL_GROUP, (g + 1) * D_POOL_GROUP)
        tok = xp[:, cs]
        acc = tok
        for i in range(1, w):
            if decode:
                acc = acc + ext_s[:, 16 - i:24 - i, cs].reshape(tm, D_POOL_GROUP)
            else:
                acc = acc + ext_s[16 - i:16 - i + tm, cs]
        cnt = jnp.minimum(w, pos + 1).astype(F32)
        d = acc / cnt - tok
        y = _dot(d.astype(BF16), wpg_ref[g]) * psc_ref[:, cs]
        pool_ref[:, cs] = y.astype(BF16)
    if decode:
        xp_ref[...] = xp
    else:
        ext_s[0:16, :] = xp[tm - 16:tm, :]
        xp_ref[0] = xp[tm - 16:tm, :]


def _rope_tables(pos):
    half = HEAD_DIM // 2
    inv = ROPE_THETA ** (-np.arange(half, dtype=np.float64) / half)
    ang = pos.astype(np.float64)[:, None] * inv[None, :]
    cos32, sin32 = np.cos(ang).astype(np.float32), np.sin(ang).astype(np.float32)
    cos = np.tile(cos32, (1, LANES // half))
    sin = np.tile(np.concatenate([-sin32, sin32], axis=1), (1, LANES // HEAD_DIM))
    return tuple(jnp.asarray(a) for a in (cos, sin, np.ascontiguousarray(cos32.T), np.ascontiguousarray(sin32.T)))


def _arrange_w_in_even(w):
    q = w[:, 0:512].reshape(-1, N_HEADS, HEAD_DIM)
    q = jnp.stack([q[:, 0:4], q[:, 4:8]], axis=2).reshape(-1, 512)
    k = w[:, 512:640]
    v = w[:, 640:768]
    iq = w[:, 768:1280]
    ik = w[:, 1280:1344]
    iw = w[:, 1344:1352]
    xp = w[:, 1352:1864]
    z = jnp.zeros_like(ik)
    main = jnp.concatenate([q, iq, ik, z, z, ik, k, xp], axis=1).astype(BF16)
    feat = jnp.concatenate([k, v, ik], axis=1).T.astype(BF16)
    return main, feat, iw.T.astype(BF16)


def _even_in(x, g, wmain, wfeat, wiwt, tabs, wpg, psc, prefix, *, nb, tm, past_len):
    n, d = x.shape
    t = n // nb
    nt = t // tm
    decode = prefix is not None
    cos, sin, cos_t, sin_t = tabs
    row = lambda b, j: (b * nt + j, 0)
    tab_row = (lambda b, j: (0, 0)) if decode else (lambda b, j: (j, 0))
    tab_col = (lambda b, j: (0, 0)) if decode else (lambda b, j: (0, j))
    const = lambda b, j: (0, 0)
    feat = lambda b, j: (b, 0, j)
    in_specs = [
        pl.BlockSpec((tm, d), row),
        pl.BlockSpec((1, d), const),
        pl.BlockSpec(wmain.shape, const),
        pl.BlockSpec(wfeat.shape, const),
        pl.BlockSpec(wiwt.shape, const),
        pl.BlockSpec((tm, LANES), tab_row),
        pl.BlockSpec((tm, LANES), tab_row),
        pl.BlockSpec((HEAD_DIM // 2, tm), tab_col),
        pl.BlockSpec((HEAD_DIM // 2, tm), tab_col),
        pl.BlockSpec(wpg.shape, lambda b, j: (0, 0, 0)),
        pl.BlockSpec((1, 512), const),
    ]
    args = [x, g, wmain, wfeat, wiwt, cos, sin, cos_t, sin_t, wpg, psc]
    if decode:
        ns = tm // SUBLANES
        in_specs.append(pl.BlockSpec((ns, 16, 512), lambda b, j: (b * nt + j, 0, 0)))
        args.append(prefix)
        xp_shape = jax.ShapeDtypeStruct((n, 512), F32)
        xp_spec = pl.BlockSpec((tm, 512), row)
        scratch = [pltpu.VMEM((ns, 24, 512), F32)]
    else:
        xp_shape = jax.ShapeDtypeStruct((nb, 16, 512), F32)
        xp_spec = pl.BlockSpec((1, 16, 512), lambda b, j: (b, 0, 0))
        scratch = [pltpu.VMEM((tm + 16, 512), F32)]
    out_shape = [
        jax.ShapeDtypeStruct((n, 512), BF16),
        jax.ShapeDtypeStruct((n, 512), BF16),
        jax.ShapeDtypeStruct((n, 256), BF16),
        jax.ShapeDtypeStruct((n, 128), BF16),
        jax.ShapeDtypeStruct((nb, 128, t), F32),
        jax.ShapeDtypeStruct((nb, 128, t), F32),
        jax.ShapeDtypeStruct((nb, N_KV_HEADS * V_ROWS, t), BF16),
        jax.ShapeDtypeStruct((nb, D_IDX, t), F32),
        jax.ShapeDtypeStruct((nb, 8, t), F32),
        jax.ShapeDtypeStruct((n, 512), BF16),
        xp_shape,
    ]
    out_specs = [
        pl.BlockSpec((tm, 512), row),
        pl.BlockSpec((tm, 512), row),
        pl.BlockSpec((tm, 256), row),
        pl.BlockSpec((tm, 128), row),
        pl.BlockSpec((1, 128, tm), feat),
        pl.BlockSpec((1, 128, tm), feat),
        pl.BlockSpec((1, N_KV_HEADS * V_ROWS, tm), feat),
        pl.BlockSpec((1, D_IDX, tm), feat),
        pl.BlockSpec((1, 8, tm), feat),
        pl.BlockSpec((tm, 512), row),
        xp_spec,
    ]
    return pl.pallas_call(
        functools.partial(_even_in_kernel, tm=tm, decode=decode, past_len=past_len),
        grid=(nb, nt), in_specs=in_specs, out_specs=out_specs, out_shape=out_shape,
        scratch_shapes=scratch, compiler_params=_cparams(("arbitrary", "arbitrary")),
        name="even_in_decode" if decode else "even_in_prompt",
    )(*args)


F32_MIN_NORMAL = 2.0 ** -126
F32_LOWEST = -3.4028234663852886e38
LOWEST_FINITE_CODE = -2139095040


def _score_for_search(s, valid):
    return jnp.where(valid, jnp.where(jnp.abs(s) < F32_MIN_NORMAL, 0.0, s), -jnp.inf)


def _code_to_float(code):
    b = code ^ ((code >> 31) & 0x7FFFFFFF)
    f = lax.bitcast_convert_type(b, F32)
    tiny = ((b >> 23) & 0xFF) == 0
    return jnp.where(tiny, jnp.where(b > 0, F32_MIN_NORMAL, 0.0), f)


FRACTION_BITS = 24


def _topk_search(count, shape, n_index_bits, topk, first_index=None):
    def bit_step(i, carry):
        code, c_ge = carry
        cand = code + jnp.left_shift(jnp.int32(1), 31 - i)
        cf = _code_to_float(cand)
        c = count(lambda sv, idx: sv >= cf)
        ok = c >= topk
        return jnp.where(ok, cand, code), jnp.where(ok, c, c_ge)

    code, c_ge = lax.fori_loop(0, 32, bit_step, (jnp.full(shape, INT_MIN, I32), jnp.zeros(shape, I32)))
    short = code < LOWEST_FINITE_CODE
    thr = jnp.where(short, -jnp.inf, _code_to_float(code))
    nxt = jnp.where(short, F32_LOWEST, _code_to_float(code + 1))
    c_above = count(lambda sv, idx: sv >= nxt)
    need = topk - c_above
    ambiguous = jnp.logical_and(c_ge - c_above > need, jnp.logical_not(short))
    width = nxt - thr
    inv_width = jnp.where(jnp.logical_and(width >= F32_MIN_NORMAL, width < jnp.inf), 1.0 / width, 0.0)

    def frac(sv):
        return (sv - thr) * inv_width

    def in_band(sv):
        return jnp.logical_and(sv >= thr, sv < nxt)

    def split_band():
        c_pos = count(lambda sv, idx: jnp.logical_and(in_band(sv), frac(sv) > 0.0))

        def bisect_fraction():
            def frac_step(i, m):
                cand = m + jnp.left_shift(jnp.int32(1), FRACTION_BITS - 1 - i)
                cf = cand.astype(F32) * (2.0 ** -FRACTION_BITS)
                c = count(lambda sv, idx: jnp.logical_and(in_band(sv), frac(sv) >= cf))
                return jnp.where(c >= need, cand, m)
            m = lax.fori_loop(0, FRACTION_BITS, frac_step, jnp.zeros(shape, I32))
            r = m.astype(F32) * (2.0 ** -FRACTION_BITS)
            return r, count(lambda sv, idx: jnp.logical_and(in_band(sv), frac(sv) > r))

        any_frac = jnp.max(jnp.where(ambiguous, c_pos, 0)) > 0
        rstar, c_gtr = lax.cond(any_frac, bisect_fraction, lambda: (jnp.zeros(shape, F32), c_pos))
        need_eq = need - c_gtr

        def index_cut(tied):
            def idx_step(i, lo):
                cand = lo + jnp.left_shift(jnp.int32(1), n_index_bits - 1 - i)
                c = count(lambda sv, idx: jnp.logical_and(tied(sv), idx < cand))
                return jnp.where(c < need_eq, cand, lo)
            return lax.fori_loop(0, n_index_bits, idx_step, jnp.zeros(shape, I32))

        def plain_cut():
            tied = lambda sv: sv == thr
            if first_index is None:
                return index_cut(tied)
            single = jnp.max(jnp.where(ambiguous, need_eq, 1)) == 1
            return lax.cond(single, lambda: first_index(tied), lambda: index_cut(tied))

        cut = lax.cond(any_frac,
                       lambda: index_cut(lambda sv: jnp.logical_and(in_band(sv), frac(sv) == rstar)),
                       plain_cut)
        return jnp.where(ambiguous, rstar, -1.0), jnp.where(ambiguous, cut, take_all), any_frac

    take_all = jnp.where(short, -1, 2 ** 30).astype(I32)
    any_amb = jnp.max(ambiguous.astype(I32)) > 0
    rstar, cut, any_frac = lax.cond(any_amb, split_band,
                                    lambda: (jnp.full(shape, -1.0, F32), take_all, jnp.zeros((), jnp.bool_)))

    def select(sv, idx):
        r = frac(sv)
        taken = jnp.logical_or(r > rstar, jnp.logical_and(r == rstar, idx <= cut))
        return jnp.logical_or(sv >= nxt, jnp.logical_and(in_band(sv), taken))

    def select_no_fraction(sv, idx):
        return jnp.logical_or(sv > thr, jnp.logical_and(sv == thr, idx <= cut))

    return select, select_no_fraction, jnp.maximum(thr, F32_LOWEST), any_amb, any_frac


def _index_bits(n):
    return max(1, int(np.ceil(np.log2(n))))


def _fold_rows(x, op):
    return op(x.reshape(x.shape[0] // SUBLANES, SUBLANES, x.shape[1]), axis=0)


def _dsa_prompt_kernel(q_ref, iq_ref, iwt_ref, kb_ref, vt_ref, ik2_ref, o_ref,
                       keys_s, qm_s, iqs_s, acc_s, *, tk, topk):
    i = pl.program_id(1)
    n_tiles = (i * Q_BLOCK + Q_BLOCK + tk - 1) // tk
    lane = lax.broadcasted_iota(I32, (1, LANES), 1)
    qpos = i * Q_BLOCK + lane
    qpos_l = qpos
    lo_half = lane < HEAD_DIM
    nq = KV_GROUP * Q_BLOCK

    for jj in range(KV_GROUP):
        qt = q_ref[:, jj * LANES:(jj + 1) * LANES]
        zero = jnp.zeros_like(qt)
        qm_s[0, jj * Q_BLOCK:(jj + 1) * Q_BLOCK, :] = jnp.where(lo_half, qt, zero)
        qm_s[1, jj * Q_BLOCK:(jj + 1) * Q_BLOCK, :] = jnp.where(lo_half, zero, qt)
        iqs_s[jj * Q_BLOCK:(jj + 1) * Q_BLOCK, :] = iq_ref[:, jj * LANES:(jj + 1) * LANES]
    w = iwt_ref[0]

    def score_tile(t, carry):
        r0 = pl.multiple_of(t * tk, tk)
        ik2 = ik2_ref[pl.ds(r0, tk), :]
        iqs = iqs_s[...]
        rel_e = _dot_nt(ik2[:, 0:LANES], iqs)
        rel_o = _dot_nt(ik2[:, LANES:2 * LANES], iqs)
        s = jnp.zeros((tk, LANES), F32)
        for jj in range(4):
            cs = slice(jj * LANES, (jj + 1) * LANES)
            s = s + w[2 * jj:2 * jj + 1, :] * jnp.maximum(rel_e[:, cs], 0.0)
            s = s + w[2 * jj + 1:2 * jj + 2, :] * jnp.maximum(rel_o[:, cs], 0.0)
        kpos = r0 + lax.broadcasted_iota(I32, (tk, 1), 0)
        keys_s[pl.ds(r0, tk), :] = _score_for_search(s, kpos <= qpos_l)
        return carry

    lax.fori_loop(0, n_tiles, score_tile, 0)

    def count(pred):
        def body(t, acc):
            r0 = pl.multiple_of(t * tk, tk)
            sv = keys_s[pl.ds(r0, tk), :]
            idx = r0 + lax.broadcasted_iota(I32, (tk, 1), 0)
            return acc + _fold_rows(pred(sv, idx).astype(I32), jnp.sum)
        acc = lax.fori_loop(0, n_tiles, body, jnp.zeros((SUBLANES, LANES), I32))
        return acc.sum(axis=0, keepdims=True)

    def first_index(tied):
        far = jnp.int32(2 ** 30)
        def body(t, acc):
            r0 = pl.multiple_of(t * tk, tk)
            idx = r0 + lax.broadcasted_iota(I32, (tk, 1), 0)
            return jnp.minimum(acc, _fold_rows(jnp.where(tied(keys_s[pl.ds(r0, tk), :]), idx, far), jnp.min))
        acc = lax.fori_loop(0, n_tiles, body, jnp.full((SUBLANES, LANES), far, I32))
        return acc.min(axis=0, keepdims=True)

    select, select_no_fraction, thr_ge, any_cut, any_frac = _topk_search(
        count, (1, LANES), _index_bits(keys_s.shape[0]), topk, first_index)

    def attend(selector):
        def pv_tile(t, ms):
            r0 = pl.multiple_of(t * tk, tk)
            sv = keys_s[pl.ds(r0, tk), :]
            if selector is None:
                sel = sv >= thr_ge
            else:
                sel = selector(sv, r0 + lax.broadcasted_iota(I32, (tk, 1), 0))
            sel = jnp.concatenate([sel] * KV_GROUP, axis=1)
            kt = kb_ref[pl.ds(r0, tk), :]
            sts = [jnp.where(sel, _dot_nt(kt, qm_s[g]), NEG_BIG) for g in range(N_KV_HEADS)]
            ms_new = []
            for g, st in enumerate(sts):
                m_new = jnp.maximum(ms[g], jnp.max(_fold_rows(st, jnp.max), axis=0, keepdims=True))
                alpha = jnp.exp2(ms[g] - m_new)
                p = jnp.exp2(st - m_new)
                ms_new.append(m_new)
                vt = vt_ref[0, g * V_ROWS:(g + 1) * V_ROWS, pl.ds(r0, tk)]
                acc_s[g] = acc_s[g] * alpha + _dot(vt, p.astype(BF16))
            return tuple(ms_new)

        lax.fori_loop(0, n_tiles, pv_tile, tuple(jnp.full((1, nq), NEG_BIG, F32) for _ in range(N_KV_HEADS)))

    acc_s[...] = jnp.zeros(acc_s.shape, F32)
    lax.cond(any_cut,
             lambda: lax.cond(any_frac, lambda: attend(select), lambda: attend(select_no_fraction)),
             lambda: attend(None))
    heads = []
    for g in range(N_KV_HEADS):
        og = acc_s[g, 0:HEAD_DIM, :] / acc_s[g, HEAD_DIM:HEAD_DIM + 1, :]
        heads += [og[:, jj * Q_BLOCK:(jj + 1) * Q_BLOCK] for jj in range(KV_GROUP)]
    o_ref[...] = jnp.concatenate(heads, axis=0).T.astype(o_ref.dtype)


def _store_heads(o_ref, heads, lo_half):
    for jj in range(N_HEADS // 2):
        a, b = heads[2 * jj], heads[2 * jj + 1]
        if 2 * jj < KV_GROUP:
            tile = jnp.where(lo_half, a, pltpu.roll(b, HEAD_DIM, 1))
        else:
            tile = jnp.where(lo_half, pltpu.roll(a, HEAD_DIM, 1), b)
        o_ref[:, jj * LANES:(jj + 1) * LANES] = tile.astype(o_ref.dtype)


def _dsa_prompt(q, iq, iwt, kb, vtb, ik2, *, nb, tk, topk):
    n = q.shape[0]
    t = n // nb
    nq = t // Q_BLOCK
    assert t % tk == 0
    blk = lambda b, i: (b * nq + i, 0)
    seq = lambda b, i: (b, 0)
    feat = lambda b, i: (b, 0, 0)
    return pl.pallas_call(
        functools.partial(_dsa_prompt_kernel, tk=tk, topk=topk),
        grid=(nb, nq),
        in_specs=[
            pl.BlockSpec((Q_BLOCK, 512), blk),
            pl.BlockSpec((Q_BLOCK, 512), blk),
            pl.BlockSpec((1, 8, Q_BLOCK), lambda b, i: (b, 0, i)),
            pl.BlockSpec((t, 128), seq),
            pl.BlockSpec((1, N_KV_HEADS * V_ROWS, t), feat),
            pl.BlockSpec((t, 256), seq),
        ],
        out_specs=pl.BlockSpec((Q_BLOCK, 512), blk),
        out_shape=jax.ShapeDtypeStruct((n, 512), BF16),
        scratch_shapes=[
            pltpu.VMEM((t, LANES), F32),
            pltpu.VMEM((N_KV_HEADS, KV_GROUP * Q_BLOCK, LANES), BF16),
            pltpu.VMEM((4 * Q_BLOCK, LANES), BF16),
            pltpu.VMEM((N_KV_HEADS, V_ROWS, KV_GROUP * Q_BLOCK), F32),
        ],
        compiler_params=_cparams(("arbitrary", "arbitrary")),
        name="dsa_prompt",
    )(q, iq, iwt, kb, vtb, ik2)


SEQ_GROUP = Q_BLOCK // SUBLANES


def _pad_rows(a):
    return jnp.concatenate([a, jnp.zeros((LANES - a.shape[0], a.shape[1]), a.dtype)], axis=0).astype(BF16)


def _decode_select_kernel(*refs, n_pages, topk, nnew):
    ip = refs[1:1 + n_pages]
    iqd_ref, wcol_ref, inew_ref, bias_ref, keys_s = refs[1 + n_pages:]
    t = pl.program_id(1)
    n_keys = keys_s.shape[1]
    past = n_pages * PAGE_SIZE
    n_tiles = n_keys // LANES
    lane = lax.broadcasted_iota(I32, (1, LANES), 1)
    qrow = lax.broadcasted_iota(I32, (nnew, 1), 0)
    new_valid = jnp.logical_and(lane <= qrow, lane < nnew)

    def head_sum(x):
        return x.reshape(N_IDX_HEADS, nnew, x.shape[1]).sum(axis=0)

    r0 = pl.multiple_of(t * nnew, nnew)
    iqd = iqd_ref[0]
    wcol = wcol_ref[0]
    for p in range(n_pages):
        rel = _dot(iqd, ip[p][0].astype(BF16))
        s = head_sum(wcol * jnp.maximum(rel, 0.0))
        keys_s[pl.ds(r0, nnew), p * LANES:(p + 1) * LANES] = _score_for_search(s, True)
    rel = _dot_nt(iqd, _pad_rows(inew_ref[0]))
    s = head_sum(wcol * jnp.maximum(rel, 0.0))
    keys_s[pl.ds(r0, nnew), past:past + LANES] = _score_for_search(s, new_valid)

    @pl.when(t == SEQ_GROUP - 1)
    def _search():
        def count(pred):
            acc = jnp.zeros((Q_BLOCK, LANES), I32)
            for c in range(n_tiles):
                idx = c * LANES + lane
                acc = acc + pred(keys_s[:, c * LANES:(c + 1) * LANES], idx).astype(I32)
            return acc.sum(axis=1, keepdims=True)
        select, _, _, _, _ = _topk_search(count, (Q_BLOCK, 1), _index_bits(n_keys), topk)
        for c in range(n_tiles):
            sel = select(keys_s[:, c * LANES:(c + 1) * LANES], c * LANES + lane)
            bias_ref[:, c * LANES:(c + 1) * LANES] = jnp.where(sel, 0.0, NEG_BIG)


def _decode_attend_kernel(*refs, n_pages, nnew):
    kp = refs[1:1 + n_pages]
    vp = refs[1 + n_pages:1 + 2 * n_pages]
    qd_ref, knew_ref, vnew_ref, bias_ref, o_ref = refs[1 + 2 * n_pages:]
    past = n_pages * PAGE_SIZE
    lane = lax.broadcasted_iota(I32, (1, LANES), 1)
    bias = jnp.concatenate([bias_ref[...]] * N_HEADS, axis=0)
    qd = qd_ref[0]
    st = jnp.concatenate([_dot(qd, kp[p][0].astype(BF16)) for p in range(n_pages)]
                         + [_dot_nt(qd, _pad_rows(knew_ref[0]))], axis=1) + bias
    m = jnp.max(st, axis=1, keepdims=True)
    pr = jnp.exp2(st - m)
    l = jnp.sum(pr, axis=1, keepdims=True)
    pb = pr.astype(BF16)
    o = _dot(pb[:, past:past + LANES], _pad_rows(vnew_ref[0]))
    for p in range(n_pages):
        o = o + _dot_nt(pb[:, p * LANES:(p + 1) * LANES], vp[p][0].astype(BF16))
    o = o / l
    _store_heads(o_ref, [o[h * nnew:(h + 1) * nnew] for h in range(N_HEADS)], lane < HEAD_DIM)


def _dsa_decode(page_table, ci_t, ck_t, cv_t, iqd, wcol, inew, qd, knew, vnew, *, topk):
    ns, n_pages = page_table.shape
    nnew = knew.shape[1]
    assert ns % SEQ_GROUP == 0 and nnew == SUBLANES
    n_keys = (n_pages + 1) * PAGE_SIZE

    sel_page = lambda p: pl.BlockSpec((1, D_IDX, PAGE_SIZE),
                                      lambda g, t, pt, p=p: (pt[g * SEQ_GROUP + t, p], 0, 0))
    sel3 = lambda g, t, pt: (g * SEQ_GROUP + t, 0, 0)
    bias = pl.pallas_call(
        functools.partial(_decode_select_kernel, n_pages=n_pages, topk=topk, nnew=nnew),
        grid_spec=pltpu.PrefetchScalarGridSpec(
            num_scalar_prefetch=1, grid=(ns // SEQ_GROUP, SEQ_GROUP),
            in_specs=([sel_page(p) for p in range(n_pages)]
                      + [pl.BlockSpec((1, N_IDX_HEADS * nnew, D_IDX), sel3),
                         pl.BlockSpec((1, N_IDX_HEADS * nnew, 1), sel3),
                         pl.BlockSpec((1, nnew, D_IDX), sel3)]),
            out_specs=pl.BlockSpec((Q_BLOCK, n_keys), lambda g, t, pt: (g, 0)),
            scratch_shapes=[pltpu.VMEM((Q_BLOCK, n_keys), F32)]),
        out_shape=jax.ShapeDtypeStruct((ns * nnew, n_keys), F32),
        compiler_params=_cparams(("arbitrary", "arbitrary")),
        name="dsa_decode_select",
    )(page_table, *([ci_t] * n_pages), iqd, wcol, inew)

    att_page = lambda p: pl.BlockSpec((1, D_KV, PAGE_SIZE), lambda s, pt, p=p: (pt[s, p], 0, 0))
    att3 = lambda s, pt: (s, 0, 0)
    return pl.pallas_call(
        functools.partial(_decode_attend_kernel, n_pages=n_pages, nnew=nnew),
        grid_spec=pltpu.PrefetchScalarGridSpec(
            num_scalar_prefetch=1, grid=(ns,),
            in_specs=([att_page(p) for p in range(n_pages)] + [att_page(p) for p in range(n_pages)]
                      + [pl.BlockSpec((1, N_HEADS * nnew, D_KV), att3),
                         pl.BlockSpec((1, nnew, D_KV), att3),
                         pl.BlockSpec((1, nnew, D_KV), att3),
                         pl.BlockSpec((nnew, n_keys), lambda s, pt: (s, 0))]),
            out_specs=pl.BlockSpec((nnew, D_ATTN), lambda s, pt: (s, 0))),
        out_shape=jax.ShapeDtypeStruct((ns * nnew, D_ATTN), F32),
        compiler_params=_cparams(("arbitrary",)),
        name="dsa_decode_attend",
    )(page_table, *([ck_t] * n_pages), *([cv_t] * n_pages), qd, knew, vnew, bias)


FF_CHUNK = 256


def _ffn_block(h, gpre, gpost, wgu_ref, wd_ref, a_s):
    d_ff = wd_ref.shape[0]
    hn = _rms(h, gpre).astype(BF16)
    for c in range(0, d_ff, FF_CHUNK):
        gate = _dot(hn, wgu_ref[:, c:c + FF_CHUNK])
        up = _dot(hn, wgu_ref[:, d_ff + c:d_ff + c + FF_CHUNK])
        a_s[:, c:c + FF_CHUNK] = (gate * jax.nn.sigmoid(gate) * up).astype(BF16)
    return h + _rms(_dot(a_s[...], wd_ref[...]), gpost)


def _even_out_kernel(x_ref, attn_ref, pool_ref, wo_ref, gmix_ref, gpre_ref, gpost_ref, wgu_ref, wd_ref,
                     o_ref, a_s):
    mix = _dot(attn_ref[...], wo_ref[0:D_ATTN, :]) + _dot(pool_ref[...], wo_ref[D_ATTN:, :])
    h = x_ref[...] + _rms(mix, gmix_ref[...])
    o_ref[...] = _ffn_block(h, gpre_ref[...], gpost_ref[...], wgu_ref, wd_ref, a_s)


def _even_out(x, attn, pool, wo, gmix, gpre, gpost, wgu, wd, *, tm):
    n, d = x.shape
    row = lambda i: (i, 0)
    const = lambda i: (0, 0)
    return pl.pallas_call(
        _even_out_kernel, grid=(n // tm,),
        in_specs=[pl.BlockSpec((tm, d), row), pl.BlockSpec((tm, 512), row), pl.BlockSpec((tm, 512), row),
                  pl.BlockSpec(wo.shape, const), pl.BlockSpec((1, d), const), pl.BlockSpec((1, d), const),
                  pl.BlockSpec((1, d), const), pl.BlockSpec(wgu.shape, const), pl.BlockSpec(wd.shape, const)],
        out_specs=pl.BlockSpec((tm, d), row),
        out_shape=jax.ShapeDtypeStruct((n, d), F32),
        scratch_shapes=[pltpu.VMEM((tm, wd.shape[0]), BF16)],
        compiler_params=_cparams(("arbitrary",)),
        name="even_out_ffn",
    )(x, attn, pool, wo, gmix, gpre, gpost, wgu, wd)


def _odd_kernel(x_ref, g_ref, win_ref, lng_ref, lnb_ref, ws_ref, bs_ref, wout_ref, gmix_ref,
                gpre_ref, gpost_ref, wgu_ref, wd_ref, o_ref, z_ref, y_s, a_s, *, tm, seq_len):
    d_sgu = wout_ref.shape[0]
    dh = d_sgu // N_SGU_HEADS
    x = x_ref[...]
    xn = _rms(x, g_ref[...]).astype(BF16)
    r = lax.broadcasted_iota(I32, (CHUNK, CHUNK), 0)
    c = lax.broadcasted_iota(I32, (CHUNK, CHUNK), 1)
    causal = jnp.logical_and(r // seq_len == c // seq_len, c <= r)
    for hd in range(N_SGU_HEADS):
        cs = slice(hd * dh, (hd + 1) * dh)
        u = _gelu(_dot(xn, win_ref[:, cs]))
        y_s[:, cs] = u.astype(BF16)
    vs = []
    for hd in range(N_SGU_HEADS):
        cs = slice(d_sgu + hd * dh, d_sgu + (hd + 1) * dh)
        vs.append(_gelu(_dot(xn, win_ref[:, cs])))
    v = jnp.concatenate(vs, axis=1)
    mu = jnp.mean(v, axis=-1, keepdims=True)
    vc = v - mu
    z = vc * lax.rsqrt(jnp.mean(vc * vc, axis=-1, keepdims=True) + EPS) * lng_ref[...] + lnb_ref[...]
    z_ref[...] = z
    zb = z.astype(BF16)
    for hd in range(N_SGU_HEADS):
        cs = slice(hd * dh, (hd + 1) * dh)
        wm = jnp.where(causal, ws_ref[hd], 0.0).astype(BF16)
        bias = bs_ref[hd]
        for ch in range(tm // CHUNK):
            rs = slice(ch * CHUNK, (ch + 1) * CHUNK)
            s = _dot(wm, zb[rs, cs]) + bias
            y_s[rs, cs] = (y_s[rs, cs].astype(F32) * s).astype(BF16)
    h = x + _rms(_dot(y_s[...], wout_ref[...]), gmix_ref[...])
    o_ref[...] = _ffn_block(h, gpre_ref[...], gpost_ref[...], wgu_ref, wd_ref, a_s)


def _odd(x, g, win, lng, lnb, ws, bs, wout, gmix, gpre, gpost, wgu, wd, *, tm, seq_len):
    n, d = x.shape
    d_sgu = wout.shape[0]
    row = lambda i: (i, 0)
    const = lambda i: (0, 0)
    const3 = lambda i: (0, 0, 0)
    return pl.pallas_call(
        functools.partial(_odd_kernel, tm=tm, seq_len=seq_len), grid=(n // tm,),
        in_specs=[pl.BlockSpec((tm, d), row), pl.BlockSpec((1, d), const), pl.BlockSpec(win.shape, const),
                  pl.BlockSpec((1, d_sgu), const), pl.BlockSpec((1, d_sgu), const),
                  pl.BlockSpec(ws.shape, const3), pl.BlockSpec(bs.shape, const3),
                  pl.BlockSpec(wout.shape, const), pl.BlockSpec((1, d), const), pl.BlockSpec((1, d), const),
                  pl.BlockSpec((1, d), const), pl.BlockSpec(wgu.shape, const), pl.BlockSpec(wd.shape, const)],
        out_specs=[pl.BlockSpec((tm, d), row), pl.BlockSpec((tm, d_sgu), row)],
        out_shape=[jax.ShapeDtypeStruct((n, d), F32), jax.ShapeDtypeStruct((n, d_sgu), F32)],
        scratch_shapes=[pltpu.VMEM((tm, d_sgu), BF16), pltpu.VMEM((tm, wd.shape[0]), BF16)],
        compiler_params=_cparams(("arbitrary",)),
        name="odd_mixer_ffn",
    )(x, g, win, lng, lnb, ws, bs, wout, gmix, gpre, gpost, wgu, wd)


def _tile_rows(n, want):
    tm = min(want, n)
    while n % tm:
        tm //= 2
    return tm


def _from_feature_major(a, heads):
    b, f, t = a.shape
    if heads is None:
        return jnp.transpose(a, (0, 2, 1))
    return jnp.transpose(a.reshape(b, heads, f // heads, t), (0, 3, 1, 2))


def kernel(x_prompt, x_sample, cache_k, cache_v, cache_idx_k, state_pool, page_table, norm_mix_pre, norm_mix_post, norm_ffn_pre, norm_ffn_post, w_in_even, w_out_even, w_pool_group, pool_scale, w_in_odd, sgu_norm_g, sgu_norm_b, w_spatial, b_spatial, w_out_odd, w_ffn_gate_up, w_ffn_down):
    bp, tp, d = x_prompt.shape
    bs, ts, _ = x_sample.shape
    n_pages = page_table.shape[1]
    past_len = n_pages * PAGE_SIZE
    topk_p = min(TOPK_MAX, tp // 4)
    topk_s = min(TOPK_MAX, (past_len + ts) // 4)
    depth = norm_mix_pre.shape[0]
    assert ts == SUBLANES and tp % Q_BLOCK == 0 and d % LANES == 0

    hp = x_prompt.reshape(bp * tp, d)
    hs = x_sample.reshape(bs * ts, d)
    tm_p = _tile_rows(bp * tp, 512)
    tm_p_seq = _tile_rows(tp, 512)
    tm_s = _tile_rows(bs * ts, 256)
    row = lambda a: a.reshape(1, -1)

    tabs_p = _rope_tables(np.arange(tp))
    tabs_s = _rope_tables(np.tile(past_len + np.arange(ts), tm_s // ts))

    outs_p, outs_s, sgu_s = [], [], []
    for layer in range(depth):
        li = layer // 2
        gpre, gpost = row(norm_ffn_pre[layer]), row(norm_ffn_post[layer])
        wgu = w_ffn_gate_up[layer].astype(BF16)
        wd = w_ffn_down[layer].astype(BF16)
        gmix_pre, gmix_post = row(norm_mix_pre[layer]), row(norm_mix_post[layer])
        if layer % 2 == 0:
            wmain, wfeat, wiwt = _arrange_w_in_even(w_in_even[li])
            wpg = w_pool_group[li].astype(BF16)
            psc = row(pool_scale[li])
            wo = w_out_even[li].astype(BF16)
            (q, iq, ik2, kb, kt, vt, vtb, ikt, iwt, pool, xp_tail) = _even_in(
                hp, gmix_pre, wmain, wfeat, wiwt, tabs_p, wpg, psc, None, nb=bp, tm=tm_p_seq, past_len=0)
            attn = _dsa_prompt(q, iq, iwt, kb, vtb, ik2, nb=bp, tk=min(512, tp), topk=topk_p)
            hp = _even_out(hp, attn, pool, wo, gmix_post, gpre, gpost, wgu, wd, tm=tm_p)
            outs_p.append((_from_feature_major(kt, N_KV_HEADS), _from_feature_major(vt, N_KV_HEADS),
                           _from_feature_major(ikt, None), xp_tail[:, 1:, :]))
            prefix = jnp.pad(state_pool[li], ((0, 0), (1, 0), (0, 0)))
            (q, iq, ik2, kb, kt, vt, vtb, ikt, iwt, pool, xp) = _even_in(
                hs, gmix_pre, wmain, wfeat, wiwt, tabs_s, wpg, psc, prefix, nb=1, tm=tm_s, past_len=past_len)
            del kb, vtb
            k_new = _from_feature_major(kt.reshape(1, D_KV, bs * ts), None).reshape(bs, ts, D_KV)
            v_new = _from_feature_major(vt.reshape(1, D_KV, bs * ts), None).reshape(bs, ts, D_KV)
            ik_new = _from_feature_major(ikt, None).reshape(bs, ts, D_IDX)
            q3 = q.reshape(bs, ts, 4, 2, HEAD_DIM)
            zq = jnp.zeros_like(q3[:, :, :, 0])
            qg = jnp.concatenate([jnp.concatenate([q3[:, :, :, 0], zq], axis=-1),
                                  jnp.concatenate([zq, q3[:, :, :, 1]], axis=-1)], axis=2)
            qd = jnp.transpose(qg, (0, 2, 1, 3)).reshape(bs, N_HEADS * ts, D_KV)
            iqd = jnp.transpose(iq.reshape(bs, ts, N_IDX_HEADS, D_IDX), (0, 2, 1, 3)).reshape(bs, N_IDX_HEADS * ts, D_IDX)
            wcol = jnp.transpose(iwt[0].reshape(N_IDX_HEADS, bs, ts), (1, 0, 2)).reshape(bs, N_IDX_HEADS * ts, 1)
            ci_t = jnp.transpose(cache_idx_k[li], (0, 2, 1))
            ck_t = jnp.transpose(cache_k[li], (0, 2, 3, 1)).reshape(-1, D_KV, PAGE_SIZE)
            cv_t = jnp.transpose(cache_v[li], (0, 2, 3, 1)).reshape(-1, D_KV, PAGE_SIZE)
            attn_s = _dsa_decode(page_table, ci_t, ck_t, cv_t, iqd, wcol, ik_new, qd, k_new, v_new, topk=topk_s)
            hs = _even_out(hs, attn_s.astype(BF16), pool, wo, gmix_post, gpre, gpost, wgu, wd, tm=tm_s)
            new_pool_s = jnp.concatenate([state_pool[li][:, ts:, :], xp.reshape(bs, ts, -1)], axis=1)
            outs_s.append((k_new.reshape(bs, ts, N_KV_HEADS, HEAD_DIM), v_new.reshape(bs, ts, N_KV_HEADS, HEAD_DIM),
                           ik_new, new_pool_s))
        else:
            win = w_in_odd[li].astype(BF16)
            wout = w_out_odd[li].astype(BF16)
            lng, lnb = row(sgu_norm_g[li]), row(sgu_norm_b[li])
            ws_p = w_spatial[li][:, :CHUNK, :CHUNK]
            bs_p = b_spatial[li][:, :CHUNK, None]
            hp, _ = _odd(hp, gmix_pre, win, lng, lnb, ws_p, bs_p, wout, gmix_post, gpre, gpost, wgu, wd,
                         tm=tm_p, seq_len=CHUNK)
            reps = CHUNK // ts
            ws_s = jnp.tile(w_spatial[li][:, :ts, :ts], (1, reps, reps))
            bs_s = jnp.tile(b_spatial[li][:, :ts], (1, reps))[:, :, None]
            hs, zs = _odd(hs, gmix_pre, win, lng, lnb, ws_s, bs_s, wout, gmix_post, gpre, gpost, wgu, wd,
                          tm=_tile_rows(bs * ts, 256), seq_len=ts)
            sgu_s.append(zs.reshape(bs, ts, -1))

    return (hp.reshape(bp, tp, d), hs.reshape(bs, ts, d),
            jnp.stack([o[0] for o in outs_p]), jnp.stack([o[1] for o in outs_p]),
            jnp.stack([o[2] for o in outs_p]), jnp.stack([o[3] for o in outs_p]),
            jnp.stack([o[0] for o in outs_s]), jnp.stack([o[1] for o in outs_s]),
            jnp.stack([o[2] for o in outs_s]), jnp.stack([o[3] for o in outs_s]),
            jnp.stack(sgu_s))
```

```python
import functools

import jax
import jax.numpy as jnp
import numpy as np
from jax import lax
from jax.experimental import pallas as pl
from jax.experimental.pallas import tpu as pltpu

EPS = 1e-6
N_HEADS = 8
HEAD_DIM = 64
N_KV_HEADS = 2
KV_GROUP = N_HEADS // N_KV_HEADS
D_ATTN = N_HEADS * HEAD_DIM
D_KV = N_KV_HEADS * HEAD_DIM
N_IDX_HEADS = 8
D_IDX = 64
TOPK_MAX = 256
Q_BLOCK = 128
ROPE_THETA = 10000.0
POOL_WINDOWS = (2, 4, 8, 16)
POOL_BUF = 15
PAGE_SIZE = 128
CHUNK = 128
N_SGU_HEADS = 8
D_POOL_GROUP = 128
V_ROWS = HEAD_DIM + 16

LANES = 128
SUBLANES = 8
VMEM_LIMIT = 56 * 1024 * 1024
INT_MIN = -2 ** 31
NEG_BIG = -1e30
LOG2E = 1.4426950408889634

F32 = jnp.float32
BF16 = jnp.bfloat16
I32 = jnp.int32

_NT = (((1,), (1,)), ((), ()))


def _cparams(sem):
    return pltpu.CompilerParams(dimension_semantics=sem, vmem_limit_bytes=VMEM_LIMIT)


def _rms(x, g):
    return x * lax.rsqrt(jnp.mean(x * x, axis=-1, keepdims=True) + EPS) * g


def _dot(a, b):
    return jnp.dot(a, b, preferred_element_type=F32)


def _dot_nt(a, b):
    return lax.dot_general(a, b, _NT, preferred_element_type=F32)


def _gelu(x):
    return 0.5 * x * (1.0 + lax.erf(x * (2.0 ** -0.5)))


C_Q, C_IQ, C_IK2, C_K, C_XP, C_END = 0, 512, 1024, 1280, 1408, 1920
R_K, R_V, R_IK, R_END = 0, 128, 256, 320


def _rope_tile(t, cos, sin, first_half):
    partner = jnp.where(first_half, pltpu.roll(t, 96, 1), pltpu.roll(t, 32, 1))
    return t * cos + partner * sin


def _rope_rows(t, cos_t, sin_t):
    half = HEAD_DIM // 2
    x1, x2 = t[0:half], t[half:HEAD_DIM]
    return jnp.concatenate([x1 * cos_t - x2 * sin_t, x2 * cos_t + x1 * sin_t], axis=0)


def _even_in_kernel(*refs, tm, decode, past_len):
    if decode:
        (x_ref, g_ref, w_ref, wt_ref, wiwt_ref, cos_ref, sin_ref, cost_ref, sint_ref, wpg_ref, psc_ref, pre_ref,
         q_ref, iq_ref, ik2_ref, kb_ref, kt_ref, vt_ref, vtb_ref, ikt_ref, iwt_ref, pool_ref, xp_ref,
         ext_s) = refs
    else:
        (x_ref, g_ref, w_ref, wt_ref, wiwt_ref, cos_ref, sin_ref, cost_ref, sint_ref, wpg_ref, psc_ref,
         q_ref, iq_ref, ik2_ref, kb_ref, kt_ref, vt_ref, vtb_ref, ikt_ref, iwt_ref, pool_ref, xp_ref,
         ext_s) = refs
    j = pl.program_id(1)
    xn = _rms(x_ref[...], g_ref[...]).astype(BF16)
    cos = cos_ref[...]
    sin = sin_ref[...]
    lane = lax.broadcasted_iota(I32, (1, LANES), 1)
    first_half = (lane % HEAD_DIM) < (HEAD_DIM // 2)

    def proj_rope(c0, c1):
        t = _dot(xn, w_ref[:, c0:c1])
        return [_rope_tile(t[:, c:c + LANES], cos, sin, first_half) for c in range(0, c1 - c0, LANES)]

    for half in range(2):
        tiles = proj_rope(C_Q + 256 * half, C_Q + 256 * (half + 1))
        for c, t in enumerate(tiles):
            col = 256 * half + LANES * c
            q_ref[:, col:col + LANES] = (t * (HEAD_DIM ** -0.5 * LOG2E)).astype(BF16)
    for half in range(2):
        tiles = proj_rope(C_IQ + 256 * half, C_IQ + 256 * (half + 1))
        for c, t in enumerate(tiles):
            col = 256 * half + LANES * c
            iq_ref[:, col:col + LANES] = t.astype(BF16)
    tiles = proj_rope(C_IK2, C_K)
    ik2_ref[:, 0:LANES] = tiles[0].astype(BF16)
    ik2_ref[:, LANES:2 * LANES] = tiles[1].astype(BF16)
    kb_ref[...] = proj_rope(C_K, C_XP)[0].astype(BF16)

    cos_t = cost_ref[...]
    sin_t = sint_ref[...]
    ft = _dot_nt(wt_ref[...], xn)
    kt = jnp.concatenate([_rope_rows(ft[R_K + h * HEAD_DIM:R_K + (h + 1) * HEAD_DIM], cos_t, sin_t)
                          for h in range(N_KV_HEADS)], axis=0)
    vt = ft[R_V:R_IK]
    kt_ref[0] = kt
    vt_ref[0] = vt
    pad = jnp.concatenate([jnp.ones((1, tm), F32), jnp.zeros((V_ROWS - HEAD_DIM - 1, tm), F32)], axis=0)
    for h in range(N_KV_HEADS):
        vtb_ref[0, h * V_ROWS:h * V_ROWS + HEAD_DIM, :] = vt[h * HEAD_DIM:(h + 1) * HEAD_DIM].astype(BF16)
        vtb_ref[0, h * V_ROWS + HEAD_DIM:(h + 1) * V_ROWS, :] = pad.astype(BF16)
    ikt_ref[0] = _rope_rows(ft[R_IK:R_END], cos_t, sin_t)
    iwt_ref[0] = _dot_nt(wiwt_ref[...], xn) * (N_IDX_HEADS ** -0.5) * (D_IDX ** -0.5)

    xp = _dot(xn, w_ref[:, C_XP:C_END])
    row = lax.broadcasted_iota(I32, (tm, 1), 0)
    if decode:
        ns = tm // SUBLANES
        ext_s[:, 0:16, :] = pre_ref[...]
        ext_s[:, 16:24, :] = xp.reshape(ns, SUBLANES, 4 * D_POOL_GROUP)
        pos = past_len + (row % SUBLANES)
    else:
        @pl.when(j == 0)
        def _():
            ext_s[0:16, :] = jnp.zeros((16, 4 * D_POOL_GROUP), F32)
        ext_s[16:16 + tm, :] = xp
        pos = j * tm + row
    for g, w in enumerate(POOL_WINDOWS):
        cs = slice(g * D_POOL_GROUP, (g + 1) * D_POOL_GROUP)
        tok = xp[:, cs]
        acc = tok
        for i in range(1, w):
            if decode:
                acc = acc + ext_s[:, 16 - i:24 - i, cs].reshape(tm, D_POOL_GROUP)
            else:
                acc = acc + ext_s[16 - i:16 - i + tm, cs]
        cnt = jnp.minimum(w, pos + 1).astype(F32)
        d = acc / cnt - tok
        y = _dot(d.astype(BF16), wpg_ref[g]) * psc_ref[:, cs]
        pool_ref[:, cs] = y.astype(BF16)
    if decode:
        xp_ref[...] = xp
    else:
        ext_s[0:16, :] = xp[tm - 16:tm, :]
        xp_ref[0] = xp[tm - 16:tm, :]


def _rope_tables(pos):
    half = HEAD_DIM // 2
    inv = ROPE_THETA ** (-np.arange(half, dtype=np.float64) / half)
    ang = pos.astype(np.float64)[:, None] * inv[None, :]
    cos32, sin32 = np.cos(ang).astype(np.float32), np.sin(ang).astype(np.float32)
    cos = np.tile(cos32, (1, LANES // half))
    sin = np.tile(np.concatenate([-sin32, sin32], axis=1), (1, LANES // HEAD_DIM))
    return tuple(jnp.asarray(a) for a in (cos, sin, np.ascontiguousarray(cos32.T), np.ascontiguousarray(sin32.T)))


def _arrange_w_in_even(w):
    q = w[:, 0:512].reshape(-1, N_HEADS, HEAD_DIM)
    q = jnp.stack([q[:, 0:4], q[:, 4:8]], axis=2).reshape(-1, 512)
    k = w[:, 512:640]
    v = w[:, 640:768]
    iq = w[:, 768:1280]
    ik = w[:, 1280:1344]
    iw = w[:, 1344:1352]
    xp = w[:, 1352:1864]
    z = jnp.zeros_like(ik)
    main = jnp.concatenate([q, iq, ik, z, z, ik, k, xp], axis=1).astype(BF16)
    feat = jnp.concatenate([k, v, ik], axis=1).T.astype(BF16)
    return main, feat, iw.T.astype(BF16)


def _even_in(x, g, wmain, wfeat, wiwt, tabs, wpg, psc, prefix, *, nb, tm, past_len):
    n, d = x.shape
    t = n // nb
    nt = t // tm
    decode = prefix is not None
    cos, sin, cos_t, sin_t = tabs
    row = lambda b, j: (b * nt + j, 0)
    tab_row = (lambda b, j: (0, 0)) if decode else (lambda b, j: (j, 0))
    tab_col = (lambda b, j: (0, 0)) if decode else (lambda b, j: (0, j))
    const = lambda b, j: (0, 0)
    feat = lambda b, j: (b, 0, j)
    in_specs = [
        pl.BlockSpec((tm, d), row),
        pl.BlockSpec((1, d), const),
        pl.BlockSpec(wmain.shape, const),
        pl.BlockSpec(wfeat.shape, const),
        pl.BlockSpec(wiwt.shape, const),
        pl.BlockSpec((tm, LANES), tab_row),
        pl.BlockSpec((tm, LANES), tab_row),
        pl.BlockSpec((HEAD_DIM // 2, tm), tab_col),
        pl.BlockSpec((HEAD_DIM // 2, tm), tab_col),
        pl.BlockSpec(wpg.shape, lambda b, j: (0, 0, 0)),
        pl.BlockSpec((1, 512), const),
    ]
    args = [x, g, wmain, wfeat, wiwt, cos, sin, cos_t, sin_t, wpg, psc]
    if decode:
        ns = tm // SUBLANES
        in_specs.append(pl.BlockSpec((ns, 16, 512), lambda b, j: (b * nt + j, 0, 0)))
        args.append(prefix)
        xp_shape = jax.ShapeDtypeStruct((n, 512), F32)
        xp_spec = pl.BlockSpec((tm, 512), row)
        scratch = [pltpu.VMEM((ns, 24, 512), F32)]
    else:
        xp_shape = jax.ShapeDtypeStruct((nb, 16, 512), F32)
        xp_spec = pl.BlockSpec((1, 16, 512), lambda b, j: (b, 0, 0))
        scratch = [pltpu.VMEM((tm + 16, 512), F32)]
    out_shape = [
        jax.ShapeDtypeStruct((n, 512), BF16),
        jax.ShapeDtypeStruct((n, 512), BF16),
        jax.ShapeDtypeStruct((n, 256), BF16),
        jax.ShapeDtypeStruct((n, 128), BF16),
        jax.ShapeDtypeStruct((nb, 128, t), F32),
        jax.ShapeDtypeStruct((nb, 128, t), F32),
        jax.ShapeDtypeStruct((nb, N_KV_HEADS * V_ROWS, t), BF16),
        jax.ShapeDtypeStruct((nb, D_IDX, t), F32),
        jax.ShapeDtypeStruct((nb, 8, t), F32),
        jax.ShapeDtypeStruct((n, 512), BF16),
        xp_shape,
    ]
    out_specs = [
        pl.BlockSpec((tm, 512), row),
        pl.BlockSpec((tm, 512), row),
        pl.BlockSpec((tm, 256), row),
        pl.BlockSpec((tm, 128), row),
        pl.BlockSpec((1, 128, tm), feat),
        pl.BlockSpec((1, 128, tm), feat),
        pl.BlockSpec((1, N_KV_HEADS * V_ROWS, tm), feat),
        pl.BlockSpec((1, D_IDX, tm), feat),
        pl.BlockSpec((1, 8, tm), feat),
        pl.BlockSpec((tm, 512), row),
        xp_spec,
    ]
    return pl.pallas_call(
        functools.partial(_even_in_kernel, tm=tm, decode=decode, past_len=past_len),
        grid=(nb, nt), in_specs=in_specs, out_specs=out_specs, out_shape=out_shape,
        scratch_shapes=scratch, compiler_params=_cparams(("arbitrary", "arbitrary")),
        name="even_in_decode" if decode else "even_in_prompt",
    )(*args)


F32_MIN_NORMAL = 2.0 ** -126
F32_LOWEST = -3.4028234663852886e38
LOWEST_FINITE_CODE = -2139095040


def _score_for_search(s, valid):
    return jnp.where(valid, jnp.where(jnp.abs(s) < F32_MIN_NORMAL, 0.0, s), -jnp.inf)


def _code_to_float(code):
    b = code ^ ((code >> 31) & 0x7FFFFFFF)
    f = lax.bitcast_convert_type(b, F32)
    tiny = ((b >> 23) & 0xFF) == 0
    return jnp.where(tiny, jnp.where(b > 0, F32_MIN_NORMAL, 0.0), f)


FRACTION_BITS = 24
MAX_TIE_WALK = 4


def _topk_search(count, shape, n_index_bits, topk, first_index=None):
    def bit_step(i, carry):
        code, c_ge = carry
        cand = code + jnp.left_shift(jnp.int32(1), 31 - i)
        cf = _code_to_float(cand)
        c = count(lambda sv, idx: sv >= cf)
        ok = c >= topk
        return jnp.where(ok, cand, code), jnp.where(ok, c, c_ge)

    code, c_ge = lax.fori_loop(0, 32, bit_step, (jnp.full(shape, INT_MIN, I32), jnp.zeros(shape, I32)))
    short = code < LOWEST_FINITE_CODE
    thr = jnp.where(short, -jnp.inf, _code_to_float(code))
    nxt = jnp.where(short, F32_LOWEST, _code_to_float(code + 1))
    c_above = count(lambda sv, idx: sv >= nxt)
    need = topk - c_above
    ambiguous = jnp.logical_and(c_ge - c_above > need, jnp.logical_not(short))
    width = nxt - thr
    inv_width = jnp.where(jnp.logical_and(width >= F32_MIN_NORMAL, width < jnp.inf), 1.0 / width, 0.0)

    def frac(sv):
        return (sv - thr) * inv_width

    def in_band(sv):
        return jnp.logical_and(sv >= thr, sv < nxt)

    def split_band():
        c_pos = count(lambda sv, idx: jnp.logical_and(in_band(sv), frac(sv) > 0.0))

        def bisect_fraction():
            def frac_step(i, m):
                cand = m + jnp.left_shift(jnp.int32(1), FRACTION_BITS - 1 - i)
                cf = cand.astype(F32) * (2.0 ** -FRACTION_BITS)
                c = count(lambda sv, idx: jnp.logical_and(in_band(sv), frac(sv) >= cf))
                return jnp.where(c >= need, cand, m)
            m = lax.fori_loop(0, FRACTION_BITS, frac_step, jnp.zeros(shape, I32))
            r = m.astype(F32) * (2.0 ** -FRACTION_BITS)
            return r, count(lambda sv, idx: jnp.logical_and(in_band(sv), frac(sv) > r))

        any_frac = jnp.max(jnp.where(ambiguous, c_pos, 0)) > 0
        rstar, c_gtr = lax.cond(any_frac, bisect_fraction, lambda: (jnp.zeros(shape, F32), c_pos))
        need_eq = need - c_gtr

        def index_cut(tied):
            def idx_step(i, lo):
                cand = lo + jnp.left_shift(jnp.int32(1), n_index_bits - 1 - i)
                c = count(lambda sv, idx: jnp.logical_and(tied(sv), idx < cand))
                return jnp.where(c < need_eq, cand, lo)
            return lax.fori_loop(0, n_index_bits, idx_step, jnp.zeros(shape, I32))

        def plain_cut():
            tied = lambda sv: sv == thr
            if first_index is None:
                return index_cut(tied)
            most = jnp.max(jnp.where(ambiguous, need_eq, 1))

            def walk():
                def step(j, cut):
                    nxt_cut = first_index(lambda sv, idx: jnp.logical_and(tied(sv), idx > cut))
                    return jnp.where(j < need_eq, nxt_cut, cut)
                return lax.fori_loop(0, most, step, jnp.full(shape, -1, I32))

            return lax.cond(most <= MAX_TIE_WALK, walk, lambda: index_cut(tied))

        cut = lax.cond(any_frac,
                       lambda: index_cut(lambda sv: jnp.logical_and(in_band(sv), frac(sv) == rstar)),
                       plain_cut)
        return jnp.where(ambiguous, rstar, -1.0), jnp.where(ambiguous, cut, take_all), any_frac

    take_all = jnp.where(short, -1, 2 ** 30).astype(I32)
    any_amb = jnp.max(ambiguous.astype(I32)) > 0
    rstar, cut, any_frac = lax.cond(any_amb, split_band,
                                    lambda: (jnp.full(shape, -1.0, F32), take_all, jnp.zeros((), jnp.bool_)))

    def select(sv, idx):
        r = frac(sv)
        taken = jnp.logical_or(r > rstar, jnp.logical_and(r == rstar, idx <= cut))
        return jnp.logical_or(sv >= nxt, jnp.logical_and(in_band(sv), taken))

    def select_no_fraction(sv, idx):
        return jnp.logical_or(sv > thr, jnp.logical_and(sv == thr, idx <= cut))

    return select, select_no_fraction, jnp.maximum(thr, F32_LOWEST), any_amb, any_frac


def _index_bits(n):
    return max(1, int(np.ceil(np.log2(n))))


def _fold_rows(x, op):
    return op(x.reshape(x.shape[0] // SUBLANES, SUBLANES, x.shape[1]), axis=0)


def _dsa_prompt_kernel(q_ref, iq_ref, iwt_ref, kb_ref, vt_ref, ik2_ref, o_ref,
                       keys_s, qm_s, iqs_s, acc_s, *, tk, topk):
    i = pl.program_id(1)
    n_tiles = (i * Q_BLOCK + Q_BLOCK + tk - 1) // tk
    lane = lax.broadcasted_iota(I32, (1, LANES), 1)
    qpos = i * Q_BLOCK + lane
    lo_half = lane < HEAD_DIM
    nq = KV_GROUP * Q_BLOCK

    for jj in range(KV_GROUP):
        qt = q_ref[:, jj * LANES:(jj + 1) * LANES]
        zero = jnp.zeros_like(qt)
        qm_s[0, jj * Q_BLOCK:(jj + 1) * Q_BLOCK, :] = jnp.where(lo_half, qt, zero)
        qm_s[1, jj * Q_BLOCK:(jj + 1) * Q_BLOCK, :] = jnp.where(lo_half, zero, qt)
        iqs_s[jj * Q_BLOCK:(jj + 1) * Q_BLOCK, :] = iq_ref[:, jj * LANES:(jj + 1) * LANES]
    w = iwt_ref[0]

    def score_tile(t, carry):
        r0 = pl.multiple_of(t * tk, tk)
        ik2 = ik2_ref[pl.ds(r0, tk), :]
        iqs = iqs_s[...]
        rel_e = _dot_nt(ik2[:, 0:LANES], iqs)
        rel_o = _dot_nt(ik2[:, LANES:2 * LANES], iqs)
        s = jnp.zeros((tk, LANES), F32)
        for jj in range(4):
            cs = slice(jj * LANES, (jj + 1) * LANES)
            s = s + w[2 * jj:2 * jj + 1, :] * jnp.maximum(rel_e[:, cs], 0.0)
            s = s + w[2 * jj + 1:2 * jj + 2, :] * jnp.maximum(rel_o[:, cs], 0.0)
        kpos = r0 + lax.broadcasted_iota(I32, (tk, 1), 0)
        keys_s[pl.ds(r0, tk), :] = _score_for_search(s, kpos <= qpos)
        return carry

    lax.fori_loop(0, n_tiles, score_tile, 0)

    def count(pred):
        def body(t, acc):
            r0 = pl.multiple_of(t * tk, tk)
            sv = keys_s[pl.ds(r0, tk), :]
            idx = r0 + lax.broadcasted_iota(I32, (tk, 1), 0)
            return acc + _fold_rows(pred(sv, idx).astype(I32), jnp.sum)
        acc = lax.fori_loop(0, n_tiles, body, jnp.zeros((SUBLANES, LANES), I32))
        return acc.sum(axis=0, keepdims=True)

    def first_index(pred):
        far = jnp.int32(2 ** 30)
        def body(t, acc):
            r0 = pl.multiple_of(t * tk, tk)
            idx = r0 + lax.broadcasted_iota(I32, (tk, 1), 0)
            return jnp.minimum(acc, _fold_rows(jnp.where(pred(keys_s[pl.ds(r0, tk), :], idx), idx, far), jnp.min))
        acc = lax.fori_loop(0, n_tiles, body, jnp.full((SUBLANES, LANES), far, I32))
        return acc.min(axis=0, keepdims=True)

    select, select_no_fraction, thr_ge, any_cut, any_frac = _topk_search(
        count, (1, LANES), _index_bits(keys_s.shape[0]), topk, first_index)

    def attend(selector):
        def pv_tile(t, ms):
            r0 = pl.multiple_of(t * tk, tk)
            sv = keys_s[pl.ds(r0, tk), :]
            if selector is None:
                sel = sv >= thr_ge
            else:
                sel = selector(sv, r0 + lax.broadcasted_iota(I32, (tk, 1), 0))
            sel = jnp.concatenate([sel] * KV_GROUP, axis=1)
            kt = kb_ref[pl.ds(r0, tk), :]
            sts = [jnp.where(sel, _dot_nt(kt, qm_s[g]), NEG_BIG) for g in range(N_KV_HEADS)]
            ms_new = []
            for g, st in enumerate(sts):
                m_new = jnp.maximum(ms[g], jnp.max(_fold_rows(st, jnp.max), axis=0, keepdims=True))
                alpha = jnp.exp2(ms[g] - m_new)
                p = jnp.exp2(st - m_new)
                ms_new.append(m_new)
                vt = vt_ref[0, g * V_ROWS:(g + 1) * V_ROWS, pl.ds(r0, tk)]
                acc_s[g] = acc_s[g] * alpha + _dot(vt, p.astype(BF16))
            return tuple(ms_new)

        lax.fori_loop(0, n_tiles, pv_tile, tuple(jnp.full((1, nq), NEG_BIG, F32) for _ in range(N_KV_HEADS)))

    acc_s[...] = jnp.zeros(acc_s.shape, F32)
    lax.cond(any_cut,
             lambda: lax.cond(any_frac, lambda: attend(select), lambda: attend(select_no_fraction)),
             lambda: attend(None))
    heads = []
    for g in range(N_KV_HEADS):
        og = acc_s[g, 0:HEAD_DIM, :] / acc_s[g, HEAD_DIM:HEAD_DIM + 1, :]
        heads += [og[:, jj * Q_BLOCK:(jj + 1) * Q_BLOCK] for jj in range(KV_GROUP)]
    o_ref[...] = jnp.concatenate(heads, axis=0).T.astype(o_ref.dtype)


def _store_heads(o_ref, heads, lo_half):
    for jj in range(N_HEADS // 2):
        a, b = heads[2 * jj], heads[2 * jj + 1]
        if 2 * jj < KV_GROUP:
            tile = jnp.where(lo_half, a, pltpu.roll(b, HEAD_DIM, 1))
        else:
            tile = jnp.where(lo_half, pltpu.roll(a, HEAD_DIM, 1), b)
        o_ref[:, jj * LANES:(jj + 1) * LANES] = tile.astype(o_ref.dtype)


def _dsa_prompt(q, iq, iwt, kb, vtb, ik2, *, nb, tk, topk):
    n = q.shape[0]
    t = n // nb
    nq = t // Q_BLOCK
    assert t % tk == 0
    blk = lambda b, i: (b * nq + i, 0)
    seq = lambda b, i: (b, 0)
    feat = lambda b, i: (b, 0, 0)
    return pl.pallas_call(
        functools.partial(_dsa_prompt_kernel, tk=tk, topk=topk),
        grid=(nb, nq),
        in_specs=[
            pl.BlockSpec((Q_BLOCK, 512), blk),
            pl.BlockSpec((Q_BLOCK, 512), blk),
            pl.BlockSpec((1, 8, Q_BLOCK), lambda b, i: (b, 0, i)),
            pl.BlockSpec((t, 128), seq),
            pl.BlockSpec((1, N_KV_HEADS * V_ROWS, t), feat),
            pl.BlockSpec((t, 256), seq),
        ],
        out_specs=pl.BlockSpec((Q_BLOCK, 512), blk),
        out_shape=jax.ShapeDtypeStruct((n, 512), BF16),
        scratch_shapes=[
            pltpu.VMEM((t, LANES), F32),
            pltpu.VMEM((N_KV_HEADS, KV_GROUP * Q_BLOCK, LANES), BF16),
            pltpu.VMEM((4 * Q_BLOCK, LANES), BF16),
            pltpu.VMEM((N_KV_HEADS, V_ROWS, KV_GROUP * Q_BLOCK), F32),
        ],
        compiler_params=_cparams(("arbitrary", "arbitrary")),
        name="dsa_prompt",
    )(q, iq, iwt, kb, vtb, ik2)


SEQ_GROUP = Q_BLOCK // SUBLANES


def _pad_rows(a):
    return jnp.concatenate([a, jnp.zeros((LANES - a.shape[0], a.shape[1]), a.dtype)], axis=0).astype(BF16)


def _decode_select_kernel(*refs, n_pages, topk, nnew):
    ip = refs[1:1 + n_pages]
    iqd_ref, wcol_ref, inew_ref, bias_ref, keys_s = refs[1 + n_pages:]
    t = pl.program_id(1)
    n_keys = keys_s.shape[1]
    past = n_pages * PAGE_SIZE
    n_tiles = n_keys // LANES
    lane = lax.broadcasted_iota(I32, (1, LANES), 1)
    qrow = lax.broadcasted_iota(I32, (nnew, 1), 0)
    new_valid = jnp.logical_and(lane <= qrow, lane < nnew)

    def head_sum(x):
        return x.reshape(N_IDX_HEADS, nnew, x.shape[1]).sum(axis=0)

    r0 = pl.multiple_of(t * nnew, nnew)
    iqd = iqd_ref[0]
    wcol = wcol_ref[0]
    for p in range(n_pages):
        rel = _dot(iqd, ip[p][0].astype(BF16))
        s = head_sum(wcol * jnp.maximum(rel, 0.0))
        keys_s[pl.ds(r0, nnew), p * LANES:(p + 1) * LANES] = _score_for_search(s, True)
    rel = _dot_nt(iqd, _pad_rows(inew_ref[0]))
    s = head_sum(wcol * jnp.maximum(rel, 0.0))
    keys_s[pl.ds(r0, nnew), past:past + LANES] = _score_for_search(s, new_valid)

    @pl.when(t == SEQ_GROUP - 1)
    def _search():
        def count(pred):
            acc = jnp.zeros((Q_BLOCK, LANES), I32)
            for c in range(n_tiles):
                idx = c * LANES + lane
                acc = acc + pred(keys_s[:, c * LANES:(c + 1) * LANES], idx).astype(I32)
            return acc.sum(axis=1, keepdims=True)
        select, _, _, _, _ = _topk_search(count, (Q_BLOCK, 1), _index_bits(n_keys), topk)
        for c in range(n_tiles):
            sel = select(keys_s[:, c * LANES:(c + 1) * LANES], c * LANES + lane)
            bias_ref[:, c * LANES:(c + 1) * LANES] = jnp.where(sel, 0.0, NEG_BIG)


def _decode_attend_kernel(*refs, n_pages, nnew):
    kp = refs[1:1 + n_pages]
    vp = refs[1 + n_pages:1 + 2 * n_pages]
    qd_ref, knew_ref, vnew_ref, bias_ref, o_ref = refs[1 + 2 * n_pages:]
    past = n_pages * PAGE_SIZE
    lane = lax.broadcasted_iota(I32, (1, LANES), 1)
    bias = jnp.concatenate([bias_ref[...]] * N_HEADS, axis=0)
    qd = qd_ref[0]
    st = jnp.concatenate([_dot(qd, kp[p][0].astype(BF16)) for p in range(n_pages)]
                         + [_dot_nt(qd, _pad_rows(knew_ref[0]))], axis=1) + bias
    m = jnp.max(st, axis=1, keepdims=True)
    pr = jnp.exp2(st - m)
    l = jnp.sum(pr, axis=1, keepdims=True)
    pb = pr.astype(BF16)
    o = _dot(pb[:, past:past + LANES], _pad_rows(vnew_ref[0]))
    for p in range(n_pages):
        o = o + _dot_nt(pb[:, p * LANES:(p + 1) * LANES], vp[p][0].astype(BF16))
    o = o / l
    _store_heads(o_ref, [o[h * nnew:(h + 1) * nnew] for h in range(N_HEADS)], lane < HEAD_DIM)


def _dsa_decode(page_table, ci_t, ck_t, cv_t, iqd, wcol, inew, qd, knew, vnew, *, topk):
    ns, n_pages = page_table.shape
    nnew = knew.shape[1]
    assert ns % SEQ_GROUP == 0 and nnew == SUBLANES
    n_keys = (n_pages + 1) * PAGE_SIZE

    sel_page = lambda p: pl.BlockSpec((1, D_IDX, PAGE_SIZE),
                                      lambda g, t, pt, p=p: (pt[g * SEQ_GROUP + t, p], 0, 0))
    sel3 = lambda g, t, pt: (g * SEQ_GROUP + t, 0, 0)
    bias = pl.pallas_call(
        functools.partial(_decode_select_kernel, n_pages=n_pages, topk=topk, nnew=nnew),
        grid_spec=pltpu.PrefetchScalarGridSpec(
            num_scalar_prefetch=1, grid=(ns // SEQ_GROUP, SEQ_GROUP),
            in_specs=([sel_page(p) for p in range(n_pages)]
                      + [pl.BlockSpec((1, N_IDX_HEADS * nnew, D_IDX), sel3),
                         pl.BlockSpec((1, N_IDX_HEADS * nnew, 1), sel3),
                         pl.BlockSpec((1, nnew, D_IDX), sel3)]),
            out_specs=pl.BlockSpec((Q_BLOCK, n_keys), lambda g, t, pt: (g, 0)),
            scratch_shapes=[pltpu.VMEM((Q_BLOCK, n_keys), F32)]),
        out_shape=jax.ShapeDtypeStruct((ns * nnew, n_keys), F32),
        compiler_params=_cparams(("arbitrary", "arbitrary")),
        name="dsa_decode_select",
    )(page_table, *([ci_t] * n_pages), iqd, wcol, inew)

    att_page = lambda p: pl.BlockSpec((1, D_KV, PAGE_SIZE), lambda s, pt, p=p: (pt[s, p], 0, 0))
    att3 = lambda s, pt: (s, 0, 0)
    return pl.pallas_call(
        functools.partial(_decode_attend_kernel, n_pages=n_pages, nnew=nnew),
        grid_spec=pltpu.PrefetchScalarGridSpec(
            num_scalar_prefetch=1, grid=(ns,),
            in_specs=([att_page(p) for p in range(n_pages)] + [att_page(p) for p in range(n_pages)]
                      + [pl.BlockSpec((1, N_HEADS * nnew, D_KV), att3),
                         pl.BlockSpec((1, nnew, D_KV), att3),
                         pl.BlockSpec((1, nnew, D_KV), att3),
                         pl.BlockSpec((nnew, n_keys), lambda s, pt: (s, 0))]),
            out_specs=pl.BlockSpec((nnew, D_ATTN), lambda s, pt: (s, 0))),
        out_shape=jax.ShapeDtypeStruct((ns * nnew, D_ATTN), F32),
        compiler_params=_cparams(("arbitrary",)),
        name="dsa_decode_attend",
    )(page_table, *([ck_t] * n_pages), *([cv_t] * n_pages), qd, knew, vnew, bias)


FF_CHUNK = 256


def _ffn_block(h, gpre, gpost, wgu_ref, wd_ref, a_s):
    d_ff = wd_ref.shape[0]
    hn = _rms(h, gpre).astype(BF16)
    for c in range(0, d_ff, FF_CHUNK):
        gate = _dot(hn, wgu_ref[:, c:c + FF_CHUNK])
        up = _dot(hn, wgu_ref[:, d_ff + c:d_ff + c + FF_CHUNK])
        a_s[:, c:c + FF_CHUNK] = (gate * jax.nn.sigmoid(gate) * up).astype(BF16)
    return h + _rms(_dot(a_s[...], wd_ref[...]), gpost)


def _even_out_kernel(x_ref, attn_ref, pool_ref, wo_ref, gmix_ref, gpre_ref, gpost_ref, wgu_ref, wd_ref,
                     o_ref, a_s):
    mix = _dot(attn_ref[...], wo_ref[0:D_ATTN, :]) + _dot(pool_ref[...], wo_ref[D_ATTN:, :])
    h = x_ref[...] + _rms(mix, gmix_ref[...])
    o_ref[...] = _ffn_block(h, gpre_ref[...], gpost_ref[...], wgu_ref, wd_ref, a_s)


def _even_out(x, attn, pool, wo, gmix, gpre, gpost, wgu, wd, *, tm):
    n, d = x.shape
    row = lambda i: (i, 0)
    const = lambda i: (0, 0)
    return pl.pallas_call(
        _even_out_kernel, grid=(n // tm,),
        in_specs=[pl.BlockSpec((tm, d), row), pl.BlockSpec((tm, 512), row), pl.BlockSpec((tm, 512), row),
                  pl.BlockSpec(wo.shape, const), pl.BlockSpec((1, d), const), pl.BlockSpec((1, d), const),
                  pl.BlockSpec((1, d), const), pl.BlockSpec(wgu.shape, const), pl.BlockSpec(wd.shape, const)],
        out_specs=pl.BlockSpec((tm, d), row),
        out_shape=jax.ShapeDtypeStruct((n, d), F32),
        scratch_shapes=[pltpu.VMEM((tm, wd.shape[0]), BF16)],
        compiler_params=_cparams(("arbitrary",)),
        name="even_out_ffn",
    )(x, attn, pool, wo, gmix, gpre, gpost, wgu, wd)


def _odd_kernel(x_ref, g_ref, win_ref, lng_ref, lnb_ref, ws_ref, bs_ref, wout_ref, gmix_ref,
                gpre_ref, gpost_ref, wgu_ref, wd_ref, o_ref, z_ref, y_s, a_s, *, tm, seq_len):
    d_sgu = wout_ref.shape[0]
    dh = d_sgu // N_SGU_HEADS
    x = x_ref[...]
    xn = _rms(x, g_ref[...]).astype(BF16)
    r = lax.broadcasted_iota(I32, (CHUNK, CHUNK), 0)
    c = lax.broadcasted_iota(I32, (CHUNK, CHUNK), 1)
    causal = jnp.logical_and(r // seq_len == c // seq_len, c <= r)
    for hd in range(N_SGU_HEADS):
        cs = slice(hd * dh, (hd + 1) * dh)
        u = _gelu(_dot(xn, win_ref[:, cs]))
        y_s[:, cs] = u.astype(BF16)
    vs = []
    for hd in range(N_SGU_HEADS):
        cs = slice(d_sgu + hd * dh, d_sgu + (hd + 1) * dh)
        vs.append(_gelu(_dot(xn, win_ref[:, cs])))
    v = jnp.concatenate(vs, axis=1)
    mu = jnp.mean(v, axis=-1, keepdims=True)
    vc = v - mu
    z = vc * lax.rsqrt(jnp.mean(vc * vc, axis=-1, keepdims=True) + EPS) * lng_ref[...] + lnb_ref[...]
    z_ref[...] = z
    zb = z.astype(BF16)
    for hd in range(N_SGU_HEADS):
        cs = slice(hd * dh, (hd + 1) * dh)
        wm = jnp.where(causal, ws_ref[hd], 0.0).astype(BF16)
        bias = bs_ref[hd]
        for ch in range(tm // CHUNK):
            rs = slice(ch * CHUNK, (ch + 1) * CHUNK)
            s = _dot(wm, zb[rs, cs]) + bias
            y_s[rs, cs] = (y_s[rs, cs].astype(F32) * s).astype(BF16)
    h = x + _rms(_dot(y_s[...], wout_ref[...]), gmix_ref[...])
    o_ref[...] = _ffn_block(h, gpre_ref[...], gpost_ref[...], wgu_ref, wd_ref, a_s)


def _odd(x, g, win, lng, lnb, ws, bs, wout, gmix, gpre, gpost, wgu, wd, *, tm, seq_len):
    n, d = x.shape
    d_sgu = wout.shape[0]
    row = lambda i: (i, 0)
    const = lambda i: (0, 0)
    const3 = lambda i: (0, 0, 0)
    return pl.pallas_call(
        functools.partial(_odd_kernel, tm=tm, seq_len=seq_len), grid=(n // tm,),
        in_specs=[pl.BlockSpec((tm, d), row), pl.BlockSpec((1, d), const), pl.BlockSpec(win.shape, const),
                  pl.BlockSpec((1, d_sgu), const), pl.BlockSpec((1, d_sgu), const),
                  pl.BlockSpec(ws.shape, const3), pl.BlockSpec(bs.shape, const3),
                  pl.BlockSpec(wout.shape, const), pl.BlockSpec((1, d), const), pl.BlockSpec((1, d), const),
                  pl.BlockSpec((1, d), const), pl.BlockSpec(wgu.shape, const), pl.BlockSpec(wd.shape, const)],
        out_specs=[pl.BlockSpec((tm, d), row), pl.BlockSpec((tm, d_sgu), row)],
        out_shape=[jax.ShapeDtypeStruct((n, d), F32), jax.ShapeDtypeStruct((n, d_sgu), F32)],
        scratch_shapes=[pltpu.VMEM((tm, d_sgu), BF16), pltpu.VMEM((tm, wd.shape[0]), BF16)],
        compiler_params=_cparams(("arbitrary",)),
        name="odd_mixer_ffn",
    )(x, g, win, lng, lnb, ws, bs, wout, gmix, gpre, gpost, wgu, wd)


PROMPT_ROW_TILE = 512
DECODE_ROW_TILE = 256
PROMPT_KEY_TILE = 512


def _tile_rows(n, want):
    tm = min(want, n)
    while n % tm:
        tm //= 2
    return tm


def _from_feature_major(a, heads):
    b, f, t = a.shape
    if heads is None:
        return jnp.transpose(a, (0, 2, 1))
    return jnp.transpose(a.reshape(b, heads, f // heads, t), (0, 3, 1, 2))


def kernel(x_prompt, x_sample, cache_k, cache_v, cache_idx_k, state_pool, page_table, norm_mix_pre, norm_mix_post, norm_ffn_pre, norm_ffn_post, w_in_even, w_out_even, w_pool_group, pool_scale, w_in_odd, sgu_norm_g, sgu_norm_b, w_spatial, b_spatial, w_out_odd, w_ffn_gate_up, w_ffn_down):
    bp, tp, d = x_prompt.shape
    bs, ts, _ = x_sample.shape
    n_pages = page_table.shape[1]
    past_len = n_pages * PAGE_SIZE
    topk_p = min(TOPK_MAX, tp // 4)
    topk_s = min(TOPK_MAX, (past_len + ts) // 4)
    depth = norm_mix_pre.shape[0]
    assert ts == SUBLANES and tp % Q_BLOCK == 0 and d % LANES == 0

    hp = x_prompt.reshape(bp * tp, d)
    hs = x_sample.reshape(bs * ts, d)
    tm_p = _tile_rows(bp * tp, PROMPT_ROW_TILE)
    tm_p_seq = _tile_rows(tp, PROMPT_ROW_TILE)
    tm_s = _tile_rows(bs * ts, DECODE_ROW_TILE)
    row = lambda a: a.reshape(1, -1)

    tabs_p = _rope_tables(np.arange(tp))
    tabs_s = _rope_tables(np.tile(past_len + np.arange(ts), tm_s // ts))

    outs_p, outs_s, sgu_s = [], [], []
    for layer in range(depth):
        li = layer // 2
        gpre, gpost = row(norm_ffn_pre[layer]), row(norm_ffn_post[layer])
        wgu = w_ffn_gate_up[layer].astype(BF16)
        wd = w_ffn_down[layer].astype(BF16)
        gmix_pre, gmix_post = row(norm_mix_pre[layer]), row(norm_mix_post[layer])
        if layer % 2 == 0:
            wmain, wfeat, wiwt = _arrange_w_in_even(w_in_even[li])
            wpg = w_pool_group[li].astype(BF16)
            psc = row(pool_scale[li])
            wo = w_out_even[li].astype(BF16)
            (q, iq, ik2, kb, kt, vt, vtb, ikt, iwt, pool, xp_tail) = _even_in(
                hp, gmix_pre, wmain, wfeat, wiwt, tabs_p, wpg, psc, None, nb=bp, tm=tm_p_seq, past_len=0)
            attn = _dsa_prompt(q, iq, iwt, kb, vtb, ik2, nb=bp, tk=min(PROMPT_KEY_TILE, tp), topk=topk_p)
            hp = _even_out(hp, attn, pool, wo, gmix_post, gpre, gpost, wgu, wd, tm=tm_p)
            outs_p.append((_from_feature_major(kt, N_KV_HEADS), _from_feature_major(vt, N_KV_HEADS),
                           _from_feature_major(ikt, None), xp_tail[:, 1:, :]))
            prefix = jnp.pad(state_pool[li], ((0, 0), (1, 0), (0, 0)))
            (q, iq, ik2, kb, kt, vt, vtb, ikt, iwt, pool, xp) = _even_in(
                hs, gmix_pre, wmain, wfeat, wiwt, tabs_s, wpg, psc, prefix, nb=1, tm=tm_s, past_len=past_len)
            del kb, vtb
            k_new = _from_feature_major(kt.reshape(1, D_KV, bs * ts), None).reshape(bs, ts, D_KV)
            v_new = _from_feature_major(vt.reshape(1, D_KV, bs * ts), None).reshape(bs, ts, D_KV)
            ik_new = _from_feature_major(ikt, None).reshape(bs, ts, D_IDX)
            q3 = q.reshape(bs, ts, 4, 2, HEAD_DIM)
            zq = jnp.zeros_like(q3[:, :, :, 0])
            qg = jnp.concatenate([jnp.concatenate([q3[:, :, :, 0], zq], axis=-1),
                                  jnp.concatenate([zq, q3[:, :, :, 1]], axis=-1)], axis=2)
            qd = jnp.transpose(qg, (0, 2, 1, 3)).reshape(bs, N_HEADS * ts, D_KV)
            iqd = jnp.transpose(iq.reshape(bs, ts, N_IDX_HEADS, D_IDX), (0, 2, 1, 3)).reshape(bs, N_IDX_HEADS * ts, D_IDX)
            wcol = jnp.transpose(iwt[0].reshape(N_IDX_HEADS, bs, ts), (1, 0, 2)).reshape(bs, N_IDX_HEADS * ts, 1)
            ci_t = jnp.transpose(cache_idx_k[li], (0, 2, 1))
            ck_t = jnp.transpose(cache_k[li], (0, 2, 3, 1)).reshape(-1, D_KV, PAGE_SIZE)
            cv_t = jnp.transpose(cache_v[li], (0, 2, 3, 1)).reshape(-1, D_KV, PAGE_SIZE)
            attn_s = _dsa_decode(page_table, ci_t, ck_t, cv_t, iqd, wcol, ik_new, qd, k_new, v_new, topk=topk_s)
            hs = _even_out(hs, attn_s.astype(BF16), pool, wo, gmix_post, gpre, gpost, wgu, wd, tm=tm_s)
            new_pool_s = jnp.concatenate([state_pool[li][:, ts:, :], xp.reshape(bs, ts, -1)], axis=1)
            outs_s.append((k_new.reshape(bs, ts, N_KV_HEADS, HEAD_DIM), v_new.reshape(bs, ts, N_KV_HEADS, HEAD_DIM),
                           ik_new, new_pool_s))
        else:
            win = w_in_odd[li].astype(BF16)
            wout = w_out_odd[li].astype(BF16)
            lng, lnb = row(sgu_norm_g[li]), row(sgu_norm_b[li])
            ws_p = w_spatial[li][:, :CHUNK, :CHUNK]
            bs_p = b_spatial[li][:, :CHUNK, None]
            hp, _ = _odd(hp, gmix_pre, win, lng, lnb, ws_p, bs_p, wout, gmix_post, gpre, gpost, wgu, wd,
                         tm=tm_p, seq_len=CHUNK)
            reps = CHUNK // ts
            ws_s = jnp.tile(w_spatial[li][:, :ts, :ts], (1, reps, reps))
            bs_s = jnp.tile(b_spatial[li][:, :ts], (1, reps))[:, :, None]
            hs, zs = _odd(hs, gmix_pre, win, lng, lnb, ws_s, bs_s, wout, gmix_post, gpre, gpost, wgu, wd,
                          tm=tm_s, seq_len=ts)
            sgu_s.append(zs.reshape(bs, ts, -1))

    return (hp.reshape(bp, tp, d), hs.reshape(bs, ts, d),
            jnp.stack([o[0] for o in outs_p]), jnp.stack([o[1] for o in outs_p]),
            jnp.stack([o[2] for o in outs_p]), jnp.stack([o[3] for o in outs_p]),
            jnp.stack([o[0] for o in outs_s]), jnp.stack([o[1] for o in outs_s]),
            jnp.stack([o[2] for o in outs_s]), jnp.stack([o[3] for o in outs_s]),
            jnp.stack(sgu_s))
```

```python
import functools

import jax
import jax.numpy as jnp
import numpy as np
from jax import lax
from jax.experimental import pallas as pl
from jax.experimental.pallas import tpu as pltpu

EPS = 1e-6
N_HEADS = 8
HEAD_DIM = 64
N_KV_HEADS = 2
KV_GROUP = N_HEADS // N_KV_HEADS
D_ATTN = N_HEADS * HEAD_DIM
D_KV = N_KV_HEADS * HEAD_DIM
N_IDX_HEADS = 8
D_IDX = 64
TOPK_MAX = 256
Q_BLOCK = 128
ROPE_THETA = 10000.0
POOL_WINDOWS = (2, 4, 8, 16)
POOL_BUF = 15
PAGE_SIZE = 128
CHUNK = 128
N_SGU_HEADS = 8
D_POOL_GROUP = 128
V_ROWS = HEAD_DIM + 16

LANES = 128
SUBLANES = 8
VMEM_LIMIT = 56 * 1024 * 1024
INT_MIN = -2 ** 31
NEG_BIG = -1e30
LOG2E = 1.4426950408889634

F32 = jnp.float32
BF16 = jnp.bfloat16
I32 = jnp.int32

_NT = (((1,), (1,)), ((), ()))


def _cparams(sem):
    return pltpu.CompilerParams(dimension_semantics=sem, vmem_limit_bytes=VMEM_LIMIT)


def _rms(x, g):
    return x * lax.rsqrt(jnp.mean(x * x, axis=-1, keepdims=True) + EPS) * g


def _dot(a, b):
    return jnp.dot(a, b, preferred_element_type=F32)


def _dot_nt(a, b):
    return lax.dot_general(a, b, _NT, preferred_element_type=F32)


def _gelu(x):
    return 0.5 * x * (1.0 + lax.erf(x * (2.0 ** -0.5)))


C_Q, C_IQ, C_IK2, C_K, C_V, C_XP, C_END = 0, 512, 1024, 1280, 1408, 1536, 2048


def _rope_tile(t, cos, sin, first_half):
    partner = jnp.where(first_half, pltpu.roll(t, 96, 1), pltpu.roll(t, 32, 1))
    return t * cos + partner * sin


def _even_in_kernel(*refs, tm, decode, past_len):
    if decode:
        (x_ref, g_ref, w_ref, wiwt_ref, cos_ref, sin_ref, wpg_ref, psc_ref, pre_ref,
         q_ref, iq_ref, ik2_ref, kb_ref, kt_ref, vt_ref, vtb_ref, ikt_ref, iwt_ref, pool_ref, xp_ref,
         ext_s, vtok_s) = refs
    else:
        (x_ref, g_ref, w_ref, wiwt_ref, cos_ref, sin_ref, wpg_ref, psc_ref,
         q_ref, iq_ref, ik2_ref, kb_ref, kt_ref, vt_ref, vtb_ref, ikt_ref, iwt_ref, pool_ref, xp_ref,
         ext_s, vtok_s) = refs
    j = pl.program_id(1)
    xn = _rms(x_ref[...], g_ref[...]).astype(BF16)
    cos = cos_ref[...]
    sin = sin_ref[...]
    lane = lax.broadcasted_iota(I32, (1, LANES), 1)
    first_half = (lane % HEAD_DIM) < (HEAD_DIM // 2)

    def proj_rope(c0, c1):
        t = _dot(xn, w_ref[:, c0:c1])
        return [_rope_tile(t[:, c:c + LANES], cos, sin, first_half) for c in range(0, c1 - c0, LANES)]

    for half in range(2):
        tiles = proj_rope(C_Q + 256 * half, C_Q + 256 * (half + 1))
        for c, t in enumerate(tiles):
            col = 256 * half + LANES * c
            q_ref[:, col:col + LANES] = (t * (HEAD_DIM ** -0.5 * LOG2E)).astype(BF16)
    for half in range(2):
        tiles = proj_rope(C_IQ + 256 * half, C_IQ + 256 * (half + 1))
        for c, t in enumerate(tiles):
            col = 256 * half + LANES * c
            iq_ref[:, col:col + LANES] = t.astype(BF16)
    tiles = proj_rope(C_IK2, C_K)
    ik2_ref[:, 0:LANES] = tiles[0].astype(BF16)
    ik2_ref[:, LANES:2 * LANES] = tiles[1].astype(BF16)
    ikt_ref[0] = tiles[0].T[0:D_IDX]
    k_tile = proj_rope(C_K, C_V)[0]
    kb_ref[...] = k_tile.astype(BF16)
    kt_ref[0] = k_tile.T
    vtok_s[...] = _dot(xn, w_ref[:, C_V:C_XP])
    vt = vtok_s[...].T
    vt_ref[0] = vt
    pad = jnp.concatenate([jnp.ones((1, tm), F32), jnp.zeros((V_ROWS - HEAD_DIM - 1, tm), F32)], axis=0)
    for h in range(N_KV_HEADS):
        vtb_ref[0, h * V_ROWS:h * V_ROWS + HEAD_DIM, :] = vt[h * HEAD_DIM:(h + 1) * HEAD_DIM].astype(BF16)
        vtb_ref[0, h * V_ROWS + HEAD_DIM:(h + 1) * V_ROWS, :] = pad.astype(BF16)
    iwt_ref[0] = _dot_nt(wiwt_ref[...], xn) * (N_IDX_HEADS ** -0.5) * (D_IDX ** -0.5)

    xp = _dot(xn, w_ref[:, C_XP:C_END])
    row = lax.broadcasted_iota(I32, (tm, 1), 0)
    if decode:
        ns = tm // SUBLANES
        ext_s[:, 0:16, :] = pre_ref[...]
        ext_s[:, 16:24, :] = xp.reshape(ns, SUBLANES, 4 * D_POOL_GROUP)
        pos = past_len + (row % SUBLANES)
    else:
        @pl.when(j == 0)
        def _():
            ext_s[0:16, :] = jnp.zeros((16, 4 * D_POOL_GROUP), F32)
        ext_s[16:16 + tm, :] = xp
        pos = j * tm + row
    for g, w in enumerate(POOL_WINDOWS):
        cs = slice(g * D_POOL_GROUP, (g + 1) * D_POOL_GROUP)
        tok = xp[:, cs]
        acc = tok
        for i in range(1, w):
            if decode:
                acc = acc + ext_s[:, 16 - i:24 - i, cs].reshape(tm, D_POOL_GROUP)
            else:
                acc = acc + ext_s[16 - i:16 - i + tm, cs]
        cnt = jnp.minimum(w, pos + 1).astype(F32)
        d = acc / cnt - tok
        y = _dot(d.astype(BF16), wpg_ref[g]) * psc_ref[:, cs]
        pool_ref[:, cs] = y.astype(BF16)
    if decode:
        xp_ref[...] = xp
    else:
        ext_s[0:16, :] = xp[tm - 16:tm, :]
        xp_ref[0] = xp[tm - 16:tm, :]


def _rope_tables(pos):
    half = HEAD_DIM // 2
    inv = ROPE_THETA ** (-np.arange(half, dtype=np.float64) / half)
    ang = pos.astype(np.float64)[:, None] * inv[None, :]
    cos32, sin32 = np.cos(ang).astype(np.float32), np.sin(ang).astype(np.float32)
    cos = np.tile(cos32, (1, LANES // half))
    sin = np.tile(np.concatenate([-sin32, sin32], axis=1), (1, LANES // HEAD_DIM))
    return jnp.asarray(cos), jnp.asarray(sin)


def _arrange_w_in_even(w):
    q = w[:, 0:512].reshape(-1, N_HEADS, HEAD_DIM)
    q = jnp.stack([q[:, 0:4], q[:, 4:8]], axis=2).reshape(-1, 512)
    k = w[:, 512:640]
    v = w[:, 640:768]
    iq = w[:, 768:1280]
    ik = w[:, 1280:1344]
    iw = w[:, 1344:1352]
    xp = w[:, 1352:1864]
    z = jnp.zeros_like(ik)
    main = jnp.concatenate([q, iq, ik, z, z, ik, k, v, xp], axis=1).astype(BF16)
    return main, iw.T.astype(BF16)


def _even_in(x, g, wmain, wiwt, tabs, wpg, psc, prefix, *, nb, tm, past_len):
    n, d = x.shape
    t = n // nb
    nt = t // tm
    decode = prefix is not None
    cos, sin = tabs
    row = lambda b, j: (b * nt + j, 0)
    tab_row = (lambda b, j: (0, 0)) if decode else (lambda b, j: (j, 0))
    const = lambda b, j: (0, 0)
    feat = lambda b, j: (b, 0, j)
    in_specs = [
        pl.BlockSpec((tm, d), row),
        pl.BlockSpec((1, d), const),
        pl.BlockSpec(wmain.shape, const),
        pl.BlockSpec(wiwt.shape, const),
        pl.BlockSpec((tm, LANES), tab_row),
        pl.BlockSpec((tm, LANES), tab_row),
        pl.BlockSpec(wpg.shape, lambda b, j: (0, 0, 0)),
        pl.BlockSpec((1, 512), const),
    ]
    args = [x, g, wmain, wiwt, cos, sin, wpg, psc]
    if decode:
        ns = tm // SUBLANES
        in_specs.append(pl.BlockSpec((ns, 16, 512), lambda b, j: (b * nt + j, 0, 0)))
        args.append(prefix)
        xp_shape = jax.ShapeDtypeStruct((n, 512), F32)
        xp_spec = pl.BlockSpec((tm, 512), row)
        scratch = [pltpu.VMEM((ns, 24, 512), F32), pltpu.VMEM((tm, D_KV), F32)]
    else:
        xp_shape = jax.ShapeDtypeStruct((nb, 16, 512), F32)
        xp_spec = pl.BlockSpec((1, 16, 512), lambda b, j: (b, 0, 0))
        scratch = [pltpu.VMEM((tm + 16, 512), F32), pltpu.VMEM((tm, D_KV), F32)]
    out_shape = [
        jax.ShapeDtypeStruct((n, 512), BF16),
        jax.ShapeDtypeStruct((n, 512), BF16),
        jax.ShapeDtypeStruct((n, 256), BF16),
        jax.ShapeDtypeStruct((n, 128), BF16),
        jax.ShapeDtypeStruct((nb, 128, t), F32),
        jax.ShapeDtypeStruct((nb, 128, t), F32),
        jax.ShapeDtypeStruct((nb, N_KV_HEADS * V_ROWS, t), BF16),
        jax.ShapeDtypeStruct((nb, D_IDX, t), F32),
        jax.ShapeDtypeStruct((nb, 8, t), F32),
        jax.ShapeDtypeStruct((n, 512), BF16),
        xp_shape,
    ]
    out_specs = [
        pl.BlockSpec((tm, 512), row),
        pl.BlockSpec((tm, 512), row),
        pl.BlockSpec((tm, 256), row),
        pl.BlockSpec((tm, 128), row),
        pl.BlockSpec((1, 128, tm), feat),
        pl.BlockSpec((1, 128, tm), feat),
        pl.BlockSpec((1, N_KV_HEADS * V_ROWS, tm), feat),
        pl.BlockSpec((1, D_IDX, tm), feat),
        pl.BlockSpec((1, 8, tm), feat),
        pl.BlockSpec((tm, 512), row),
        xp_spec,
    ]
    return pl.pallas_call(
        functools.partial(_even_in_kernel, tm=tm, decode=decode, past_len=past_len),
        grid=(nb, nt), in_specs=in_specs, out_specs=out_specs, out_shape=out_shape,
        scratch_shapes=scratch, compiler_params=_cparams(("arbitrary", "arbitrary")),
        name="even_in_decode" if decode else "even_in_prompt",
    )(*args)


F32_MIN_NORMAL = 2.0 ** -126
F32_LOWEST = -3.4028234663852886e38
LOWEST_FINITE_CODE = -2139095040


def _score_for_search(s, valid):
    return jnp.where(valid, jnp.where(jnp.abs(s) < F32_MIN_NORMAL, 0.0, s), -jnp.inf)


def _code_to_float(code):
    b = code ^ ((code >> 31) & 0x7FFFFFFF)
    f = lax.bitcast_convert_type(b, F32)
    tiny = ((b >> 23) & 0xFF) == 0
    return jnp.where(tiny, jnp.where(b > 0, F32_MIN_NORMAL, 0.0), f)


FRACTION_BITS = 24
MAX_TIE_WALK = 4


def _topk_search(count, shape, n_index_bits, topk, first_index=None):
    def bit_step(i, carry):
        code, c_ge = carry
        cand = code + jnp.left_shift(jnp.int32(1), 31 - i)
        cf = _code_to_float(cand)
        c = count(lambda sv, idx: sv >= cf)
        ok = c >= topk
        return jnp.where(ok, cand, code), jnp.where(ok, c, c_ge)

    code, c_ge = lax.fori_loop(0, 32, bit_step, (jnp.full(shape, INT_MIN, I32), jnp.zeros(shape, I32)))
    short = code < LOWEST_FINITE_CODE
    thr = jnp.where(short, -jnp.inf, _code_to_float(code))
    nxt = jnp.where(short, F32_LOWEST, _code_to_float(code + 1))
    ambiguous = jnp.logical_and(c_ge > topk, jnp.logical_not(short))
    width = nxt - thr
    inv_width = jnp.where(jnp.logical_and(width >= F32_MIN_NORMAL, width < jnp.inf), 1.0 / width, 0.0)

    def frac(sv):
        return (sv - thr) * inv_width

    def in_band(sv):
        return jnp.logical_and(sv >= thr, sv < nxt)

    def split_band():
        need = topk - count(lambda sv, idx: sv >= nxt)
        c_pos = count(lambda sv, idx: jnp.logical_and(in_band(sv), frac(sv) > 0.0))

        def bisect_fraction():
            def frac_step(i, m):
                cand = m + jnp.left_shift(jnp.int32(1), FRACTION_BITS - 1 - i)
                cf = cand.astype(F32) * (2.0 ** -FRACTION_BITS)
                c = count(lambda sv, idx: jnp.logical_and(in_band(sv), frac(sv) >= cf))
                return jnp.where(c >= need, cand, m)
            m = lax.fori_loop(0, FRACTION_BITS, frac_step, jnp.zeros(shape, I32))
            r = m.astype(F32) * (2.0 ** -FRACTION_BITS)
            return r, count(lambda sv, idx: jnp.logical_and(in_band(sv), frac(sv) > r))

        any_frac = jnp.max(jnp.where(ambiguous, c_pos, 0)) > 0
        rstar, c_gtr = lax.cond(any_frac, bisect_fraction, lambda: (jnp.zeros(shape, F32), c_pos))
        need_eq = need - c_gtr

        def index_cut(tied):
            def idx_step(i, lo):
                cand = lo + jnp.left_shift(jnp.int32(1), n_index_bits - 1 - i)
                c = count(lambda sv, idx: jnp.logical_and(tied(sv), idx < cand))
                return jnp.where(c < need_eq, cand, lo)
            return lax.fori_loop(0, n_index_bits, idx_step, jnp.zeros(shape, I32))

        def plain_cut():
            tied = lambda sv: sv == thr
            if first_index is None:
                return index_cut(tied)
            most = jnp.max(jnp.where(ambiguous, need_eq, 1))

            def walk():
                def step(j, cut):
                    nxt_cut = first_index(lambda sv, idx: jnp.logical_and(tied(sv), idx > cut))
                    return jnp.where(j < need_eq, nxt_cut, cut)
                return lax.fori_loop(0, most, step, jnp.full(shape, -1, I32))

            return lax.cond(most <= MAX_TIE_WALK, walk, lambda: index_cut(tied))

        cut = lax.cond(any_frac,
                       lambda: index_cut(lambda sv: jnp.logical_and(in_band(sv), frac(sv) == rstar)),
                       plain_cut)
        return jnp.where(ambiguous, rstar, -1.0), jnp.where(ambiguous, cut, take_all), any_frac

    take_all = jnp.where(short, -1, 2 ** 30).astype(I32)
    any_amb = jnp.max(ambiguous.astype(I32)) > 0
    rstar, cut, any_frac = lax.cond(any_amb, split_band,
                                    lambda: (jnp.full(shape, -1.0, F32), take_all, jnp.zeros((), jnp.bool_)))

    def select(sv, idx):
        r = frac(sv)
        taken = jnp.logical_or(r > rstar, jnp.logical_and(r == rstar, idx <= cut))
        return jnp.logical_or(sv >= nxt, jnp.logical_and(in_band(sv), taken))

    def select_no_fraction(sv, idx):
        return jnp.logical_or(sv > thr, jnp.logical_and(sv == thr, idx <= cut))

    return select, select_no_fraction, jnp.maximum(thr, F32_LOWEST), any_amb, any_frac


def _index_bits(n):
    return max(1, int(np.ceil(np.log2(n))))


def _fold_rows(x, op):
    return op(x.reshape(x.shape[0] // SUBLANES, SUBLANES, x.shape[1]), axis=0)


def _dsa_prompt_kernel(q_ref, iq_ref, iwt_ref, kb_ref, vt_ref, ik2_ref, o_ref,
                       keys_s, qm_s, iqs_s, acc_s, *, tk, topk):
    i = pl.program_id(1)
    n_tiles = (i * Q_BLOCK + Q_BLOCK + tk - 1) // tk
    lane = lax.broadcasted_iota(I32, (1, LANES), 1)
    qpos = i * Q_BLOCK + lane
    lo_half = lane < HEAD_DIM
    nq = KV_GROUP * Q_BLOCK

    for jj in range(KV_GROUP):
        qt = q_ref[:, jj * LANES:(jj + 1) * LANES]
        zero = jnp.zeros_like(qt)
        qm_s[0, jj * Q_BLOCK:(jj + 1) * Q_BLOCK, :] = jnp.where(lo_half, qt, zero)
        qm_s[1, jj * Q_BLOCK:(jj + 1) * Q_BLOCK, :] = jnp.where(lo_half, zero, qt)
        iqs_s[jj * Q_BLOCK:(jj + 1) * Q_BLOCK, :] = iq_ref[:, jj * LANES:(jj + 1) * LANES]
    w = iwt_ref[0]

    def score_tile(t, carry):
        r0 = pl.multiple_of(t * tk, tk)
        ik2 = ik2_ref[pl.ds(r0, tk), :]
        iqs = iqs_s[...]
        rel_e = _dot_nt(ik2[:, 0:LANES], iqs)
        rel_o = _dot_nt(ik2[:, LANES:2 * LANES], iqs)
        s = jnp.zeros((tk, LANES), F32)
        for jj in range(4):
            cs = slice(jj * LANES, (jj + 1) * LANES)
            s = s + w[2 * jj:2 * jj + 1, :] * jnp.maximum(rel_e[:, cs], 0.0)
            s = s + w[2 * jj + 1:2 * jj + 2, :] * jnp.maximum(rel_o[:, cs], 0.0)
        kpos = r0 + lax.broadcasted_iota(I32, (tk, 1), 0)
        keys_s[pl.ds(r0, tk), :] = _score_for_search(s, kpos <= qpos)
        return carry

    lax.fori_loop(0, n_tiles, score_tile, 0)

    def count(pred):
        def body(t, acc):
            r0 = pl.multiple_of(t * tk, tk)
            sv = keys_s[pl.ds(r0, tk), :]
            idx = r0 + lax.broadcasted_iota(I32, (tk, 1), 0)
            return acc + _fold_rows(pred(sv, idx).astype(I32), jnp.sum)
        acc = lax.fori_loop(0, n_tiles, body, jnp.zeros((SUBLANES, LANES), I32))
        return acc.sum(axis=0, keepdims=True)

    def first_index(pred):
        far = jnp.int32(2 ** 30)
        def body(t, acc):
            r0 = pl.multiple_of(t * tk, tk)
            idx = r0 + lax.broadcasted_iota(I32, (tk, 1), 0)
            return jnp.minimum(acc, _fold_rows(jnp.where(pred(keys_s[pl.ds(r0, tk), :], idx), idx, far), jnp.min))
        acc = lax.fori_loop(0, n_tiles, body, jnp.full((SUBLANES, LANES), far, I32))
        return acc.min(axis=0, keepdims=True)

    select, select_no_fraction, thr_ge, any_cut, any_frac = _topk_search(
        count, (1, LANES), _index_bits(keys_s.shape[0]), topk, first_index)

    def attend(selector):
        def pv_tile(t, ms):
            r0 = pl.multiple_of(t * tk, tk)
            sv = keys_s[pl.ds(r0, tk), :]
            if selector is None:
                sel = sv >= thr_ge
            else:
                sel = selector(sv, r0 + lax.broadcasted_iota(I32, (tk, 1), 0))
            sel = jnp.concatenate([sel] * KV_GROUP, axis=1)
            kt = kb_ref[pl.ds(r0, tk), :]
            sts = [jnp.where(sel, _dot_nt(kt, qm_s[g]), NEG_BIG) for g in range(N_KV_HEADS)]
            ms_new = []
            for g, st in enumerate(sts):
                m_new = jnp.maximum(ms[g], jnp.max(_fold_rows(st, jnp.max), axis=0, keepdims=True))
                alpha = jnp.exp2(ms[g] - m_new)
                p = jnp.exp2(st - m_new)
                ms_new.append(m_new)
                vt = vt_ref[0, g * V_ROWS:(g + 1) * V_ROWS, pl.ds(r0, tk)]
                acc_s[g] = acc_s[g] * alpha + _dot(vt, p.astype(BF16))
            return tuple(ms_new)

        lax.fori_loop(0, n_tiles, pv_tile, tuple(jnp.full((1, nq), NEG_BIG, F32) for _ in range(N_KV_HEADS)))

    acc_s[...] = jnp.zeros(acc_s.shape, F32)
    lax.cond(any_cut,
             lambda: lax.cond(any_frac, lambda: attend(select), lambda: attend(select_no_fraction)),
             lambda: attend(None))
    heads = []
    for g in range(N_KV_HEADS):
        og = acc_s[g, 0:HEAD_DIM, :] / acc_s[g, HEAD_DIM:HEAD_DIM + 1, :]
        heads += [og[:, jj * Q_BLOCK:(jj + 1) * Q_BLOCK] for jj in range(KV_GROUP)]
    o_ref[...] = jnp.concatenate(heads, axis=0).T.astype(o_ref.dtype)


def _store_heads(o_ref, heads, lo_half):
    for jj in range(N_HEADS // 2):
        a, b = heads[2 * jj], heads[2 * jj + 1]
        if 2 * jj < KV_GROUP:
            tile = jnp.where(lo_half, a, pltpu.roll(b, HEAD_DIM, 1))
        else:
            tile = jnp.where(lo_half, pltpu.roll(a, HEAD_DIM, 1), b)
        o_ref[:, jj * LANES:(jj + 1) * LANES] = tile.astype(o_ref.dtype)


def _dsa_prompt(q, iq, iwt, kb, vtb, ik2, *, nb, tk, topk):
    n = q.shape[0]
    t = n // nb
    nq = t // Q_BLOCK
    assert t % tk == 0
    blk = lambda b, i: (b * nq + i, 0)
    seq = lambda b, i: (b, 0)
    feat = lambda b, i: (b, 0, 0)
    return pl.pallas_call(
        functools.partial(_dsa_prompt_kernel, tk=tk, topk=topk),
        grid=(nb, nq),
        in_specs=[
            pl.BlockSpec((Q_BLOCK, 512), blk),
            pl.BlockSpec((Q_BLOCK, 512), blk),
            pl.BlockSpec((1, 8, Q_BLOCK), lambda b, i: (b, 0, i)),
            pl.BlockSpec((t, 128), seq),
            pl.BlockSpec((1, N_KV_HEADS * V_ROWS, t), feat),
            pl.BlockSpec((t, 256), seq),
        ],
        out_specs=pl.BlockSpec((Q_BLOCK, 512), blk),
        out_shape=jax.ShapeDtypeStruct((n, 512), BF16),
        scratch_shapes=[
            pltpu.VMEM((t, LANES), F32),
            pltpu.VMEM((N_KV_HEADS, KV_GROUP * Q_BLOCK, LANES), BF16),
            pltpu.VMEM((4 * Q_BLOCK, LANES), BF16),
            pltpu.VMEM((N_KV_HEADS, V_ROWS, KV_GROUP * Q_BLOCK), F32),
        ],
        compiler_params=_cparams(("arbitrary", "arbitrary")),
        name="dsa_prompt",
    )(q, iq, iwt, kb, vtb, ik2)


SEQ_GROUP = Q_BLOCK // SUBLANES


def _pad_rows(a):
    return jnp.concatenate([a, jnp.zeros((LANES - a.shape[0], a.shape[1]), a.dtype)], axis=0).astype(BF16)


def _decode_select_kernel(*refs, n_pages, topk, nnew):
    ip = refs[1:1 + n_pages]
    iqd_ref, wcol_ref, inew_ref, bias_ref, keys_s = refs[1 + n_pages:]
    t = pl.program_id(1)
    n_keys = keys_s.shape[1]
    past = n_pages * PAGE_SIZE
    n_tiles = n_keys // LANES
    lane = lax.broadcasted_iota(I32, (1, LANES), 1)
    qrow = lax.broadcasted_iota(I32, (nnew, 1), 0)
    new_valid = jnp.logical_and(lane <= qrow, lane < nnew)

    def head_sum(x):
        return x.reshape(N_IDX_HEADS, nnew, x.shape[1]).sum(axis=0)

    r0 = pl.multiple_of(t * nnew, nnew)
    iqd = iqd_ref[0]
    wcol = wcol_ref[0]
    for p in range(n_pages):
        rel = _dot(iqd, ip[p][0].astype(BF16))
        s = head_sum(wcol * jnp.maximum(rel, 0.0))
        keys_s[pl.ds(r0, nnew), p * LANES:(p + 1) * LANES] = _score_for_search(s, True)
    rel = _dot_nt(iqd, _pad_rows(inew_ref[0]))
    s = head_sum(wcol * jnp.maximum(rel, 0.0))
    keys_s[pl.ds(r0, nnew), past:past + LANES] = _score_for_search(s, new_valid)

    @pl.when(t == SEQ_GROUP - 1)
    def _search():
        def count(pred):
            acc = jnp.zeros((Q_BLOCK, LANES), I32)
            for c in range(n_tiles):
                idx = c * LANES + lane
                acc = acc + pred(keys_s[:, c * LANES:(c + 1) * LANES], idx).astype(I32)
            return acc.sum(axis=1, keepdims=True)
        select, _, _, _, _ = _topk_search(count, (Q_BLOCK, 1), _index_bits(n_keys), topk)
        for c in range(n_tiles):
            sel = select(keys_s[:, c * LANES:(c + 1) * LANES], c * LANES + lane)
            bias_ref[:, c * LANES:(c + 1) * LANES] = jnp.where(sel, 0.0, NEG_BIG)


def _decode_attend_kernel(*refs, n_pages, nnew):
    kp = refs[1:1 + n_pages]
    vp = refs[1 + n_pages:1 + 2 * n_pages]
    qd_ref, knew_ref, vnew_ref, bias_ref, o_ref = refs[1 + 2 * n_pages:]
    past = n_pages * PAGE_SIZE
    lane = lax.broadcasted_iota(I32, (1, LANES), 1)
    bias = jnp.concatenate([bias_ref[...]] * N_HEADS, axis=0)
    qd = qd_ref[0]
    st = jnp.concatenate([_dot(qd, kp[p][0].astype(BF16)) for p in range(n_pages)]
                         + [_dot_nt(qd, _pad_rows(knew_ref[0]))], axis=1) + bias
    m = jnp.max(st, axis=1, keepdims=True)
    pr = jnp.exp2(st - m)
    l = jnp.sum(pr, axis=1, keepdims=True)
    pb = pr.astype(BF16)
    o = _dot(pb[:, past:past + LANES], _pad_rows(vnew_ref[0]))
    for p in range(n_pages):
        o = o + _dot_nt(pb[:, p * LANES:(p + 1) * LANES], vp[p][0].astype(BF16))
    o = o / l
    _store_heads(o_ref, [o[h * nnew:(h + 1) * nnew] for h in range(N_HEADS)], lane < HEAD_DIM)


def _dsa_decode(page_table, ci_t, ck_t, cv_t, iqd, wcol, inew, qd, knew, vnew, *, topk):
    ns, n_pages = page_table.shape
    nnew = knew.shape[1]
    assert ns % SEQ_GROUP == 0 and nnew == SUBLANES
    n_keys = (n_pages + 1) * PAGE_SIZE

    sel_page = lambda p: pl.BlockSpec((1, D_IDX, PAGE_SIZE),
                                      lambda g, t, pt, p=p: (pt[g * SEQ_GROUP + t, p], 0, 0))
    sel3 = lambda g, t, pt: (g * SEQ_GROUP + t, 0, 0)
    bias = pl.pallas_call(
        functools.partial(_decode_select_kernel, n_pages=n_pages, topk=topk, nnew=nnew),
        grid_spec=pltpu.PrefetchScalarGridSpec(
            num_scalar_prefetch=1, grid=(ns // SEQ_GROUP, SEQ_GROUP),
            in_specs=([sel_page(p) for p in range(n_pages)]
                      + [pl.BlockSpec((1, N_IDX_HEADS * nnew, D_IDX), sel3),
                         pl.BlockSpec((1, N_IDX_HEADS * nnew, 1), sel3),
                         pl.BlockSpec((1, nnew, D_IDX), sel3)]),
            out_specs=pl.BlockSpec((Q_BLOCK, n_keys), lambda g, t, pt: (g, 0)),
            scratch_shapes=[pltpu.VMEM((Q_BLOCK, n_keys), F32)]),
        out_shape=jax.ShapeDtypeStruct((ns * nnew, n_keys), F32),
        compiler_params=_cparams(("arbitrary", "arbitrary")),
        name="dsa_decode_select",
    )(page_table, *([ci_t] * n_pages), iqd, wcol, inew)

    att_page = lambda p: pl.BlockSpec((1, D_KV, PAGE_SIZE), lambda s, pt, p=p: (pt[s, p], 0, 0))
    att3 = lambda s, pt: (s, 0, 0)
    return pl.pallas_call(
        functools.partial(_decode_attend_kernel, n_pages=n_pages, nnew=nnew),
        grid_spec=pltpu.PrefetchScalarGridSpec(
            num_scalar_prefetch=1, grid=(ns,),
            in_specs=([att_page(p) for p in range(n_pages)] + [att_page(p) for p in range(n_pages)]
                      + [pl.BlockSpec((1, N_HEADS * nnew, D_KV), att3),
                         pl.BlockSpec((1, nnew, D_KV), att3),
                         pl.BlockSpec((1, nnew, D_KV), att3),
                         pl.BlockSpec((nnew, n_keys), lambda s, pt: (s, 0))]),
            out_specs=pl.BlockSpec((nnew, D_ATTN), lambda s, pt: (s, 0))),
        out_shape=jax.ShapeDtypeStruct((ns * nnew, D_ATTN), F32),
        compiler_params=_cparams(("arbitrary",)),
        name="dsa_decode_attend",
    )(page_table, *([ck_t] * n_pages), *([cv_t] * n_pages), qd, knew, vnew, bias)


FF_CHUNK = 256


def _ffn_block(h, gpre, gpost, wgu_ref, wd_ref, a_s):
    d_ff = wd_ref.shape[0]
    hn = _rms(h, gpre).astype(BF16)
    for c in range(0, d_ff, FF_CHUNK):
        gate = _dot(hn, wgu_ref[:, c:c + FF_CHUNK])
        up = _dot(hn, wgu_ref[:, d_ff + c:d_ff + c + FF_CHUNK])
        a_s[:, c:c + FF_CHUNK] = (gate * jax.nn.sigmoid(gate) * up).astype(BF16)
    return h + _rms(_dot(a_s[...], wd_ref[...]), gpost)


def _even_out_kernel(x_ref, attn_ref, pool_ref, wo_ref, gmix_ref, gpre_ref, gpost_ref, wgu_ref, wd_ref,
                     o_ref, a_s):
    mix = _dot(attn_ref[...], wo_ref[0:D_ATTN, :]) + _dot(pool_ref[...], wo_ref[D_ATTN:, :])
    h = x_ref[...] + _rms(mix, gmix_ref[...])
    o_ref[...] = _ffn_block(h, gpre_ref[...], gpost_ref[...], wgu_ref, wd_ref, a_s)


def _even_out(x, attn, pool, wo, gmix, gpre, gpost, wgu, wd, *, tm):
    n, d = x.shape
    row = lambda i: (i, 0)
    const = lambda i: (0, 0)
    return pl.pallas_call(
        _even_out_kernel, grid=(n // tm,),
        in_specs=[pl.BlockSpec((tm, d), row), pl.BlockSpec((tm, 512), row), pl.BlockSpec((tm, 512), row),
                  pl.BlockSpec(wo.shape, const), pl.BlockSpec((1, d), const), pl.BlockSpec((1, d), const),
                  pl.BlockSpec((1, d), const), pl.BlockSpec(wgu.shape, const), pl.BlockSpec(wd.shape, const)],
        out_specs=pl.BlockSpec((tm, d), row),
        out_shape=jax.ShapeDtypeStruct((n, d), F32),
        scratch_shapes=[pltpu.VMEM((tm, wd.shape[0]), BF16)],
        compiler_params=_cparams(("arbitrary",)),
        name="even_out_ffn",
    )(x, attn, pool, wo, gmix, gpre, gpost, wgu, wd)


def _odd_kernel(x_ref, g_ref, win_ref, lng_ref, lnb_ref, ws_ref, bs_ref, wout_ref, gmix_ref,
                gpre_ref, gpost_ref, wgu_ref, wd_ref, o_ref, z_ref, y_s, a_s, *, tm, seq_len):
    d_sgu = wout_ref.shape[0]
    dh = d_sgu // N_SGU_HEADS
    x = x_ref[...]
    xn = _rms(x, g_ref[...]).astype(BF16)
    r = lax.broadcasted_iota(I32, (CHUNK, CHUNK), 0)
    c = lax.broadcasted_iota(I32, (CHUNK, CHUNK), 1)
    causal = jnp.logical_and(r // seq_len == c // seq_len, c <= r)
    for hd in range(N_SGU_HEADS):
        cs = slice(hd * dh, (hd + 1) * dh)
        u = _gelu(_dot(xn, win_ref[:, cs]))
        y_s[:, cs] = u.astype(BF16)
    vs = []
    for hd in range(N_SGU_HEADS):
        cs = slice(d_sgu + hd * dh, d_sgu + (hd + 1) * dh)
        vs.append(_gelu(_dot(xn, win_ref[:, cs])))
    v = jnp.concatenate(vs, axis=1)
    mu = jnp.mean(v, axis=-1, keepdims=True)
    vc = v - mu
    z = vc * lax.rsqrt(jnp.mean(vc * vc, axis=-1, keepdims=True) + EPS) * lng_ref[...] + lnb_ref[...]
    z_ref[...] = z
    zb = z.astype(BF16)
    for hd in range(N_SGU_HEADS):
        cs = slice(hd * dh, (hd + 1) * dh)
        wm = jnp.where(causal, ws_ref[hd], 0.0).astype(BF16)
        bias = bs_ref[hd]
        for ch in range(tm // CHUNK):
            rs = slice(ch * CHUNK, (ch + 1) * CHUNK)
            s = _dot(wm, zb[rs, cs]) + bias
            y_s[rs, cs] = (y_s[rs, cs].astype(F32) * s).astype(BF16)
    h = x + _rms(_dot(y_s[...], wout_ref[...]), gmix_ref[...])
    o_ref[...] = _ffn_block(h, gpre_ref[...], gpost_ref[...], wgu_ref, wd_ref, a_s)


def _odd(x, g, win, lng, lnb, ws, bs, wout, gmix, gpre, gpost, wgu, wd, *, tm, seq_len):
    n, d = x.shape
    d_sgu = wout.shape[0]
    row = lambda i: (i, 0)
    const = lambda i: (0, 0)
    const3 = lambda i: (0, 0, 0)
    return pl.pallas_call(
        functools.partial(_odd_kernel, tm=tm, seq_len=seq_len), grid=(n // tm,),
        in_specs=[pl.BlockSpec((tm, d), row), pl.BlockSpec((1, d), const), pl.BlockSpec(win.shape, const),
                  pl.BlockSpec((1, d_sgu), const), pl.BlockSpec((1, d_sgu), const),
                  pl.BlockSpec(ws.shape, const3), pl.BlockSpec(bs.shape, const3),
                  pl.BlockSpec(wout.shape, const), pl.BlockSpec((1, d), const), pl.BlockSpec((1, d), const),
                  pl.BlockSpec((1, d), const), pl.BlockSpec(wgu.shape, const), pl.BlockSpec(wd.shape, const)],
        out_specs=[pl.BlockSpec((tm, d), row), pl.BlockSpec((tm, d_sgu), row)],
        out_shape=[jax.ShapeDtypeStruct((n, d), F32), jax.ShapeDtypeStruct((n, d_sgu), F32)],
        scratch_shapes=[pltpu.VMEM((tm, d_sgu), BF16), pltpu.VMEM((tm, wd.shape[0]), BF16)],
        compiler_params=_cparams(("arbitrary",)),
        name="odd_mixer_ffn",
    )(x, g, win, lng, lnb, ws, bs, wout, gmix, gpre, gpost, wgu, wd)


PROMPT_ROW_TILE = 512
DECODE_ROW_TILE = 256
PROMPT_KEY_TILE = 512


def _tile_rows(n, want):
    tm = min(want, n)
    while n % tm:
        tm //= 2
    return tm


def _from_feature_major(a, heads):
    b, f, t = a.shape
    if heads is None:
        return jnp.transpose(a, (0, 2, 1))
    return jnp.transpose(a.reshape(b, heads, f // heads, t), (0, 3, 1, 2))


def kernel(x_prompt, x_sample, cache_k, cache_v, cache_idx_k, state_pool, page_table, norm_mix_pre, norm_mix_post, norm_ffn_pre, norm_ffn_post, w_in_even, w_out_even, w_pool_group, pool_scale, w_in_odd, sgu_norm_g, sgu_norm_b, w_spatial, b_spatial, w_out_odd, w_ffn_gate_up, w_ffn_down):
    bp, tp, d = x_prompt.shape
    bs, ts, _ = x_sample.shape
    n_pages = page_table.shape[1]
    past_len = n_pages * PAGE_SIZE
    topk_p = min(TOPK_MAX, tp // 4)
    topk_s = min(TOPK_MAX, (past_len + ts) // 4)
    depth = norm_mix_pre.shape[0]
    assert ts == SUBLANES and tp % Q_BLOCK == 0 and d % LANES == 0

    hp = x_prompt.reshape(bp * tp, d)
    hs = x_sample.reshape(bs * ts, d)
    tm_p = _tile_rows(bp * tp, PROMPT_ROW_TILE)
    tm_p_seq = _tile_rows(tp, PROMPT_ROW_TILE)
    tm_s = _tile_rows(bs * ts, DECODE_ROW_TILE)
    row = lambda a: a.reshape(1, -1)

    tabs_p = _rope_tables(np.arange(tp))
    tabs_s = _rope_tables(np.tile(past_len + np.arange(ts), tm_s // ts))

    outs_p, outs_s, sgu_s = [], [], []
    for layer in range(depth):
        li = layer // 2
        gpre, gpost = row(norm_ffn_pre[layer]), row(norm_ffn_post[layer])
        wgu = w_ffn_gate_up[layer].astype(BF16)
        wd = w_ffn_down[layer].astype(BF16)
        gmix_pre, gmix_post = row(norm_mix_pre[layer]), row(norm_mix_post[layer])
        if layer % 2 == 0:
            wmain, wiwt = _arrange_w_in_even(w_in_even[li])
            wpg = w_pool_group[li].astype(BF16)
            psc = row(pool_scale[li])
            wo = w_out_even[li].astype(BF16)
            (q, iq, ik2, kb, kt, vt, vtb, ikt, iwt, pool, xp_tail) = _even_in(
                hp, gmix_pre, wmain, wiwt, tabs_p, wpg, psc, None, nb=bp, tm=tm_p_seq, past_len=0)
            attn = _dsa_prompt(q, iq, iwt, kb, vtb, ik2, nb=bp, tk=min(PROMPT_KEY_TILE, tp), topk=topk_p)
            hp = _even_out(hp, attn, pool, wo, gmix_post, gpre, gpost, wgu, wd, tm=tm_p)
            outs_p.append((_from_feature_major(kt, N_KV_HEADS), _from_feature_major(vt, N_KV_HEADS),
                           _from_feature_major(ikt, None), xp_tail[:, 1:, :]))
            prefix = jnp.pad(state_pool[li], ((0, 0), (1, 0), (0, 0)))
            (q, iq, ik2, kb, kt, vt, vtb, ikt, iwt, pool, xp) = _even_in(
                hs, gmix_pre, wmain, wiwt, tabs_s, wpg, psc, prefix, nb=1, tm=tm_s, past_len=past_len)
            del kb, vtb
            k_new = _from_feature_major(kt.reshape(1, D_KV, bs * ts), None).reshape(bs, ts, D_KV)
            v_new = _from_feature_major(vt.reshape(1, D_KV, bs * ts), None).reshape(bs, ts, D_KV)
            ik_new = _from_feature_major(ikt, None).reshape(bs, ts, D_IDX)
            q3 = q.reshape(bs, ts, 4, 2, HEAD_DIM)
            zq = jnp.zeros_like(q3[:, :, :, 0])
            qg = jnp.concatenate([jnp.concatenate([q3[:, :, :, 0], zq], axis=-1),
                                  jnp.concatenate([zq, q3[:, :, :, 1]], axis=-1)], axis=2)
            qd = jnp.transpose(qg, (0, 2, 1, 3)).reshape(bs, N_HEADS * ts, D_KV)
            iqd = jnp.transpose(iq.reshape(bs, ts, N_IDX_HEADS, D_IDX), (0, 2, 1, 3)).reshape(bs, N_IDX_HEADS * ts, D_IDX)
            wcol = jnp.transpose(iwt[0].reshape(N_IDX_HEADS, bs, ts), (1, 0, 2)).reshape(bs, N_IDX_HEADS * ts, 1)
            ci_t = jnp.transpose(cache_idx_k[li], (0, 2, 1))
            ck_t = jnp.transpose(cache_k[li], (0, 2, 3, 1)).reshape(-1, D_KV, PAGE_SIZE)
            cv_t = jnp.transpose(cache_v[li], (0, 2, 3, 1)).reshape(-1, D_KV, PAGE_SIZE)
            attn_s = _dsa_decode(page_table, ci_t, ck_t, cv_t, iqd, wcol, ik_new, qd, k_new, v_new, topk=topk_s)
            hs = _even_out(hs, attn_s.astype(BF16), pool, wo, gmix_post, gpre, gpost, wgu, wd, tm=tm_s)
            new_pool_s = jnp.concatenate([state_pool[li][:, ts:, :], xp.reshape(bs, ts, -1)], axis=1)
            outs_s.append((k_new.reshape(bs, ts, N_KV_HEADS, HEAD_DIM), v_new.reshape(bs, ts, N_KV_HEADS, HEAD_DIM),
                           ik_new, new_pool_s))
        else:
            win = w_in_odd[li].astype(BF16)
            wout = w_out_odd[li].astype(BF16)
            lng, lnb = row(sgu_norm_g[li]), row(sgu_norm_b[li])
            ws_p = w_spatial[li][:, :CHUNK, :CHUNK]
            bs_p = b_spatial[li][:, :CHUNK, None]
            hp, _ = _odd(hp, gmix_pre, win, lng, lnb, ws_p, bs_p, wout, gmix_post, gpre, gpost, wgu, wd,
                         tm=tm_p, seq_len=CHUNK)
            reps = CHUNK // ts
            ws_s = jnp.tile(w_spatial[li][:, :ts, :ts], (1, reps, reps))
            bs_s = jnp.tile(b_spatial[li][:, :ts], (1, reps))[:, :, None]
            hs, zs = _odd(hs, gmix_pre, win, lng, lnb, ws_s, bs_s, wout, gmix_post, gpre, gpost, wgu, wd,
                          tm=tm_s, seq_len=ts)
            sgu_s.append(zs.reshape(bs, ts, -1))

    return (hp.reshape(bp, tp, d), hs.reshape(bs, ts, d),
            jnp.stack([o[0] for o in outs_p]), jnp.stack([o[1] for o in outs_p]),
            jnp.stack([o[2] for o in outs_p]), jnp.stack([o[3] for o in outs_p]),
            jnp.stack([o[0] for o in outs_s]), jnp.stack([o[1] for o in outs_s]),
            jnp.stack([o[2] for o in outs_s]), jnp.stack([o[3] for o in outs_s]),
            jnp.stack(sgu_s))
```

```python
import functools

import jax
import jax.numpy as jnp
import numpy as np
from jax import lax
from jax.experimental import pallas as pl
from jax.experimental.pallas import tpu as pltpu

EPS = 1e-6
N_HEADS = 8
HEAD_DIM = 64
N_KV_HEADS = 2
KV_GROUP = N_HEADS // N_KV_HEADS
D_ATTN = N_HEADS * HEAD_DIM
D_KV = N_KV_HEADS * HEAD_DIM
N_IDX_HEADS = 8
D_IDX = 64
TOPK_MAX = 256
Q_BLOCK = 128
ROPE_THETA = 10000.0
POOL_WINDOWS = (2, 4, 8, 16)
POOL_BUF = 15
PAGE_SIZE = 128
CHUNK = 128
N_SGU_HEADS = 8
D_POOL_GROUP = 128
V_ROWS = HEAD_DIM + 16

LANES = 128
SUBLANES = 8
VMEM_LIMIT = 56 * 1024 * 1024
INT_MIN = -2 ** 31
NEG_BIG = -1e30
LOG2E = 1.4426950408889634

F32 = jnp.float32
BF16 = jnp.bfloat16
I32 = jnp.int32

_NT = (((1,), (1,)), ((), ()))


def _cparams(sem):
    return pltpu.CompilerParams(dimension_semantics=sem, vmem_limit_bytes=VMEM_LIMIT)


def _rms(x, g):
    return x * lax.rsqrt(jnp.mean(x * x, axis=-1, keepdims=True) + EPS) * g


def _dot(a, b):
    return jnp.dot(a, b, preferred_element_type=F32)


def _dot_nt(a, b):
    return lax.dot_general(a, b, _NT, preferred_element_type=F32)


def _gelu(x):
    return 0.5 * x * (1.0 + lax.erf(x * (2.0 ** -0.5)))


C_Q, C_IQ, C_IK2, C_K, C_V, C_XP, C_END = 0, 512, 1024, 1280, 1408, 1536, 2048


def _rope_tile(t, cos, sin, first_half):
    partner = jnp.where(first_half, pltpu.roll(t, 96, 1), pltpu.roll(t, 32, 1))
    return t * cos + partner * sin


def _even_in_kernel(*refs, tm, decode, past_len):
    if decode:
        (x_ref, g_ref, w_ref, wiwt_ref, cos_ref, sin_ref, wpg_ref, psc_ref, pre_ref,
         q_ref, iq_ref, ik2_ref, kb_ref, kt_ref, vt_ref, vtb_ref, ikt_ref, iwt_ref, pool_ref, xp_ref,
         ext_s, vtok_s) = refs
    else:
        (x_ref, g_ref, w_ref, wiwt_ref, cos_ref, sin_ref, wpg_ref, psc_ref,
         q_ref, iq_ref, ik2_ref, kb_ref, kt_ref, vt_ref, vtb_ref, ikt_ref, iwt_ref, pool_ref, xp_ref,
         ext_s, vtok_s) = refs
    j = pl.program_id(1)
    xn = _rms(x_ref[...], g_ref[...]).astype(BF16)
    cos = cos_ref[...]
    sin = sin_ref[...]
    lane = lax.broadcasted_iota(I32, (1, LANES), 1)
    first_half = (lane % HEAD_DIM) < (HEAD_DIM // 2)

    def proj_rope(c0, c1):
        t = _dot(xn, w_ref[:, c0:c1])
        return [_rope_tile(t[:, c:c + LANES], cos, sin, first_half) for c in range(0, c1 - c0, LANES)]

    for half in range(2):
        tiles = proj_rope(C_Q + 256 * half, C_Q + 256 * (half + 1))
        for c, t in enumerate(tiles):
            col = 256 * half + LANES * c
            q_ref[:, col:col + LANES] = (t * (HEAD_DIM ** -0.5 * LOG2E)).astype(BF16)
    for half in range(2):
        tiles = proj_rope(C_IQ + 256 * half, C_IQ + 256 * (half + 1))
        for c, t in enumerate(tiles):
            col = 256 * half + LANES * c
            iq_ref[:, col:col + LANES] = t.astype(BF16)
    tiles = proj_rope(C_IK2, C_K)
    ik2_ref[:, 0:LANES] = tiles[0].astype(BF16)
    ik2_ref[:, LANES:2 * LANES] = tiles[1].astype(BF16)
    ikt_ref[0] = tiles[0].T[0:D_IDX]
    k_tile = proj_rope(C_K, C_V)[0]
    kb_ref[...] = k_tile.astype(BF16)
    kt_ref[0] = k_tile.T
    vtok_s[...] = _dot(xn, w_ref[:, C_V:C_XP])
    vt = vtok_s[...].T
    vt_ref[0] = vt
    pad = jnp.concatenate([jnp.ones((1, tm), F32), jnp.zeros((V_ROWS - HEAD_DIM - 1, tm), F32)], axis=0)
    for h in range(N_KV_HEADS):
        vtb_ref[0, h * V_ROWS:h * V_ROWS + HEAD_DIM, :] = vt[h * HEAD_DIM:(h + 1) * HEAD_DIM].astype(BF16)
        vtb_ref[0, h * V_ROWS + HEAD_DIM:(h + 1) * V_ROWS, :] = pad.astype(BF16)
    iwt_ref[0] = _dot_nt(wiwt_ref[...], xn) * (N_IDX_HEADS ** -0.5) * (D_IDX ** -0.5)

    xp = _dot(xn, w_ref[:, C_XP:C_END])
    row = lax.broadcasted_iota(I32, (tm, 1), 0)
    if decode:
        ns = tm // SUBLANES
        ext_s[:, 0:16, :] = pre_ref[...]
        ext_s[:, 16:24, :] = xp.reshape(ns, SUBLANES, 4 * D_POOL_GROUP)
        pos = past_len + (row % SUBLANES)
    else:
        @pl.when(j == 0)
        def _():
            ext_s[0:16, :] = jnp.zeros((16, 4 * D_POOL_GROUP), F32)
        ext_s[16:16 + tm, :] = xp
        pos = j * tm + row
    for g, w in enumerate(POOL_WINDOWS):
        cs = slice(g * D_POOL_GROUP, (g + 1) * D_POOL_GROUP)
        tok = xp[:, cs]
        acc = tok
        for i in range(1, w):
            if decode:
                acc = acc + ext_s[:, 16 - i:24 - i, cs].reshape(tm, D_POOL_GROUP)
            else:
                acc = acc + ext_s[16 - i:16 - i + tm, cs]
        cnt = jnp.minimum(w, pos + 1).astype(F32)
        d = acc / cnt - tok
        y = _dot(d.astype(BF16), wpg_ref[g]) * psc_ref[:, cs]
        pool_ref[:, cs] = y.astype(BF16)
    if decode:
        xp_ref[...] = xp
    else:
        ext_s[0:16, :] = xp[tm - 16:tm, :]
        xp_ref[0] = xp[tm - 16:tm, :]


def _rope_tables(pos):
    half = HEAD_DIM // 2
    inv = ROPE_THETA ** (-np.arange(half, dtype=np.float64) / half)
    ang = pos.astype(np.float64)[:, None] * inv[None, :]
    cos32, sin32 = np.cos(ang).astype(np.float32), np.sin(ang).astype(np.float32)
    cos = np.tile(cos32, (1, LANES // half))
    sin = np.tile(np.concatenate([-sin32, sin32], axis=1), (1, LANES // HEAD_DIM))
    return jnp.asarray(cos), jnp.asarray(sin)


def _arrange_w_in_even(w):
    q = w[:, 0:512].reshape(-1, N_HEADS, HEAD_DIM)
    q = jnp.stack([q[:, 0:4], q[:, 4:8]], axis=2).reshape(-1, 512)
    k = w[:, 512:640]
    v = w[:, 640:768]
    iq = w[:, 768:1280]
    ik = w[:, 1280:1344]
    iw = w[:, 1344:1352]
    xp = w[:, 1352:1864]
    z = jnp.zeros_like(ik)
    main = jnp.concatenate([q, iq, ik, z, z, ik, k, v, xp], axis=1).astype(BF16)
    return main, iw.T.astype(BF16)


def _even_in(x, g, wmain, wiwt, tabs, wpg, psc, prefix, *, nb, tm, past_len):
    n, d = x.shape
    t = n // nb
    nt = t // tm
    decode = prefix is not None
    cos, sin = tabs
    row = lambda b, j: (b * nt + j, 0)
    tab_row = (lambda b, j: (0, 0)) if decode else (lambda b, j: (j, 0))
    const = lambda b, j: (0, 0)
    feat = lambda b, j: (b, 0, j)
    in_specs = [
        pl.BlockSpec((tm, d), row),
        pl.BlockSpec((1, d), const),
        pl.BlockSpec(wmain.shape, const),
        pl.BlockSpec(wiwt.shape, const),
        pl.BlockSpec((tm, LANES), tab_row),
        pl.BlockSpec((tm, LANES), tab_row),
        pl.BlockSpec(wpg.shape, lambda b, j: (0, 0, 0)),
        pl.BlockSpec((1, 512), const),
    ]
    args = [x, g, wmain, wiwt, cos, sin, wpg, psc]
    if decode:
        ns = tm // SUBLANES
        in_specs.append(pl.BlockSpec((ns, 16, 512), lambda b, j: (b * nt + j, 0, 0)))
        args.append(prefix)
        xp_shape = jax.ShapeDtypeStruct((n, 512), F32)
        xp_spec = pl.BlockSpec((tm, 512), row)
        scratch = [pltpu.VMEM((ns, 24, 512), F32), pltpu.VMEM((tm, D_KV), F32)]
    else:
        xp_shape = jax.ShapeDtypeStruct((nb, 16, 512), F32)
        xp_spec = pl.BlockSpec((1, 16, 512), lambda b, j: (b, 0, 0))
        scratch = [pltpu.VMEM((tm + 16, 512), F32), pltpu.VMEM((tm, D_KV), F32)]
    out_shape = [
        jax.ShapeDtypeStruct((n, 512), BF16),
        jax.ShapeDtypeStruct((n, 512), BF16),
        jax.ShapeDtypeStruct((n, 256), BF16),
        jax.ShapeDtypeStruct((n, 128), BF16),
        jax.ShapeDtypeStruct((nb, 128, t), F32),
        jax.ShapeDtypeStruct((nb, 128, t), F32),
        jax.ShapeDtypeStruct((nb, N_KV_HEADS * V_ROWS, t), BF16),
        jax.ShapeDtypeStruct((nb, D_IDX, t), F32),
        jax.ShapeDtypeStruct((nb, 8, t), F32),
        jax.ShapeDtypeStruct((n, 512), BF16),
        xp_shape,
    ]
    out_specs = [
        pl.BlockSpec((tm, 512), row),
        pl.BlockSpec((tm, 512), row),
        pl.BlockSpec((tm, 256), row),
        pl.BlockSpec((tm, 128), row),
        pl.BlockSpec((1, 128, tm), feat),
        pl.BlockSpec((1, 128, tm), feat),
        pl.BlockSpec((1, N_KV_HEADS * V_ROWS, tm), feat),
        pl.BlockSpec((1, D_IDX, tm), feat),
        pl.BlockSpec((1, 8, tm), feat),
        pl.BlockSpec((tm, 512), row),
        xp_spec,
    ]
    return pl.pallas_call(
        functools.partial(_even_in_kernel, tm=tm, decode=decode, past_len=past_len),
        grid=(nb, nt), in_specs=in_specs, out_specs=out_specs, out_shape=out_shape,
        scratch_shapes=scratch, compiler_params=_cparams(("arbitrary", "arbitrary")),
        name="even_in_decode" if decode else "even_in_prompt",
    )(*args)


F32_MIN_NORMAL = 2.0 ** -126
F32_LOWEST = -3.4028234663852886e38
LOWEST_FINITE_CODE = -2139095040


def _score_for_search(s, valid):
    return jnp.where(valid, jnp.where(jnp.abs(s) < F32_MIN_NORMAL, 0.0, s), -jnp.inf)


def _code_to_float(code):
    b = code ^ ((code >> 31) & 0x7FFFFFFF)
    f = lax.bitcast_convert_type(b, F32)
    tiny = ((b >> 23) & 0xFF) == 0
    return jnp.where(tiny, jnp.where(b > 0, F32_MIN_NORMAL, 0.0), f)


FRACTION_BITS = 24
MAX_TIE_WALK = 4


def _topk_search(count, shape, n_index_bits, topk, first_index=None):
    def bit_step(i, carry):
        code, c_ge = carry
        cand = code + jnp.left_shift(jnp.int32(1), 31 - i)
        cf = _code_to_float(cand)
        c = count(lambda sv, idx: sv >= cf)
        ok = c >= topk
        return jnp.where(ok, cand, code), jnp.where(ok, c, c_ge)

    code, c_ge = lax.fori_loop(0, 32, bit_step, (jnp.full(shape, INT_MIN, I32), jnp.zeros(shape, I32)))
    short = code < LOWEST_FINITE_CODE
    thr = jnp.where(short, -jnp.inf, _code_to_float(code))
    nxt = jnp.where(short, F32_LOWEST, _code_to_float(code + 1))
    ambiguous = jnp.logical_and(c_ge > topk, jnp.logical_not(short))
    width = nxt - thr
    inv_width = jnp.where(jnp.logical_and(width >= F32_MIN_NORMAL, width < jnp.inf), 1.0 / width, 0.0)

    def frac(sv):
        return (sv - thr) * inv_width

    def in_band(sv):
        return jnp.logical_and(sv >= thr, sv < nxt)

    def split_band():
        need = topk - count(lambda sv, idx: sv >= nxt)
        c_pos = count(lambda sv, idx: jnp.logical_and(in_band(sv), frac(sv) > 0.0))

        def bisect_fraction():
            def frac_step(i, m):
                cand = m + jnp.left_shift(jnp.int32(1), FRACTION_BITS - 1 - i)
                cf = cand.astype(F32) * (2.0 ** -FRACTION_BITS)
                c = count(lambda sv, idx: jnp.logical_and(in_band(sv), frac(sv) >= cf))
                return jnp.where(c >= need, cand, m)
            m = lax.fori_loop(0, FRACTION_BITS, frac_step, jnp.zeros(shape, I32))
            r = m.astype(F32) * (2.0 ** -FRACTION_BITS)
            return r, count(lambda sv, idx: jnp.logical_and(in_band(sv), frac(sv) > r))

        any_frac = jnp.max(jnp.where(ambiguous, c_pos, 0)) > 0
        rstar, c_gtr = lax.cond(any_frac, bisect_fraction, lambda: (jnp.zeros(shape, F32), c_pos))
        need_eq = need - c_gtr

        def index_cut(tied):
            def idx_step(i, lo):
                cand = lo + jnp.left_shift(jnp.int32(1), n_index_bits - 1 - i)
                c = count(lambda sv, idx: jnp.logical_and(tied(sv), idx < cand))
                return jnp.where(c < need_eq, cand, lo)
            return lax.fori_loop(0, n_index_bits, idx_step, jnp.zeros(shape, I32))

        def plain_cut():
            tied = lambda sv: sv == thr
            if first_index is None:
                return index_cut(tied)
            most = jnp.max(jnp.where(ambiguous, need_eq, 1))

            def walk():
                def step(j, cut):
                    nxt_cut = first_index(lambda sv, idx: jnp.logical_and(tied(sv), idx > cut))
                    return jnp.where(j < need_eq, nxt_cut, cut)
                return lax.fori_loop(0, most, step, jnp.full(shape, -1, I32))

            return lax.cond(most <= MAX_TIE_WALK, walk, lambda: index_cut(tied))

        cut = lax.cond(any_frac,
                       lambda: index_cut(lambda sv: jnp.logical_and(in_band(sv), frac(sv) == rstar)),
                       plain_cut)
        return jnp.where(ambiguous, rstar, -1.0), jnp.where(ambiguous, cut, take_all), any_frac

    take_all = jnp.where(short, -1, 2 ** 30).astype(I32)
    any_amb = jnp.max(ambiguous.astype(I32)) > 0
    rstar, cut, any_frac = lax.cond(any_amb, split_band,
                                    lambda: (jnp.full(shape, -1.0, F32), take_all, jnp.zeros((), jnp.bool_)))

    def select(sv, idx):
        r = frac(sv)
        taken = jnp.logical_or(r > rstar, jnp.logical_and(r == rstar, idx <= cut))
        return jnp.logical_or(sv >= nxt, jnp.logical_and(in_band(sv), taken))

    def select_no_fraction(sv, idx):
        return jnp.logical_or(sv > thr, jnp.logical_and(sv == thr, idx <= cut))

    return select, select_no_fraction, jnp.maximum(thr, F32_LOWEST), any_amb, any_frac


def _index_bits(n):
    return max(1, int(np.ceil(np.log2(n))))


def _fold_rows(x, op):
    return op(x.reshape(x.shape[0] // SUBLANES, SUBLANES, x.shape[1]), axis=0)


def _dsa_prompt_kernel(q_ref, iq_ref, iwt_ref, kb_ref, vt_ref, ik2_ref, o_ref,
                       keys_s, qm_s, iqs_s, acc_s, *, tk, topk):
    i = pl.program_id(1)
    n_tiles = (i * Q_BLOCK + Q_BLOCK + tk - 1) // tk
    lane = lax.broadcasted_iota(I32, (1, LANES), 1)
    qpos = i * Q_BLOCK + lane
    lo_half = lane < HEAD_DIM
    nq = KV_GROUP * Q_BLOCK

    for jj in range(KV_GROUP):
        qt = q_ref[:, jj * LANES:(jj + 1) * LANES]
        zero = jnp.zeros_like(qt)
        qm_s[0, jj * Q_BLOCK:(jj + 1) * Q_BLOCK, :] = jnp.where(lo_half, qt, zero)
        qm_s[1, jj * Q_BLOCK:(jj + 1) * Q_BLOCK, :] = jnp.where(lo_half, zero, qt)
        iqs_s[jj * Q_BLOCK:(jj + 1) * Q_BLOCK, :] = iq_ref[:, jj * LANES:(jj + 1) * LANES]
    w = iwt_ref[0]

    def score_tile(t, carry):
        r0 = pl.multiple_of(t * tk, tk)
        ik2 = ik2_ref[pl.ds(r0, tk), :]
        iqs = iqs_s[...]
        rel_e = _dot_nt(ik2[:, 0:LANES], iqs)
        rel_o = _dot_nt(ik2[:, LANES:2 * LANES], iqs)
        s = jnp.zeros((tk, LANES), F32)
        for jj in range(4):
            cs = slice(jj * LANES, (jj + 1) * LANES)
            s = s + w[2 * jj:2 * jj + 1, :] * jnp.maximum(rel_e[:, cs], 0.0)
            s = s + w[2 * jj + 1:2 * jj + 2, :] * jnp.maximum(rel_o[:, cs], 0.0)
        kpos = r0 + lax.broadcasted_iota(I32, (tk, 1), 0)
        keys_s[pl.ds(r0, tk), :] = _score_for_search(s, kpos <= qpos)
        return carry

    lax.fori_loop(0, n_tiles, score_tile, 0)

    pair = (keys_s.shape[0] // tk) % 2 == 0
    if pair:
        @pl.when(n_tiles % 2 == 1)
        def _():
            keys_s[pl.ds(pl.multiple_of(n_tiles * tk, tk), tk), :] = jnp.full((tk, LANES), -jnp.inf, F32)
    per_step = 2 if pair else 1

    def count(pred):
        def body(p, acc):
            for u in range(per_step):
                r0 = pl.multiple_of((p * per_step + u) * tk, tk)
                sv = keys_s[pl.ds(r0, tk), :]
                idx = r0 + lax.broadcasted_iota(I32, (tk, 1), 0)
                acc = acc + _fold_rows(pred(sv, idx).astype(I32), jnp.sum)
            return acc
        acc = lax.fori_loop(0, (n_tiles + per_step - 1) // per_step, body, jnp.zeros((SUBLANES, LANES), I32))
        return acc.sum(axis=0, keepdims=True)

    def first_index(pred):
        far = jnp.int32(2 ** 30)
        def body(t, acc):
            r0 = pl.multiple_of(t * tk, tk)
            idx = r0 + lax.broadcasted_iota(I32, (tk, 1), 0)
            return jnp.minimum(acc, _fold_rows(jnp.where(pred(keys_s[pl.ds(r0, tk), :], idx), idx, far), jnp.min))
        acc = lax.fori_loop(0, n_tiles, body, jnp.full((SUBLANES, LANES), far, I32))
        return acc.min(axis=0, keepdims=True)

    select, select_no_fraction, thr_ge, any_cut, any_frac = _topk_search(
        count, (1, LANES), _index_bits(keys_s.shape[0]), topk, first_index)

    def attend(selector):
        def pv_tile(t, ms):
            r0 = pl.multiple_of(t * tk, tk)
            sv = keys_s[pl.ds(r0, tk), :]
            if selector is None:
                sel = sv >= thr_ge
            else:
                sel = selector(sv, r0 + lax.broadcasted_iota(I32, (tk, 1), 0))
            sel = jnp.concatenate([sel] * KV_GROUP, axis=1)
            kt = kb_ref[pl.ds(r0, tk), :]
            sts = [jnp.where(sel, _dot_nt(kt, qm_s[g]), NEG_BIG) for g in range(N_KV_HEADS)]
            ms_new = []
            for g, st in enumerate(sts):
                m_new = jnp.maximum(ms[g], jnp.max(_fold_rows(st, jnp.max), axis=0, keepdims=True))
                alpha = jnp.exp2(ms[g] - m_new)
                p = jnp.exp2(st - m_new)
                ms_new.append(m_new)
                vt = vt_ref[0, g * V_ROWS:(g + 1) * V_ROWS, pl.ds(r0, tk)]
                acc_s[g] = acc_s[g] * alpha + _dot(vt, p.astype(BF16))
            return tuple(ms_new)

        lax.fori_loop(0, n_tiles, pv_tile, tuple(jnp.full((1, nq), NEG_BIG, F32) for _ in range(N_KV_HEADS)))

    acc_s[...] = jnp.zeros(acc_s.shape, F32)
    lax.cond(any_cut,
             lambda: lax.cond(any_frac, lambda: attend(select), lambda: attend(select_no_fraction)),
             lambda: attend(None))
    heads = []
    for g in range(N_KV_HEADS):
        og = acc_s[g, 0:HEAD_DIM, :] / acc_s[g, HEAD_DIM:HEAD_DIM + 1, :]
        heads += [og[:, jj * Q_BLOCK:(jj + 1) * Q_BLOCK] for jj in range(KV_GROUP)]
    o_ref[...] = jnp.concatenate(heads, axis=0).T.astype(o_ref.dtype)


def _store_heads(o_ref, heads, lo_half):
    for jj in range(N_HEADS // 2):
        a, b = heads[2 * jj], heads[2 * jj + 1]
        if 2 * jj < KV_GROUP:
            tile = jnp.where(lo_half, a, pltpu.roll(b, HEAD_DIM, 1))
        else:
            tile = jnp.where(lo_half, pltpu.roll(a, HEAD_DIM, 1), b)
        o_ref[:, jj * LANES:(jj + 1) * LANES] = tile.astype(o_ref.dtype)


def _dsa_prompt(q, iq, iwt, kb, vtb, ik2, *, nb, tk, topk):
    n = q.shape[0]
    t = n // nb
    nq = t // Q_BLOCK
    assert t % tk == 0
    blk = lambda b, i: (b * nq + i, 0)
    seq = lambda b, i: (b, 0)
    feat = lambda b, i: (b, 0, 0)
    return pl.pallas_call(
        functools.partial(_dsa_prompt_kernel, tk=tk, topk=topk),
        grid=(nb, nq),
        in_specs=[
            pl.BlockSpec((Q_BLOCK, 512), blk),
            pl.BlockSpec((Q_BLOCK, 512), blk),
            pl.BlockSpec((1, 8, Q_BLOCK), lambda b, i: (b, 0, i)),
            pl.BlockSpec((t, 128), seq),
            pl.BlockSpec((1, N_KV_HEADS * V_ROWS, t), feat),
            pl.BlockSpec((t, 256), seq),
        ],
        out_specs=pl.BlockSpec((Q_BLOCK, 512), blk),
        out_shape=jax.ShapeDtypeStruct((n, 512), BF16),
        scratch_shapes=[
            pltpu.VMEM((t, LANES), F32),
            pltpu.VMEM((N_KV_HEADS, KV_GROUP * Q_BLOCK, LANES), BF16),
            pltpu.VMEM((4 * Q_BLOCK, LANES), BF16),
            pltpu.VMEM((N_KV_HEADS, V_ROWS, KV_GROUP * Q_BLOCK), F32),
        ],
        compiler_params=_cparams(("arbitrary", "arbitrary")),
        name="dsa_prompt",
    )(q, iq, iwt, kb, vtb, ik2)


SEQ_GROUP = Q_BLOCK // SUBLANES


def _pad_rows(a):
    return jnp.concatenate([a, jnp.zeros((LANES - a.shape[0], a.shape[1]), a.dtype)], axis=0).astype(BF16)


def _decode_select_kernel(*refs, n_pages, topk, nnew):
    ip = refs[1:1 + n_pages]
    iqd_ref, wcol_ref, inew_ref, bias_ref, keys_s = refs[1 + n_pages:]
    t = pl.program_id(1)
    n_keys = keys_s.shape[1]
    past = n_pages * PAGE_SIZE
    n_tiles = n_keys // LANES
    lane = lax.broadcasted_iota(I32, (1, LANES), 1)
    qrow = lax.broadcasted_iota(I32, (nnew, 1), 0)
    new_valid = jnp.logical_and(lane <= qrow, lane < nnew)

    def head_sum(x):
        return x.reshape(N_IDX_HEADS, nnew, x.shape[1]).sum(axis=0)

    r0 = pl.multiple_of(t * nnew, nnew)
    iqd = iqd_ref[0]
    wcol = wcol_ref[0]
    for p in range(n_pages):
        rel = _dot(iqd, ip[p][0].astype(BF16))
        s = head_sum(wcol * jnp.maximum(rel, 0.0))
        keys_s[pl.ds(r0, nnew), p * LANES:(p + 1) * LANES] = _score_for_search(s, True)
    rel = _dot_nt(iqd, _pad_rows(inew_ref[0]))
    s = head_sum(wcol * jnp.maximum(rel, 0.0))
    keys_s[pl.ds(r0, nnew), past:past + LANES] = _score_for_search(s, new_valid)

    @pl.when(t == SEQ_GROUP - 1)
    def _search():
        def count(pred):
            acc = jnp.zeros((Q_BLOCK, LANES), I32)
            for c in range(n_tiles):
                idx = c * LANES + lane
                acc = acc + pred(keys_s[:, c * LANES:(c + 1) * LANES], idx).astype(I32)
            return acc.sum(axis=1, keepdims=True)
        select, _, _, _, _ = _topk_search(count, (Q_BLOCK, 1), _index_bits(n_keys), topk)
        for c in range(n_tiles):
            sel = select(keys_s[:, c * LANES:(c + 1) * LANES], c * LANES + lane)
            bias_ref[:, c * LANES:(c + 1) * LANES] = jnp.where(sel, 0.0, NEG_BIG)


def _decode_attend_kernel(*refs, n_pages, nnew):
    kp = refs[1:1 + n_pages]
    vp = refs[1 + n_pages:1 + 2 * n_pages]
    qd_ref, knew_ref, vnew_ref, bias_ref, o_ref = refs[1 + 2 * n_pages:]
    past = n_pages * PAGE_SIZE
    lane = lax.broadcasted_iota(I32, (1, LANES), 1)
    bias = jnp.concatenate([bias_ref[...]] * N_HEADS, axis=0)
    qd = qd_ref[0]
    st = jnp.concatenate([_dot(qd, kp[p][0].astype(BF16)) for p in range(n_pages)]
                         + [_dot_nt(qd, _pad_rows(knew_ref[0]))], axis=1) + bias
    m = jnp.max(st, axis=1, keepdims=True)
    pr = jnp.exp2(st - m)
    l = jnp.sum(pr, axis=1, keepdims=True)
    pb = pr.astype(BF16)
    o = _dot(pb[:, past:past + LANES], _pad_rows(vnew_ref[0]))
    for p in range(n_pages):
        o = o + _dot_nt(pb[:, p * LANES:(p + 1) * LANES], vp[p][0].astype(BF16))
    o = o / l
    _store_heads(o_ref, [o[h * nnew:(h + 1) * nnew] for h in range(N_HEADS)], lane < HEAD_DIM)


def _dsa_decode(page_table, ci_t, ck_t, cv_t, iqd, wcol, inew, qd, knew, vnew, *, topk):
    ns, n_pages = page_table.shape
    nnew = knew.shape[1]
    assert ns % SEQ_GROUP == 0 and nnew == SUBLANES
    n_keys = (n_pages + 1) * PAGE_SIZE

    sel_page = lambda p: pl.BlockSpec((1, D_IDX, PAGE_SIZE),
                                      lambda g, t, pt, p=p: (pt[g * SEQ_GROUP + t, p], 0, 0))
    sel3 = lambda g, t, pt: (g * SEQ_GROUP + t, 0, 0)
    bias = pl.pallas_call(
        functools.partial(_decode_select_kernel, n_pages=n_pages, topk=topk, nnew=nnew),
        grid_spec=pltpu.PrefetchScalarGridSpec(
            num_scalar_prefetch=1, grid=(ns // SEQ_GROUP, SEQ_GROUP),
            in_specs=([sel_page(p) for p in range(n_pages)]
                      + [pl.BlockSpec((1, N_IDX_HEADS * nnew, D_IDX), sel3),
                         pl.BlockSpec((1, N_IDX_HEADS * nnew, 1), sel3),
                         pl.BlockSpec((1, nnew, D_IDX), sel3)]),
            out_specs=pl.BlockSpec((Q_BLOCK, n_keys), lambda g, t, pt: (g, 0)),
            scratch_shapes=[pltpu.VMEM((Q_BLOCK, n_keys), F32)]),
        out_shape=jax.ShapeDtypeStruct((ns * nnew, n_keys), F32),
        compiler_params=_cparams(("arbitrary", "arbitrary")),
        name="dsa_decode_select",
    )(page_table, *([ci_t] * n_pages), iqd, wcol, inew)

    att_page = lambda p: pl.BlockSpec((1, D_KV, PAGE_SIZE), lambda s, pt, p=p: (pt[s, p], 0, 0))
    att3 = lambda s, pt: (s, 0, 0)
    return pl.pallas_call(
        functools.partial(_decode_attend_kernel, n_pages=n_pages, nnew=nnew),
        grid_spec=pltpu.PrefetchScalarGridSpec(
            num_scalar_prefetch=1, grid=(ns,),
            in_specs=([att_page(p) for p in range(n_pages)] + [att_page(p) for p in range(n_pages)]
                      + [pl.BlockSpec((1, N_HEADS * nnew, D_KV), att3),
                         pl.BlockSpec((1, nnew, D_KV), att3),
                         pl.BlockSpec((1, nnew, D_KV), att3),
                         pl.BlockSpec((nnew, n_keys), lambda s, pt: (s, 0))]),
            out_specs=pl.BlockSpec((nnew, D_ATTN), lambda s, pt: (s, 0))),
        out_shape=jax.ShapeDtypeStruct((ns * nnew, D_ATTN), F32),
        compiler_params=_cparams(("arbitrary",)),
        name="dsa_decode_attend",
    )(page_table, *([ck_t] * n_pages), *([cv_t] * n_pages), qd, knew, vnew, bias)


FF_CHUNK = 256


def _ffn_block(h, gpre, gpost, wgu_ref, wd_ref, a_s):
    d_ff = wd_ref.shape[0]
    hn = _rms(h, gpre).astype(BF16)
    for c in range(0, d_ff, FF_CHUNK):
        gate = _dot(hn, wgu_ref[:, c:c + FF_CHUNK])
        up = _dot(hn, wgu_ref[:, d_ff + c:d_ff + c + FF_CHUNK])
        a_s[:, c:c + FF_CHUNK] = (gate * jax.nn.sigmoid(gate) * up).astype(BF16)
    return h + _rms(_dot(a_s[...], wd_ref[...]), gpost)


def _even_out_kernel(x_ref, attn_ref, pool_ref, wo_ref, gmix_ref, gpre_ref, gpost_ref, wgu_ref, wd_ref,
                     o_ref, a_s):
    mix = _dot(attn_ref[...], wo_ref[0:D_ATTN, :]) + _dot(pool_ref[...], wo_ref[D_ATTN:, :])
    h = x_ref[...] + _rms(mix, gmix_ref[...])
    o_ref[...] = _ffn_block(h, gpre_ref[...], gpost_ref[...], wgu_ref, wd_ref, a_s)


def _even_out(x, attn, pool, wo, gmix, gpre, gpost, wgu, wd, *, tm):
    n, d = x.shape
    row = lambda i: (i, 0)
    const = lambda i: (0, 0)
    return pl.pallas_call(
        _even_out_kernel, grid=(n // tm,),
        in_specs=[pl.BlockSpec((tm, d), row), pl.BlockSpec((tm, 512), row), pl.BlockSpec((tm, 512), row),
                  pl.BlockSpec(wo.shape, const), pl.BlockSpec((1, d), const), pl.BlockSpec((1, d), const),
                  pl.BlockSpec((1, d), const), pl.BlockSpec(wgu.shape, const), pl.BlockSpec(wd.shape, const)],
        out_specs=pl.BlockSpec((tm, d), row),
        out_shape=jax.ShapeDtypeStruct((n, d), F32),
        scratch_shapes=[pltpu.VMEM((tm, wd.shape[0]), BF16)],
        compiler_params=_cparams(("arbitrary",)),
        name="even_out_ffn",
    )(x, attn, pool, wo, gmix, gpre, gpost, wgu, wd)


def _odd_kernel(x_ref, g_ref, win_ref, lng_ref, lnb_ref, ws_ref, bs_ref, wout_ref, gmix_ref,
                gpre_ref, gpost_ref, wgu_ref, wd_ref, o_ref, z_ref, y_s, a_s, *, tm, seq_len):
    d_sgu = wout_ref.shape[0]
    dh = d_sgu // N_SGU_HEADS
    x = x_ref[...]
    xn = _rms(x, g_ref[...]).astype(BF16)
    r = lax.broadcasted_iota(I32, (CHUNK, CHUNK), 0)
    c = lax.broadcasted_iota(I32, (CHUNK, CHUNK), 1)
    causal = jnp.logical_and(r // seq_len == c // seq_len, c <= r)
    for hd in range(N_SGU_HEADS):
        cs = slice(hd * dh, (hd + 1) * dh)
        u = _gelu(_dot(xn, win_ref[:, cs]))
        y_s[:, cs] = u.astype(BF16)
    vs = []
    for hd in range(N_SGU_HEADS):
        cs = slice(d_sgu + hd * dh, d_sgu + (hd + 1) * dh)
        vs.append(_gelu(_dot(xn, win_ref[:, cs])))
    v = jnp.concatenate(vs, axis=1)
    mu = jnp.mean(v, axis=-1, keepdims=True)
    vc = v - mu
    z = vc * lax.rsqrt(jnp.mean(vc * vc, axis=-1, keepdims=True) + EPS) * lng_ref[...] + lnb_ref[...]
    z_ref[...] = z
    zb = z.astype(BF16)
    for hd in range(N_SGU_HEADS):
        cs = slice(hd * dh, (hd + 1) * dh)
        wm = jnp.where(causal, ws_ref[hd], 0.0).astype(BF16)
        bias = bs_ref[hd]
        for ch in range(tm // CHUNK):
            rs = slice(ch * CHUNK, (ch + 1) * CHUNK)
            s = _dot(wm, zb[rs, cs]) + bias
            y_s[rs, cs] = (y_s[rs, cs].astype(F32) * s).astype(BF16)
    h = x + _rms(_dot(y_s[...], wout_ref[...]), gmix_ref[...])
    o_ref[...] = _ffn_block(h, gpre_ref[...], gpost_ref[...], wgu_ref, wd_ref, a_s)


def _odd(x, g, win, lng, lnb, ws, bs, wout, gmix, gpre, gpost, wgu, wd, *, tm, seq_len):
    n, d = x.shape
    d_sgu = wout.shape[0]
    row = lambda i: (i, 0)
    const = lambda i: (0, 0)
    const3 = lambda i: (0, 0, 0)
    return pl.pallas_call(
        functools.partial(_odd_kernel, tm=tm, seq_len=seq_len), grid=(n // tm,),
        in_specs=[pl.BlockSpec((tm, d), row), pl.BlockSpec((1, d), const), pl.BlockSpec(win.shape, const),
                  pl.BlockSpec((1, d_sgu), const), pl.BlockSpec((1, d_sgu), const),
                  pl.BlockSpec(ws.shape, const3), pl.BlockSpec(bs.shape, const3),
                  pl.BlockSpec(wout.shape, const), pl.BlockSpec((1, d), const), pl.BlockSpec((1, d), const),
                  pl.BlockSpec((1, d), const), pl.BlockSpec(wgu.shape, const), pl.BlockSpec(wd.shape, const)],
        out_specs=[pl.BlockSpec((tm, d), row), pl.BlockSpec((tm, d_sgu), row)],
        out_shape=[jax.ShapeDtypeStruct((n, d), F32), jax.ShapeDtypeStruct((n, d_sgu), F32)],
        scratch_shapes=[pltpu.VMEM((tm, d_sgu), BF16), pltpu.VMEM((tm, wd.shape[0]), BF16)],
        compiler_params=_cparams(("arbitrary",)),
        name="odd_mixer_ffn",
    )(x, g, win, lng, lnb, ws, bs, wout, gmix, gpre, gpost, wgu, wd)


PROMPT_ROW_TILE = 512
DECODE_ROW_TILE = 256
PROMPT_KEY_TILE = 512


def _tile_rows(n, want):
    tm = min(want, n)
    while n % tm:
        tm //= 2
    return tm


def _from_feature_major(a, heads):
    b, f, t = a.shape
    if heads is None:
        return jnp.transpose(a, (0, 2, 1))
    return jnp.transpose(a.reshape(b, heads, f // heads, t), (0, 3, 1, 2))


def kernel(x_prompt, x_sample, cache_k, cache_v, cache_idx_k, state_pool, page_table, norm_mix_pre, norm_mix_post, norm_ffn_pre, norm_ffn_post, w_in_even, w_out_even, w_pool_group, pool_scale, w_in_odd, sgu_norm_g, sgu_norm_b, w_spatial, b_spatial, w_out_odd, w_ffn_gate_up, w_ffn_down):
    bp, tp, d = x_prompt.shape
    bs, ts, _ = x_sample.shape
    n_pages = page_table.shape[1]
    past_len = n_pages * PAGE_SIZE
    topk_p = min(TOPK_MAX, tp // 4)
    topk_s = min(TOPK_MAX, (past_len + ts) // 4)
    depth = norm_mix_pre.shape[0]
    assert ts == SUBLANES and tp % Q_BLOCK == 0 and d % LANES == 0

    hp = x_prompt.reshape(bp * tp, d)
    hs = x_sample.reshape(bs * ts, d)
    tm_p = _tile_rows(bp * tp, PROMPT_ROW_TILE)
    tm_p_seq = _tile_rows(tp, PROMPT_ROW_TILE)
    tm_s = _tile_rows(bs * ts, DECODE_ROW_TILE)
    row = lambda a: a.reshape(1, -1)

    tabs_p = _rope_tables(np.arange(tp))
    tabs_s = _rope_tables(np.tile(past_len + np.arange(ts), tm_s // ts))

    outs_p, outs_s, sgu_s = [], [], []
    for layer in range(depth):
        li = layer // 2
        gpre, gpost = row(norm_ffn_pre[layer]), row(norm_ffn_post[layer])
        wgu = w_ffn_gate_up[layer].astype(BF16)
        wd = w_ffn_down[layer].astype(BF16)
        gmix_pre, gmix_post = row(norm_mix_pre[layer]), row(norm_mix_post[layer])
        if layer % 2 == 0:
            wmain, wiwt = _arrange_w_in_even(w_in_even[li])
            wpg = w_pool_group[li].astype(BF16)
            psc = row(pool_scale[li])
            wo = w_out_even[li].astype(BF16)
            (q, iq, ik2, kb, kt, vt, vtb, ikt, iwt, pool, xp_tail) = _even_in(
                hp, gmix_pre, wmain, wiwt, tabs_p, wpg, psc, None, nb=bp, tm=tm_p_seq, past_len=0)
            attn = _dsa_prompt(q, iq, iwt, kb, vtb, ik2, nb=bp, tk=min(PROMPT_KEY_TILE, tp), topk=topk_p)
            hp = _even_out(hp, attn, pool, wo, gmix_post, gpre, gpost, wgu, wd, tm=tm_p)
            outs_p.append((_from_feature_major(kt, N_KV_HEADS), _from_feature_major(vt, N_KV_HEADS),
                           _from_feature_major(ikt, None), xp_tail[:, 1:, :]))
            prefix = jnp.pad(state_pool[li], ((0, 0), (1, 0), (0, 0)))
            (q, iq, ik2, kb, kt, vt, vtb, ikt, iwt, pool, xp) = _even_in(
                hs, gmix_pre, wmain, wiwt, tabs_s, wpg, psc, prefix, nb=1, tm=tm_s, past_len=past_len)
            del kb, vtb
            k_new = _from_feature_major(kt.reshape(1, D_KV, bs * ts), None).reshape(bs, ts, D_KV)
            v_new = _from_feature_major(vt.reshape(1, D_KV, bs * ts), None).reshape(bs, ts, D_KV)
            ik_new = _from_feature_major(ikt, None).reshape(bs, ts, D_IDX)
            q3 = q.reshape(bs, ts, 4, 2, HEAD_DIM)
            zq = jnp.zeros_like(q3[:, :, :, 0])
            qg = jnp.concatenate([jnp.concatenate([q3[:, :, :, 0], zq], axis=-1),
                                  jnp.concatenate([zq, q3[:, :, :, 1]], axis=-1)], axis=2)
            qd = jnp.transpose(qg, (0, 2, 1, 3)).reshape(bs, N_HEADS * ts, D_KV)
            iqd = jnp.transpose(iq.reshape(bs, ts, N_IDX_HEADS, D_IDX), (0, 2, 1, 3)).reshape(bs, N_IDX_HEADS * ts, D_IDX)
            wcol = jnp.transpose(iwt[0].reshape(N_IDX_HEADS, bs, ts), (1, 0, 2)).reshape(bs, N_IDX_HEADS * ts, 1)
            ci_t = jnp.transpose(cache_idx_k[li], (0, 2, 1))
            ck_t = jnp.transpose(cache_k[li], (0, 2, 3, 1)).reshape(-1, D_KV, PAGE_SIZE)
            cv_t = jnp.transpose(cache_v[li], (0, 2, 3, 1)).reshape(-1, D_KV, PAGE_SIZE)
            attn_s = _dsa_decode(page_table, ci_t, ck_t, cv_t, iqd, wcol, ik_new, qd, k_new, v_new, topk=topk_s)
            hs = _even_out(hs, attn_s.astype(BF16), pool, wo, gmix_post, gpre, gpost, wgu, wd, tm=tm_s)
            new_pool_s = jnp.concatenate([state_pool[li][:, ts:, :], xp.reshape(bs, ts, -1)], axis=1)
            outs_s.append((k_new.reshape(bs, ts, N_KV_HEADS, HEAD_DIM), v_new.reshape(bs, ts, N_KV_HEADS, HEAD_DIM),
                           ik_new, new_pool_s))
        else:
            win = w_in_odd[li].astype(BF16)
            wout = w_out_odd[li].astype(BF16)
            lng, lnb = row(sgu_norm_g[li]), row(sgu_norm_b[li])
            ws_p = w_spatial[li][:, :CHUNK, :CHUNK]
            bs_p = b_spatial[li][:, :CHUNK, None]
            hp, _ = _odd(hp, gmix_pre, win, lng, lnb, ws_p, bs_p, wout, gmix_post, gpre, gpost, wgu, wd,
                         tm=tm_p, seq_len=CHUNK)
            reps = CHUNK // ts
            ws_s = jnp.tile(w_spatial[li][:, :ts, :ts], (1, reps, reps))
            bs_s = jnp.tile(b_spatial[li][:, :ts], (1, reps))[:, :, None]
            hs, zs = _odd(hs, gmix_pre, win, lng, lnb, ws_s, bs_s, wout, gmix_post, gpre, gpost, wgu, wd,
                          tm=tm_s, seq_len=ts)
            sgu_s.append(zs.reshape(bs, ts, -1))

    return (hp.reshape(bp, tp, d), hs.reshape(bs, ts, d),
            jnp.stack([o[0] for o in outs_p]), jnp.stack([o[1] for o in outs_p]),
            jnp.stack([o[2] for o in outs_p]), jnp.stack([o[3] for o in outs_p]),
            jnp.stack([o[0] for o in outs_s]), jnp.stack([o[1] for o in outs_s]),
            jnp.stack([o[2] for o in outs_s]), jnp.stack([o[3] for o in outs_s]),
            jnp.stack(sgu_s))
```

```python
import functools

import jax
import jax.numpy as jnp
import numpy as np
from jax import lax
from jax.experimental import pallas as pl
from jax.experimental.pallas import tpu as pltpu

EPS = 1e-6
N_HEADS = 8
HEAD_DIM = 64
N_KV_HEADS = 2
KV_GROUP = N_HEADS // N_KV_HEADS
D_ATTN = N_HEADS * HEAD_DIM
D_KV = N_KV_HEADS * HEAD_DIM
N_IDX_HEADS = 8
D_IDX = 64
TOPK_MAX = 256
Q_BLOCK = 128
ROPE_THETA = 10000.0
POOL_WINDOWS = (2, 4, 8, 16)
POOL_BUF = 15
PAGE_SIZE = 128
CHUNK = 128
N_SGU_HEADS = 8
D_POOL_GROUP = 128
V_ROWS = HEAD_DIM + 16

LANES = 128
SUBLANES = 8
VMEM_LIMIT = 56 * 1024 * 1024
INT_MIN = -2 ** 31
NEG_BIG = -1e30
LOG2E = 1.4426950408889634

F32 = jnp.float32
BF16 = jnp.bfloat16
I32 = jnp.int32

_NT = (((1,), (1,)), ((), ()))


def _cparams(sem):
    return pltpu.CompilerParams(dimension_semantics=sem, vmem_limit_bytes=VMEM_LIMIT)


def _rms(x, g):
    return x * lax.rsqrt(jnp.mean(x * x, axis=-1, keepdims=True) + EPS) * g


def _dot(a, b):
    return jnp.dot(a, b, preferred_element_type=F32)


def _dot_nt(a, b):
    return lax.dot_general(a, b, _NT, preferred_element_type=F32)


def _gelu(x):
    return 0.5 * x * (1.0 + lax.erf(x * (2.0 ** -0.5)))


C_Q, C_IQ, C_IK2, C_K, C_V, C_XP, C_END = 0, 512, 1024, 1280, 1408, 1536, 2048


def _rope_tile(t, cos, sin, first_half):
    partner = jnp.where(first_half, pltpu.roll(t, 96, 1), pltpu.roll(t, 32, 1))
    return t * cos + partner * sin


def _even_in_kernel(*refs, tm, decode, past_len):
    if decode:
        (x_ref, g_ref, w_ref, wiwt_ref, cos_ref, sin_ref, wpg_ref, psc_ref, pre_ref,
         q_ref, iq_ref, ik2_ref, kb_ref, kt_ref, vt_ref, vtb_ref, ikt_ref, iwt_ref, pool_ref, xp_ref,
         ext_s, vtok_s) = refs
    else:
        (x_ref, g_ref, w_ref, wiwt_ref, cos_ref, sin_ref, wpg_ref, psc_ref,
         q_ref, iq_ref, ik2_ref, kb_ref, kt_ref, vt_ref, vtb_ref, ikt_ref, iwt_ref, pool_ref, xp_ref,
         ext_s, vtok_s) = refs
    j = pl.program_id(1)
    xn = _rms(x_ref[...], g_ref[...]).astype(BF16)
    cos = cos_ref[...]
    sin = sin_ref[...]
    lane = lax.broadcasted_iota(I32, (1, LANES), 1)
    first_half = (lane % HEAD_DIM) < (HEAD_DIM // 2)

    def proj_rope(c0, c1):
        t = _dot(xn, w_ref[:, c0:c1])
        return [_rope_tile(t[:, c:c + LANES], cos, sin, first_half) for c in range(0, c1 - c0, LANES)]

    for half in range(2):
        tiles = proj_rope(C_Q + 256 * half, C_Q + 256 * (half + 1))
        for c, t in enumerate(tiles):
            col = 256 * half + LANES * c
            q_ref[:, col:col + LANES] = (t * (HEAD_DIM ** -0.5 * LOG2E)).astype(BF16)
    for half in range(2):
        tiles = proj_rope(C_IQ + 256 * half, C_IQ + 256 * (half + 1))
        for c, t in enumerate(tiles):
            col = 256 * half + LANES * c
            iq_ref[:, col:col + LANES] = t.astype(BF16)
    tiles = proj_rope(C_IK2, C_K)
    ik2_ref[:, 0:LANES] = tiles[0].astype(BF16)
    ik2_ref[:, LANES:2 * LANES] = tiles[1].astype(BF16)
    ikt_ref[0] = tiles[0].T[0:D_IDX]
    k_tile = proj_rope(C_K, C_V)[0]
    kb_ref[...] = k_tile.astype(BF16)
    kt_ref[0] = k_tile.T
    vtok_s[...] = _dot(xn, w_ref[:, C_V:C_XP])
    vt = vtok_s[...].T
    vt_ref[0] = vt
    pad = jnp.concatenate([jnp.ones((1, tm), F32), jnp.zeros((V_ROWS - HEAD_DIM - 1, tm), F32)], axis=0)
    for h in range(N_KV_HEADS):
        vtb_ref[0, h * V_ROWS:h * V_ROWS + HEAD_DIM, :] = vt[h * HEAD_DIM:(h + 1) * HEAD_DIM].astype(BF16)
        vtb_ref[0, h * V_ROWS + HEAD_DIM:(h + 1) * V_ROWS, :] = pad.astype(BF16)
    iwt_ref[0] = _dot_nt(wiwt_ref[...], xn) * (N_IDX_HEADS ** -0.5) * (D_IDX ** -0.5)

    xp = _dot(xn, w_ref[:, C_XP:C_END])
    row = lax.broadcasted_iota(I32, (tm, 1), 0)
    if decode:
        ns = tm // SUBLANES
        ext_s[:, 0:16, :] = pre_ref[...]
        ext_s[:, 16:24, :] = xp.reshape(ns, SUBLANES, 4 * D_POOL_GROUP)
        pos = past_len + (row % SUBLANES)
    else:
        @pl.when(j == 0)
        def _():
            ext_s[0:16, :] = jnp.zeros((16, 4 * D_POOL_GROUP), F32)
        ext_s[16:16 + tm, :] = xp
        pos = j * tm + row
    for g, w in enumerate(POOL_WINDOWS):
        cs = slice(g * D_POOL_GROUP, (g + 1) * D_POOL_GROUP)
        tok = xp[:, cs]
        acc = tok
        for i in range(1, w):
            if decode:
                acc = acc + ext_s[:, 16 - i:24 - i, cs].reshape(tm, D_POOL_GROUP)
            else:
                acc = acc + ext_s[16 - i:16 - i + tm, cs]
        cnt = jnp.minimum(w, pos + 1).astype(F32)
        d = acc / cnt - tok
        y = _dot(d.astype(BF16), wpg_ref[g]) * psc_ref[:, cs]
        pool_ref[:, cs] = y.astype(BF16)
    if decode:
        xp_ref[...] = xp
    else:
        ext_s[0:16, :] = xp[tm - 16:tm, :]
        xp_ref[0] = xp[tm - 16:tm, :]


def _rope_tables(pos):
    half = HEAD_DIM // 2
    inv = ROPE_THETA ** (-np.arange(half, dtype=np.float64) / half)
    ang = pos.astype(np.float64)[:, None] * inv[None, :]
    cos32, sin32 = np.cos(ang).astype(np.float32), np.sin(ang).astype(np.float32)
    cos = np.tile(cos32, (1, LANES // half))
    sin = np.tile(np.concatenate([-sin32, sin32], axis=1), (1, LANES // HEAD_DIM))
    return jnp.asarray(cos), jnp.asarray(sin)


def _arrange_w_in_even(w):
    q = w[:, 0:512].reshape(-1, N_HEADS, HEAD_DIM)
    q = jnp.stack([q[:, 0:4], q[:, 4:8]], axis=2).reshape(-1, 512)
    k = w[:, 512:640]
    v = w[:, 640:768]
    iq = w[:, 768:1280]
    ik = w[:, 1280:1344]
    iw = w[:, 1344:1352]
    xp = w[:, 1352:1864]
    z = jnp.zeros_like(ik)
    main = jnp.concatenate([q, iq, ik, z, z, ik, k, v, xp], axis=1).astype(BF16)
    return main, iw.T.astype(BF16)


def _even_in(x, g, wmain, wiwt, tabs, wpg, psc, prefix, *, nb, tm, past_len):
    n, d = x.shape
    t = n // nb
    nt = t // tm
    decode = prefix is not None
    cos, sin = tabs
    row = lambda b, j: (b * nt + j, 0)
    tab_row = (lambda b, j: (0, 0)) if decode else (lambda b, j: (j, 0))
    const = lambda b, j: (0, 0)
    feat = lambda b, j: (b, 0, j)
    in_specs = [
        pl.BlockSpec((tm, d), row),
        pl.BlockSpec((1, d), const),
        pl.BlockSpec(wmain.shape, const),
        pl.BlockSpec(wiwt.shape, const),
        pl.BlockSpec((tm, LANES), tab_row),
        pl.BlockSpec((tm, LANES), tab_row),
        pl.BlockSpec(wpg.shape, lambda b, j: (0, 0, 0)),
        pl.BlockSpec((1, 512), const),
    ]
    args = [x, g, wmain, wiwt, cos, sin, wpg, psc]
    if decode:
        ns = tm // SUBLANES
        in_specs.append(pl.BlockSpec((ns, 16, 512), lambda b, j: (b * nt + j, 0, 0)))
        args.append(prefix)
        xp_shape = jax.ShapeDtypeStruct((n, 512), F32)
        xp_spec = pl.BlockSpec((tm, 512), row)
        scratch = [pltpu.VMEM((ns, 24, 512), F32), pltpu.VMEM((tm, D_KV), F32)]
    else:
        xp_shape = jax.ShapeDtypeStruct((nb, 16, 512), F32)
        xp_spec = pl.BlockSpec((1, 16, 512), lambda b, j: (b, 0, 0))
        scratch = [pltpu.VMEM((tm + 16, 512), F32), pltpu.VMEM((tm, D_KV), F32)]
    out_shape = [
        jax.ShapeDtypeStruct((n, 512), BF16),
        jax.ShapeDtypeStruct((n, 512), BF16),
        jax.ShapeDtypeStruct((n, 256), BF16),
        jax.ShapeDtypeStruct((n, 128), BF16),
        jax.ShapeDtypeStruct((nb, 128, t), F32),
        jax.ShapeDtypeStruct((nb, 128, t), F32),
        jax.ShapeDtypeStruct((nb, N_KV_HEADS * V_ROWS, t), BF16),
        jax.ShapeDtypeStruct((nb, D_IDX, t), F32),
        jax.ShapeDtypeStruct((nb, 8, t), F32),
        jax.ShapeDtypeStruct((n, 512), BF16),
        xp_shape,
    ]
    out_specs = [
        pl.BlockSpec((tm, 512), row),
        pl.BlockSpec((tm, 512), row),
        pl.BlockSpec((tm, 256), row),
        pl.BlockSpec((tm, 128), row),
        pl.BlockSpec((1, 128, tm), feat),
        pl.BlockSpec((1, 128, tm), feat),
        pl.BlockSpec((1, N_KV_HEADS * V_ROWS, tm), feat),
        pl.BlockSpec((1, D_IDX, tm), feat),
        pl.BlockSpec((1, 8, tm), feat),
        pl.BlockSpec((tm, 512), row),
        xp_spec,
    ]
    return pl.pallas_call(
        functools.partial(_even_in_kernel, tm=tm, decode=decode, past_len=past_len),
        grid=(nb, nt), in_specs=in_specs, out_specs=out_specs, out_shape=out_shape,
        scratch_shapes=scratch, compiler_params=_cparams(("arbitrary", "arbitrary")),
        name="even_in_decode" if decode else "even_in_prompt",
    )(*args)


F32_MIN_NORMAL = 2.0 ** -126
F32_LOWEST = -3.4028234663852886e38
LOWEST_FINITE_CODE = -2139095040


def _score_for_search(s, valid):
    return jnp.where(valid, jnp.where(jnp.abs(s) < F32_MIN_NORMAL, 0.0, s), -jnp.inf)


def _code_to_float(code):
    b = code ^ ((code >> 31) & 0x7FFFFFFF)
    f = lax.bitcast_convert_type(b, F32)
    tiny = ((b >> 23) & 0xFF) == 0
    return jnp.where(tiny, jnp.where(b > 0, F32_MIN_NORMAL, 0.0), f)


FRACTION_BITS = 24
MAX_TIE_WALK = 4


def _topk_search(count, shape, n_index_bits, topk, first_index=None):
    def bit_step(i, carry):
        code, c_ge = carry
        cand = code + jnp.left_shift(jnp.int32(1), 31 - i)
        cf = _code_to_float(cand)
        c = count(lambda sv, idx: sv >= cf)
        ok = c >= topk
        return jnp.where(ok, cand, code), jnp.where(ok, c, c_ge)

    code, c_ge = lax.fori_loop(0, 32, bit_step, (jnp.full(shape, INT_MIN, I32), jnp.zeros(shape, I32)))
    short = code < LOWEST_FINITE_CODE
    thr = jnp.where(short, -jnp.inf, _code_to_float(code))
    nxt = jnp.where(short, F32_LOWEST, _code_to_float(code + 1))
    ambiguous = jnp.logical_and(c_ge > topk, jnp.logical_not(short))
    width = nxt - thr
    inv_width = jnp.where(jnp.logical_and(width >= F32_MIN_NORMAL, width < jnp.inf), 1.0 / width, 0.0)

    def frac(sv):
        return (sv - thr) * inv_width

    def in_band(sv):
        return jnp.logical_and(sv >= thr, sv < nxt)

    def split_band():
        need = topk - count(lambda sv, idx: sv >= nxt)
        c_pos = count(lambda sv, idx: jnp.logical_and(in_band(sv), frac(sv) > 0.0))

        def bisect_fraction():
            def frac_step(i, m):
                cand = m + jnp.left_shift(jnp.int32(1), FRACTION_BITS - 1 - i)
                cf = cand.astype(F32) * (2.0 ** -FRACTION_BITS)
                c = count(lambda sv, idx: jnp.logical_and(in_band(sv), frac(sv) >= cf))
                return jnp.where(c >= need, cand, m)
            m = lax.fori_loop(0, FRACTION_BITS, frac_step, jnp.zeros(shape, I32))
            r = m.astype(F32) * (2.0 ** -FRACTION_BITS)
            return r, count(lambda sv, idx: jnp.logical_and(in_band(sv), frac(sv) > r))

        any_frac = jnp.max(jnp.where(ambiguous, c_pos, 0)) > 0
        rstar, c_gtr = lax.cond(any_frac, bisect_fraction, lambda: (jnp.zeros(shape, F32), c_pos))
        need_eq = need - c_gtr

        def index_cut(tied):
            def idx_step(i, lo):
                cand = lo + jnp.left_shift(jnp.int32(1), n_index_bits - 1 - i)
                c = count(lambda sv, idx: jnp.logical_and(tied(sv), idx < cand))
                return jnp.where(c < need_eq, cand, lo)
            return lax.fori_loop(0, n_index_bits, idx_step, jnp.zeros(shape, I32))

        def plain_cut():
            tied = lambda sv: sv == thr
            if first_index is None:
                return index_cut(tied)
            most = jnp.max(jnp.where(ambiguous, need_eq, 1))

            def walk():
                def step(j, cut):
                    nxt_cut = first_index(lambda sv, idx: jnp.logical_and(tied(sv), idx > cut))
                    return jnp.where(j < need_eq, nxt_cut, cut)
                return lax.fori_loop(0, most, step, jnp.full(shape, -1, I32))

            return lax.cond(most <= MAX_TIE_WALK, walk, lambda: index_cut(tied))

        cut = lax.cond(any_frac,
                       lambda: index_cut(lambda sv: jnp.logical_and(in_band(sv), frac(sv) == rstar)),
                       plain_cut)
        return jnp.where(ambiguous, rstar, -1.0), jnp.where(ambiguous, cut, take_all), any_frac

    take_all = jnp.where(short, -1, 2 ** 30).astype(I32)
    any_amb = jnp.max(ambiguous.astype(I32)) > 0
    rstar, cut, any_frac = lax.cond(any_amb, split_band,
                                    lambda: (jnp.full(shape, -1.0, F32), take_all, jnp.zeros((), jnp.bool_)))

    def select(sv, idx):
        r = frac(sv)
        taken = jnp.logical_or(r > rstar, jnp.logical_and(r == rstar, idx <= cut))
        return jnp.logical_or(sv >= nxt, jnp.logical_and(in_band(sv), taken))

    def select_no_fraction(sv, idx):
        return jnp.logical_or(sv > thr, jnp.logical_and(sv == thr, idx <= cut))

    return select, select_no_fraction, jnp.maximum(thr, F32_LOWEST), any_amb, any_frac


def _index_bits(n):
    return max(1, int(np.ceil(np.log2(n))))


def _fold_rows(x, op):
    return op(x.reshape(x.shape[0] // SUBLANES, SUBLANES, x.shape[1]), axis=0)


def _dsa_prompt_kernel(q_ref, iq_ref, iwt_ref, kb_ref, vt_ref, ik2_ref, o_ref,
                       keys_s, qm_s, iqs_s, acc_s, *, tk, topk):
    i = pl.program_id(1)
    n_tiles = (i * Q_BLOCK + Q_BLOCK + tk - 1) // tk
    lane = lax.broadcasted_iota(I32, (1, LANES), 1)
    qpos = i * Q_BLOCK + lane
    lo_half = lane < HEAD_DIM
    nq = KV_GROUP * Q_BLOCK

    for jj in range(KV_GROUP):
        qt = q_ref[:, jj * LANES:(jj + 1) * LANES]
        zero = jnp.zeros_like(qt)
        qm_s[0, jj * Q_BLOCK:(jj + 1) * Q_BLOCK, :] = jnp.where(lo_half, qt, zero)
        qm_s[1, jj * Q_BLOCK:(jj + 1) * Q_BLOCK, :] = jnp.where(lo_half, zero, qt)
        iqs_s[jj * Q_BLOCK:(jj + 1) * Q_BLOCK, :] = iq_ref[:, jj * LANES:(jj + 1) * LANES]
    w = iwt_ref[0]

    def score_tile(t, carry):
        r0 = pl.multiple_of(t * tk, tk)
        ik2 = ik2_ref[pl.ds(r0, tk), :]
        iqs = iqs_s[...]
        rel_e = _dot_nt(ik2[:, 0:LANES], iqs)
        rel_o = _dot_nt(ik2[:, LANES:2 * LANES], iqs)
        s = jnp.zeros((tk, LANES), F32)
        for jj in range(4):
            cs = slice(jj * LANES, (jj + 1) * LANES)
            s = s + w[2 * jj:2 * jj + 1, :] * jnp.maximum(rel_e[:, cs], 0.0)
            s = s + w[2 * jj + 1:2 * jj + 2, :] * jnp.maximum(rel_o[:, cs], 0.0)
        kpos = r0 + lax.broadcasted_iota(I32, (tk, 1), 0)
        keys_s[pl.ds(r0, tk), :] = _score_for_search(s, kpos <= qpos)
        return carry

    lax.fori_loop(0, n_tiles, score_tile, 0)

    def count(pred):
        def body(t, acc):
            r0 = pl.multiple_of(t * tk, tk)
            sv = keys_s[pl.ds(r0, tk), :]
            idx = r0 + lax.broadcasted_iota(I32, (tk, 1), 0)
            return acc + _fold_rows(pred(sv, idx).astype(I32), jnp.sum)
        acc = lax.fori_loop(0, n_tiles, body, jnp.zeros((SUBLANES, LANES), I32))
        return acc.sum(axis=0, keepdims=True)

    def first_index(pred):
        far = jnp.int32(2 ** 30)
        def body(t, acc):
            r0 = pl.multiple_of(t * tk, tk)
            idx = r0 + lax.broadcasted_iota(I32, (tk, 1), 0)
            return jnp.minimum(acc, _fold_rows(jnp.where(pred(keys_s[pl.ds(r0, tk), :], idx), idx, far), jnp.min))
        acc = lax.fori_loop(0, n_tiles, body, jnp.full((SUBLANES, LANES), far, I32))
        return acc.min(axis=0, keepdims=True)

    select, select_no_fraction, thr_ge, any_cut, any_frac = _topk_search(
        count, (1, LANES), _index_bits(keys_s.shape[0]), topk, first_index)

    def attend(selector):
        def pv_tile(t, ms):
            r0 = pl.multiple_of(t * tk, tk)
            sv = keys_s[pl.ds(r0, tk), :]
            if selector is None:
                sel = sv >= thr_ge
            else:
                sel = selector(sv, r0 + lax.broadcasted_iota(I32, (tk, 1), 0))
            sel = jnp.concatenate([sel] * KV_GROUP, axis=1)
            kt = kb_ref[pl.ds(r0, tk), :]
            sts = [jnp.where(sel, _dot_nt(kt, qm_s[g]), NEG_BIG) for g in range(N_KV_HEADS)]
            ms_new = []
            for g, st in enumerate(sts):
                m_new = jnp.maximum(ms[g], jnp.max(_fold_rows(st, jnp.max), axis=0, keepdims=True))
                alpha = jnp.exp2(ms[g] - m_new)
                p = jnp.exp2(st - m_new)
                ms_new.append(m_new)
                vt = vt_ref[0, g * V_ROWS:(g + 1) * V_ROWS, pl.ds(r0, tk)]
                acc_s[g] = acc_s[g] * alpha + _dot(vt, p.astype(BF16))
            return tuple(ms_new)

        lax.fori_loop(0, n_tiles, pv_tile, tuple(jnp.full((1, nq), NEG_BIG, F32) for _ in range(N_KV_HEADS)))

    acc_s[...] = jnp.zeros(acc_s.shape, F32)
    lax.cond(any_cut,
             lambda: lax.cond(any_frac, lambda: attend(select), lambda: attend(select_no_fraction)),
             lambda: attend(None))
    heads = []
    for g in range(N_KV_HEADS):
        og = acc_s[g, 0:HEAD_DIM, :] / acc_s[g, HEAD_DIM:HEAD_DIM + 1, :]
        heads += [og[:, jj * Q_BLOCK:(jj + 1) * Q_BLOCK] for jj in range(KV_GROUP)]
    o_ref[...] = jnp.concatenate(heads, axis=0).T.astype(o_ref.dtype)


def _store_heads(o_ref, heads, lo_half):
    for jj in range(N_HEADS // 2):
        a, b = heads[2 * jj], heads[2 * jj + 1]
        if 2 * jj < KV_GROUP:
            tile = jnp.where(lo_half, a, pltpu.roll(b, HEAD_DIM, 1))
        else:
            tile = jnp.where(lo_half, pltpu.roll(a, HEAD_DIM, 1), b)
        o_ref[:, jj * LANES:(jj + 1) * LANES] = tile.astype(o_ref.dtype)


def _dsa_prompt(q, iq, iwt, kb, vtb, ik2, *, nb, tk, topk):
    n = q.shape[0]
    t = n // nb
    nq = t // Q_BLOCK
    assert t % tk == 0
    blk = lambda b, i: (b * nq + i, 0)
    seq = lambda b, i: (b, 0)
    feat = lambda b, i: (b, 0, 0)
    return pl.pallas_call(
        functools.partial(_dsa_prompt_kernel, tk=tk, topk=topk),
        grid=(nb, nq),
        in_specs=[
            pl.BlockSpec((Q_BLOCK, 512), blk),
            pl.BlockSpec((Q_BLOCK, 512), blk),
            pl.BlockSpec((1, 8, Q_BLOCK), lambda b, i: (b, 0, i)),
            pl.BlockSpec((t, 128), seq),
            pl.BlockSpec((1, N_KV_HEADS * V_ROWS, t), feat),
            pl.BlockSpec((t, 256), seq),
        ],
        out_specs=pl.BlockSpec((Q_BLOCK, 512), blk),
        out_shape=jax.ShapeDtypeStruct((n, 512), BF16),
        scratch_shapes=[
            pltpu.VMEM((t, LANES), F32),
            pltpu.VMEM((N_KV_HEADS, KV_GROUP * Q_BLOCK, LANES), BF16),
            pltpu.VMEM((4 * Q_BLOCK, LANES), BF16),
            pltpu.VMEM((N_KV_HEADS, V_ROWS, KV_GROUP * Q_BLOCK), F32),
        ],
        compiler_params=_cparams(("arbitrary", "arbitrary")),
        name="dsa_prompt",
    )(q, iq, iwt, kb, vtb, ik2)


SEQ_GROUP = Q_BLOCK // SUBLANES


def _pad_rows(a):
    return jnp.concatenate([a, jnp.zeros((LANES - a.shape[0], a.shape[1]), a.dtype)], axis=0).astype(BF16)


def _decode_select_kernel(*refs, n_pages, topk, nnew):
    ip = refs[1:1 + n_pages]
    iqd_ref, wcol_ref, inew_ref, bias_ref, keys_s = refs[1 + n_pages:]
    t = pl.program_id(1)
    n_keys = keys_s.shape[1]
    past = n_pages * PAGE_SIZE
    n_tiles = n_keys // LANES
    lane = lax.broadcasted_iota(I32, (1, LANES), 1)
    qrow = lax.broadcasted_iota(I32, (nnew, 1), 0)
    new_valid = jnp.logical_and(lane <= qrow, lane < nnew)

    def head_sum(x):
        return x.reshape(N_IDX_HEADS, nnew, x.shape[1]).sum(axis=0)

    r0 = pl.multiple_of(t * nnew, nnew)
    iqd = iqd_ref[0]
    wcol = wcol_ref[0]
    for p in range(n_pages):
        rel = _dot(iqd, ip[p][0].astype(BF16))
        s = head_sum(wcol * jnp.maximum(rel, 0.0))
        keys_s[pl.ds(r0, nnew), p * LANES:(p + 1) * LANES] = _score_for_search(s, True)
    rel = _dot_nt(iqd, _pad_rows(inew_ref[0]))
    s = head_sum(wcol * jnp.maximum(rel, 0.0))
    keys_s[pl.ds(r0, nnew), past:past + LANES] = _score_for_search(s, new_valid)

    @pl.when(t == SEQ_GROUP - 1)
    def _search():
        def count(pred):
            acc = jnp.zeros((Q_BLOCK, LANES), I32)
            for c in range(n_tiles):
                idx = c * LANES + lane
                acc = acc + pred(keys_s[:, c * LANES:(c + 1) * LANES], idx).astype(I32)
            return acc.sum(axis=1, keepdims=True)
        select, _, _, _, _ = _topk_search(count, (Q_BLOCK, 1), _index_bits(n_keys), topk)
        for c in range(n_tiles):
            sel = select(keys_s[:, c * LANES:(c + 1) * LANES], c * LANES + lane)
            bias_ref[:, c * LANES:(c + 1) * LANES] = jnp.where(sel, 0.0, NEG_BIG)


def _decode_attend_kernel(*refs, n_pages, nnew):
    kp = refs[1:1 + n_pages]
    vp = refs[1 + n_pages:1 + 2 * n_pages]
    qd_ref, knew_ref, vnew_ref, bias_ref, o_ref = refs[1 + 2 * n_pages:]
    past = n_pages * PAGE_SIZE
    lane = lax.broadcasted_iota(I32, (1, LANES), 1)
    bias = jnp.concatenate([bias_ref[...]] * N_HEADS, axis=0)
    qd = qd_ref[0]
    st = jnp.concatenate([_dot(qd, kp[p][0].astype(BF16)) for p in range(n_pages)]
                         + [_dot_nt(qd, _pad_rows(knew_ref[0]))], axis=1) + bias
    m = jnp.max(st, axis=1, keepdims=True)
    pr = jnp.exp2(st - m)
    l = jnp.sum(pr, axis=1, keepdims=True)
    pb = pr.astype(BF16)
    o = _dot(pb[:, past:past + LANES], _pad_rows(vnew_ref[0]))
    for p in range(n_pages):
        o = o + _dot_nt(pb[:, p * LANES:(p + 1) * LANES], vp[p][0].astype(BF16))
    o = o / l
    _store_heads(o_ref, [o[h * nnew:(h + 1) * nnew] for h in range(N_HEADS)], lane < HEAD_DIM)


def _dsa_decode(page_table, ci_t, ck_t, cv_t, iqd, wcol, inew, qd, knew, vnew, *, topk):
    ns, n_pages = page_table.shape
    nnew = knew.shape[1]
    assert ns % SEQ_GROUP == 0 and nnew == SUBLANES
    n_keys = (n_pages + 1) * PAGE_SIZE

    sel_page = lambda p: pl.BlockSpec((1, D_IDX, PAGE_SIZE),
                                      lambda g, t, pt, p=p: (pt[g * SEQ_GROUP + t, p], 0, 0))
    sel3 = lambda g, t, pt: (g * SEQ_GROUP + t, 0, 0)
    bias = pl.pallas_call(
        functools.partial(_decode_select_kernel, n_pages=n_pages, topk=topk, nnew=nnew),
        grid_spec=pltpu.PrefetchScalarGridSpec(
            num_scalar_prefetch=1, grid=(ns // SEQ_GROUP, SEQ_GROUP),
            in_specs=([sel_page(p) for p in range(n_pages)]
                      + [pl.BlockSpec((1, N_IDX_HEADS * nnew, D_IDX), sel3),
                         pl.BlockSpec((1, N_IDX_HEADS * nnew, 1), sel3),
                         pl.BlockSpec((1, nnew, D_IDX), sel3)]),
            out_specs=pl.BlockSpec((Q_BLOCK, n_keys), lambda g, t, pt: (g, 0)),
            scratch_shapes=[pltpu.VMEM((Q_BLOCK, n_keys), F32)]),
        out_shape=jax.ShapeDtypeStruct((ns * nnew, n_keys), F32),
        compiler_params=_cparams(("arbitrary", "arbitrary")),
        name="dsa_decode_select",
    )(page_table, *([ci_t] * n_pages), iqd, wcol, inew)

    att_page = lambda p: pl.BlockSpec((1, D_KV, PAGE_SIZE), lambda s, pt, p=p: (pt[s, p], 0, 0))
    att3 = lambda s, pt: (s, 0, 0)
    return pl.pallas_call(
        functools.partial(_decode_attend_kernel, n_pages=n_pages, nnew=nnew),
        grid_spec=pltpu.PrefetchScalarGridSpec(
            num_scalar_prefetch=1, grid=(ns,),
            in_specs=([att_page(p) for p in range(n_pages)] + [att_page(p) for p in range(n_pages)]
                      + [pl.BlockSpec((1, N_HEADS * nnew, D_KV), att3),
                         pl.BlockSpec((1, nnew, D_KV), att3),
                         pl.BlockSpec((1, nnew, D_KV), att3),
                         pl.BlockSpec((nnew, n_keys), lambda s, pt: (s, 0))]),
            out_specs=pl.BlockSpec((nnew, D_ATTN), lambda s, pt: (s, 0))),
        out_shape=jax.ShapeDtypeStruct((ns * nnew, D_ATTN), F32),
        compiler_params=_cparams(("arbitrary",)),
        name="dsa_decode_attend",
    )(page_table, *([ck_t] * n_pages), *([cv_t] * n_pages), qd, knew, vnew, bias)


FF_CHUNK = 256


def _ffn_block(h, gpre, gpost, wgu_ref, wd_ref, a_s):
    d_ff = wd_ref.shape[0]
    hn = _rms(h, gpre).astype(BF16)
    for c in range(0, d_ff, FF_CHUNK):
        gate = _dot(hn, wgu_ref[:, c:c + FF_CHUNK])
        up = _dot(hn, wgu_ref[:, d_ff + c:d_ff + c + FF_CHUNK])
        a_s[:, c:c + FF_CHUNK] = (gate * jax.nn.sigmoid(gate) * up).astype(BF16)
    return h + _rms(_dot(a_s[...], wd_ref[...]), gpost)


def _even_out_kernel(x_ref, attn_ref, pool_ref, wo_ref, gmix_ref, gpre_ref, gpost_ref, wgu_ref, wd_ref,
                     o_ref, a_s):
    mix = _dot(attn_ref[...], wo_ref[0:D_ATTN, :]) + _dot(pool_ref[...], wo_ref[D_ATTN:, :])
    h = x_ref[...] + _rms(mix, gmix_ref[...])
    o_ref[...] = _ffn_block(h, gpre_ref[...], gpost_ref[...], wgu_ref, wd_ref, a_s)


def _even_out(x, attn, pool, wo, gmix, gpre, gpost, wgu, wd, *, tm):
    n, d = x.shape
    row = lambda i: (i, 0)
    const = lambda i: (0, 0)
    return pl.pallas_call(
        _even_out_kernel, grid=(n // tm,),
        in_specs=[pl.BlockSpec((tm, d), row), pl.BlockSpec((tm, 512), row), pl.BlockSpec((tm, 512), row),
                  pl.BlockSpec(wo.shape, const), pl.BlockSpec((1, d), const), pl.BlockSpec((1, d), const),
                  pl.BlockSpec((1, d), const), pl.BlockSpec(wgu.shape, const), pl.BlockSpec(wd.shape, const)],
        out_specs=pl.BlockSpec((tm, d), row),
        out_shape=jax.ShapeDtypeStruct((n, d), F32),
        scratch_shapes=[pltpu.VMEM((tm, wd.shape[0]), BF16)],
        compiler_params=_cparams(("arbitrary",)),
        name="even_out_ffn",
    )(x, attn, pool, wo, gmix, gpre, gpost, wgu, wd)


def _odd_kernel(x_ref, g_ref, win_ref, lng_ref, lnb_ref, ws_ref, bs_ref, wout_ref, gmix_ref,
                gpre_ref, gpost_ref, wgu_ref, wd_ref, o_ref, z_ref, y_s, a_s, *, tm, seq_len):
    d_sgu = wout_ref.shape[0]
    dh = d_sgu // N_SGU_HEADS
    x = x_ref[...]
    xn = _rms(x, g_ref[...]).astype(BF16)
    r = lax.broadcasted_iota(I32, (CHUNK, CHUNK), 0)
    c = lax.broadcasted_iota(I32, (CHUNK, CHUNK), 1)
    causal = jnp.logical_and(r // seq_len == c // seq_len, c <= r)
    for hd in range(N_SGU_HEADS):
        cs = slice(hd * dh, (hd + 1) * dh)
        u = _gelu(_dot(xn, win_ref[:, cs]))
        y_s[:, cs] = u.astype(BF16)
    vs = []
    for hd in range(N_SGU_HEADS):
        cs = slice(d_sgu + hd * dh, d_sgu + (hd + 1) * dh)
        vs.append(_gelu(_dot(xn, win_ref[:, cs])))
    v = jnp.concatenate(vs, axis=1)
    mu = jnp.mean(v, axis=-1, keepdims=True)
    vc = v - mu
    z = vc * lax.rsqrt(jnp.mean(vc * vc, axis=-1, keepdims=True) + EPS) * lng_ref[...] + lnb_ref[...]
    z_ref[...] = z
    zb = z.astype(BF16)
    for hd in range(N_SGU_HEADS):
        cs = slice(hd * dh, (hd + 1) * dh)
        wm = jnp.where(causal, ws_ref[hd], 0.0).astype(BF16)
        bias = bs_ref[hd]
        for ch in range(tm // CHUNK):
            rs = slice(ch * CHUNK, (ch + 1) * CHUNK)
            s = _dot(wm, zb[rs, cs]) + bias
            y_s[rs, cs] = (y_s[rs, cs].astype(F32) * s).astype(BF16)
    h = x + _rms(_dot(y_s[...], wout_ref[...]), gmix_ref[...])
    o_ref[...] = _ffn_block(h, gpre_ref[...], gpost_ref[...], wgu_ref, wd_ref, a_s)


def _odd(x, g, win, lng, lnb, ws, bs, wout, gmix, gpre, gpost, wgu, wd, *, tm, seq_len):
    n, d = x.shape
    d_sgu = wout.shape[0]
    row = lambda i: (i, 0)
    const = lambda i: (0, 0)
    const3 = lambda i: (0, 0, 0)
    return pl.pallas_call(
        functools.partial(_odd_kernel, tm=tm, seq_len=seq_len), grid=(n // tm,),
        in_specs=[pl.BlockSpec((tm, d), row), pl.BlockSpec((1, d), const), pl.BlockSpec(win.shape, const),
                  pl.BlockSpec((1, d_sgu), const), pl.BlockSpec((1, d_sgu), const),
                  pl.BlockSpec(ws.shape, const3), pl.BlockSpec(bs.shape, const3),
                  pl.BlockSpec(wout.shape, const), pl.BlockSpec((1, d), const), pl.BlockSpec((1, d), const),
                  pl.BlockSpec((1, d), const), pl.BlockSpec(wgu.shape, const), pl.BlockSpec(wd.shape, const)],
        out_specs=[pl.BlockSpec((tm, d), row), pl.BlockSpec((tm, d_sgu), row)],
        out_shape=[jax.ShapeDtypeStruct((n, d), F32), jax.ShapeDtypeStruct((n, d_sgu), F32)],
        scratch_shapes=[pltpu.VMEM((tm, d_sgu), BF16), pltpu.VMEM((tm, wd.shape[0]), BF16)],
        compiler_params=_cparams(("arbitrary",)),
        name="odd_mixer_ffn",
    )(x, g, win, lng, lnb, ws, bs, wout, gmix, gpre, gpost, wgu, wd)


PROMPT_ROW_TILE = 512
DECODE_ROW_TILE = 256
PROMPT_KEY_TILE = 512


def _tile_rows(n, want):
    tm = min(want, n)
    while n % tm:
        tm //= 2
    return tm


def _from_feature_major(a, heads):
    b, f, t = a.shape
    if heads is None:
        return jnp.transpose(a, (0, 2, 1))
    return jnp.transpose(a.reshape(b, heads, f // heads, t), (0, 3, 1, 2))


def kernel(x_prompt, x_sample, cache_k, cache_v, cache_idx_k, state_pool, page_table, norm_mix_pre, norm_mix_post, norm_ffn_pre, norm_ffn_post, w_in_even, w_out_even, w_pool_group, pool_scale, w_in_odd, sgu_norm_g, sgu_norm_b, w_spatial, b_spatial, w_out_odd, w_ffn_gate_up, w_ffn_down):
    bp, tp, d = x_prompt.shape
    bs, ts, _ = x_sample.shape
    n_pages = page_table.shape[1]
    past_len = n_pages * PAGE_SIZE
    topk_p = min(TOPK_MAX, tp // 4)
    topk_s = min(TOPK_MAX, (past_len + ts) // 4)
    depth = norm_mix_pre.shape[0]
    assert ts == SUBLANES and tp % Q_BLOCK == 0 and d % LANES == 0

    hp = x_prompt.reshape(bp * tp, d)
    hs = x_sample.reshape(bs * ts, d)
    tm_p = _tile_rows(bp * tp, PROMPT_ROW_TILE)
    tm_p_seq = _tile_rows(tp, 2 * PROMPT_ROW_TILE)
    tm_s = _tile_rows(bs * ts, 2 * DECODE_ROW_TILE)
    row = lambda a: a.reshape(1, -1)

    tabs_p = _rope_tables(np.arange(tp))
    tabs_s = _rope_tables(np.tile(past_len + np.arange(ts), tm_s // ts))

    outs_p, outs_s, sgu_s = [], [], []
    for layer in range(depth):
        li = layer // 2
        gpre, gpost = row(norm_ffn_pre[layer]), row(norm_ffn_post[layer])
        wgu = w_ffn_gate_up[layer].astype(BF16)
        wd = w_ffn_down[layer].astype(BF16)
        gmix_pre, gmix_post = row(norm_mix_pre[layer]), row(norm_mix_post[layer])
        if layer % 2 == 0:
            wmain, wiwt = _arrange_w_in_even(w_in_even[li])
            wpg = w_pool_group[li].astype(BF16)
            psc = row(pool_scale[li])
            wo = w_out_even[li].astype(BF16)
            (q, iq, ik2, kb, kt, vt, vtb, ikt, iwt, pool, xp_tail) = _even_in(
                hp, gmix_pre, wmain, wiwt, tabs_p, wpg, psc, None, nb=bp, tm=tm_p_seq, past_len=0)
            attn = _dsa_prompt(q, iq, iwt, kb, vtb, ik2, nb=bp, tk=min(PROMPT_KEY_TILE, tp), topk=topk_p)
            hp = _even_out(hp, attn, pool, wo, gmix_post, gpre, gpost, wgu, wd, tm=tm_p)
            outs_p.append((_from_feature_major(kt, N_KV_HEADS), _from_feature_major(vt, N_KV_HEADS),
                           _from_feature_major(ikt, None), xp_tail[:, 1:, :]))
            prefix = jnp.pad(state_pool[li], ((0, 0), (1, 0), (0, 0)))
            (q, iq, ik2, kb, kt, vt, vtb, ikt, iwt, pool, xp) = _even_in(
                hs, gmix_pre, wmain, wiwt, tabs_s, wpg, psc, prefix, nb=1, tm=tm_s, past_len=past_len)
            del kb, vtb
            k_new = _from_feature_major(kt.reshape(1, D_KV, bs * ts), None).reshape(bs, ts, D_KV)
            v_new = _from_feature_major(vt.reshape(1, D_KV, bs * ts), None).reshape(bs, ts, D_KV)
            ik_new = _from_feature_major(ikt, None).reshape(bs, ts, D_IDX)
            q3 = q.reshape(bs, ts, 4, 2, HEAD_DIM)
            zq = jnp.zeros_like(q3[:, :, :, 0])
            qg = jnp.concatenate([jnp.concatenate([q3[:, :, :, 0], zq], axis=-1),
                                  jnp.concatenate([zq, q3[:, :, :, 1]], axis=-1)], axis=2)
            qd = jnp.transpose(qg, (0, 2, 1, 3)).reshape(bs, N_HEADS * ts, D_KV)
            iqd = jnp.transpose(iq.reshape(bs, ts, N_IDX_HEADS, D_IDX), (0, 2, 1, 3)).reshape(bs, N_IDX_HEADS * ts, D_IDX)
            wcol = jnp.transpose(iwt[0].reshape(N_IDX_HEADS, bs, ts), (1, 0, 2)).reshape(bs, N_IDX_HEADS * ts, 1)
            ci_t = jnp.transpose(cache_idx_k[li], (0, 2, 1))
            ck_t = jnp.transpose(cache_k[li], (0, 2, 3, 1)).reshape(-1, D_KV, PAGE_SIZE)
            cv_t = jnp.transpose(cache_v[li], (0, 2, 3, 1)).reshape(-1, D_KV, PAGE_SIZE)
            attn_s = _dsa_decode(page_table, ci_t, ck_t, cv_t, iqd, wcol, ik_new, qd, k_new, v_new, topk=topk_s)
            hs = _even_out(hs, attn_s.astype(BF16), pool, wo, gmix_post, gpre, gpost, wgu, wd, tm=tm_s)
            new_pool_s = jnp.concatenate([state_pool[li][:, ts:, :], xp.reshape(bs, ts, -1)], axis=1)
            outs_s.append((k_new.reshape(bs, ts, N_KV_HEADS, HEAD_DIM), v_new.reshape(bs, ts, N_KV_HEADS, HEAD_DIM),
                           ik_new, new_pool_s))
        else:
            win = w_in_odd[li].astype(BF16)
            wout = w_out_odd[li].astype(BF16)
            lng, lnb = row(sgu_norm_g[li]), row(sgu_norm_b[li])
            ws_p = w_spatial[li][:, :CHUNK, :CHUNK]
            bs_p = b_spatial[li][:, :CHUNK, None]
            hp, _ = _odd(hp, gmix_pre, win, lng, lnb, ws_p, bs_p, wout, gmix_post, gpre, gpost, wgu, wd,
                         tm=tm_p, seq_len=CHUNK)
            reps = CHUNK // ts
            ws_s = jnp.tile(w_spatial[li][:, :ts, :ts], (1, reps, reps))
            bs_s = jnp.tile(b_spatial[li][:, :ts], (1, reps))[:, :, None]
            hs, zs = _odd(hs, gmix_pre, win, lng, lnb, ws_s, bs_s, wout, gmix_post, gpre, gpost, wgu, wd,
                          tm=tm_s, seq_len=ts)
            sgu_s.append(zs.reshape(bs, ts, -1))

    return (hp.reshape(bp, tp, d), hs.reshape(bs, ts, d),
            jnp.stack([o[0] for o in outs_p]), jnp.stack([o[1] for o in outs_p]),
            jnp.stack([o[2] for o in outs_p]), jnp.stack([o[3] for o in outs_p]),
            jnp.stack([o[0] for o in outs_s]), jnp.stack([o[1] for o in outs_s]),
            jnp.stack([o[2] for o in outs_s]), jnp.stack([o[3] for o in outs_s]),
            jnp.stack(sgu_s))
```

```python
import functools

import jax
import jax.numpy as jnp
import numpy as np
from jax import lax
from jax.experimental import pallas as pl
from jax.experimental.pallas import tpu as pltpu

EPS = 1e-6
N_HEADS = 8
HEAD_DIM = 64
N_KV_HEADS = 2
KV_GROUP = N_HEADS // N_KV_HEADS
D_ATTN = N_HEADS * HEAD_DIM
D_KV = N_KV_HEADS * HEAD_DIM
N_IDX_HEADS = 8
D_IDX = 64
TOPK_MAX = 256
Q_BLOCK = 128
ROPE_THETA = 10000.0
POOL_WINDOWS = (2, 4, 8, 16)
POOL_BUF = 15
PAGE_SIZE = 128
CHUNK = 128
N_SGU_HEADS = 8
D_POOL_GROUP = 128
V_ROWS = HEAD_DIM + 16

LANES = 128
SUBLANES = 8
VMEM_LIMIT = 56 * 1024 * 1024
INT_MIN = -2 ** 31
NEG_BIG = -1e30
LOG2E = 1.4426950408889634

F32 = jnp.float32
BF16 = jnp.bfloat16
I32 = jnp.int32

_NT = (((1,), (1,)), ((), ()))


def _cparams(sem):
    return pltpu.CompilerParams(dimension_semantics=sem, vmem_limit_bytes=VMEM_LIMIT)


def _rms(x, g):
    return x * lax.rsqrt(jnp.mean(x * x, axis=-1, keepdims=True) + EPS) * g


def _dot(a, b):
    return jnp.dot(a, b, preferred_element_type=F32)


def _dot_nt(a, b):
    return lax.dot_general(a, b, _NT, preferred_element_type=F32)


def _gelu(x):
    return 0.5 * x * (1.0 + lax.erf(x * (2.0 ** -0.5)))


C_Q, C_IQ, C_IK2, C_K, C_V, C_XP, C_END = 0, 512, 1024, 1280, 1408, 1536, 2048


def _rope_tile(t, cos, sin, first_half):
    partner = jnp.where(first_half, pltpu.roll(t, 96, 1), pltpu.roll(t, 32, 1))
    return t * cos + partner * sin


def _even_in_kernel(*refs, tm, decode, past_len):
    if decode:
        (x_ref, g_ref, w_ref, wiwt_ref, cos_ref, sin_ref, wpg_ref, psc_ref, pre_ref,
         q_ref, iq_ref, ik2_ref, kb_ref, kt_ref, vt_ref, vtb_ref, ikt_ref, iwt_ref, pool_ref, xp_ref,
         ext_s, vtok_s) = refs
    else:
        (x_ref, g_ref, w_ref, wiwt_ref, cos_ref, sin_ref, wpg_ref, psc_ref,
         q_ref, iq_ref, ik2_ref, kb_ref, kt_ref, vt_ref, vtb_ref, ikt_ref, iwt_ref, pool_ref, xp_ref,
         ext_s, vtok_s) = refs
    j = pl.program_id(1)
    xn = _rms(x_ref[...], g_ref[...]).astype(BF16)
    cos = cos_ref[...]
    sin = sin_ref[...]
    lane = lax.broadcasted_iota(I32, (1, LANES), 1)
    first_half = (lane % HEAD_DIM) < (HEAD_DIM // 2)

    def proj_rope(c0, c1):
        t = _dot(xn, w_ref[:, c0:c1])
        return [_rope_tile(t[:, c:c + LANES], cos, sin, first_half) for c in range(0, c1 - c0, LANES)]

    for half in range(2):
        tiles = proj_rope(C_Q + 256 * half, C_Q + 256 * (half + 1))
        for c, t in enumerate(tiles):
            col = 256 * half + LANES * c
            q_ref[:, col:col + LANES] = (t * (HEAD_DIM ** -0.5 * LOG2E)).astype(BF16)
    for half in range(2):
        tiles = proj_rope(C_IQ + 256 * half, C_IQ + 256 * (half + 1))
        for c, t in enumerate(tiles):
            col = 256 * half + LANES * c
            iq_ref[:, col:col + LANES] = t.astype(BF16)
    tiles = proj_rope(C_IK2, C_K)
    ik2_ref[:, 0:LANES] = tiles[0].astype(BF16)
    ik2_ref[:, LANES:2 * LANES] = tiles[1].astype(BF16)
    ikt_ref[0] = tiles[0].T[0:D_IDX]
    k_tile = proj_rope(C_K, C_V)[0]
    kb_ref[...] = k_tile.astype(BF16)
    kt_ref[0] = k_tile.T
    vtok_s[...] = _dot(xn, w_ref[:, C_V:C_XP])
    vt = vtok_s[...].T
    vt_ref[0] = vt
    pad = jnp.concatenate([jnp.ones((1, tm), F32), jnp.zeros((V_ROWS - HEAD_DIM - 1, tm), F32)], axis=0)
    for h in range(N_KV_HEADS):
        vtb_ref[0, h * V_ROWS:h * V_ROWS + HEAD_DIM, :] = vt[h * HEAD_DIM:(h + 1) * HEAD_DIM].astype(BF16)
        vtb_ref[0, h * V_ROWS + HEAD_DIM:(h + 1) * V_ROWS, :] = pad.astype(BF16)
    iwt_ref[0] = _dot_nt(wiwt_ref[...], xn) * (N_IDX_HEADS ** -0.5) * (D_IDX ** -0.5)

    xp = _dot(xn, w_ref[:, C_XP:C_END])
    row = lax.broadcasted_iota(I32, (tm, 1), 0)
    if decode:
        ns = tm // SUBLANES
        ext_s[:, 0:16, :] = pre_ref[...]
        ext_s[:, 16:24, :] = xp.reshape(ns, SUBLANES, 4 * D_POOL_GROUP)
        pos = past_len + (row % SUBLANES)
    else:
        @pl.when(j == 0)
        def _():
            ext_s[0:16, :] = jnp.zeros((16, 4 * D_POOL_GROUP), F32)
        ext_s[16:16 + tm, :] = xp
        pos = j * tm + row
    for g, w in enumerate(POOL_WINDOWS):
        cs = slice(g * D_POOL_GROUP, (g + 1) * D_POOL_GROUP)
        tok = xp[:, cs]
        acc = tok
        for i in range(1, w):
            if decode:
                acc = acc + ext_s[:, 16 - i:24 - i, cs].reshape(tm, D_POOL_GROUP)
            else:
                acc = acc + ext_s[16 - i:16 - i + tm, cs]
        cnt = jnp.minimum(w, pos + 1).astype(F32)
        d = acc / cnt - tok
        y = _dot(d.astype(BF16), wpg_ref[g]) * psc_ref[:, cs]
        pool_ref[:, cs] = y.astype(BF16)
    if decode:
        xp_ref[...] = xp
    else:
        ext_s[0:16, :] = xp[tm - 16:tm, :]
        xp_ref[0] = xp[tm - 16:tm, :]


def _rope_tables(pos):
    half = HEAD_DIM // 2
    inv = ROPE_THETA ** (-np.arange(half, dtype=np.float64) / half)
    ang = pos.astype(np.float64)[:, None] * inv[None, :]
    cos32, sin32 = np.cos(ang).astype(np.float32), np.sin(ang).astype(np.float32)
    cos = np.tile(cos32, (1, LANES // half))
    sin = np.tile(np.concatenate([-sin32, sin32], axis=1), (1, LANES // HEAD_DIM))
    return jnp.asarray(cos), jnp.asarray(sin)


def _arrange_w_in_even(w):
    q = w[:, 0:512].reshape(-1, N_HEADS, HEAD_DIM)
    q = jnp.stack([q[:, 0:4], q[:, 4:8]], axis=2).reshape(-1, 512)
    k = w[:, 512:640]
    v = w[:, 640:768]
    iq = w[:, 768:1280]
    ik = w[:, 1280:1344]
    iw = w[:, 1344:1352]
    xp = w[:, 1352:1864]
    z = jnp.zeros_like(ik)
    main = jnp.concatenate([q, iq, ik, z, z, ik, k, v, xp], axis=1).astype(BF16)
    return main, iw.T.astype(BF16)


def _even_in(x, g, wmain, wiwt, tabs, wpg, psc, prefix, *, nb, tm, past_len):
    n, d = x.shape
    t = n // nb
    nt = t // tm
    decode = prefix is not None
    cos, sin = tabs
    row = lambda b, j: (b * nt + j, 0)
    tab_row = (lambda b, j: (0, 0)) if decode else (lambda b, j: (j, 0))
    const = lambda b, j: (0, 0)
    feat = lambda b, j: (b, 0, j)
    in_specs = [
        pl.BlockSpec((tm, d), row),
        pl.BlockSpec((1, d), const),
        pl.BlockSpec(wmain.shape, const),
        pl.BlockSpec(wiwt.shape, const),
        pl.BlockSpec((tm, LANES), tab_row),
        pl.BlockSpec((tm, LANES), tab_row),
        pl.BlockSpec(wpg.shape, lambda b, j: (0, 0, 0)),
        pl.BlockSpec((1, 512), const),
    ]
    args = [x, g, wmain, wiwt, cos, sin, wpg, psc]
    if decode:
        ns = tm // SUBLANES
        in_specs.append(pl.BlockSpec((ns, 16, 512), lambda b, j: (b * nt + j, 0, 0)))
        args.append(prefix)
        xp_shape = jax.ShapeDtypeStruct((n, 512), F32)
        xp_spec = pl.BlockSpec((tm, 512), row)
        scratch = [pltpu.VMEM((ns, 24, 512), F32), pltpu.VMEM((tm, D_KV), F32)]
    else:
        xp_shape = jax.ShapeDtypeStruct((nb, 16, 512), F32)
        xp_spec = pl.BlockSpec((1, 16, 512), lambda b, j: (b, 0, 0))
        scratch = [pltpu.VMEM((tm + 16, 512), F32), pltpu.VMEM((tm, D_KV), F32)]
    out_shape = [
        jax.ShapeDtypeStruct((n, 512), BF16),
        jax.ShapeDtypeStruct((n, 512), BF16),
        jax.ShapeDtypeStruct((n, 256), BF16),
        jax.ShapeDtypeStruct((n, 128), BF16),
        jax.ShapeDtypeStruct((nb, 128, t), F32),
        jax.ShapeDtypeStruct((nb, 128, t), F32),
        jax.ShapeDtypeStruct((nb, N_KV_HEADS * V_ROWS, t), BF16),
        jax.ShapeDtypeStruct((nb, D_IDX, t), F32),
        jax.ShapeDtypeStruct((nb, 8, t), F32),
        jax.ShapeDtypeStruct((n, 512), BF16),
        xp_shape,
    ]
    out_specs = [
        pl.BlockSpec((tm, 512), row),
        pl.BlockSpec((tm, 512), row),
        pl.BlockSpec((tm, 256), row),
        pl.BlockSpec((tm, 128), row),
        pl.BlockSpec((1, 128, tm), feat),
        pl.BlockSpec((1, 128, tm), feat),
        pl.BlockSpec((1, N_KV_HEADS * V_ROWS, tm), feat),
        pl.BlockSpec((1, D_IDX, tm), feat),
        pl.BlockSpec((1, 8, tm), feat),
        pl.BlockSpec((tm, 512), row),
        xp_spec,
    ]
    return pl.pallas_call(
        functools.partial(_even_in_kernel, tm=tm, decode=decode, past_len=past_len),
        grid=(nb, nt), in_specs=in_specs, out_specs=out_specs, out_shape=out_shape,
        scratch_shapes=scratch, compiler_params=_cparams(("arbitrary", "arbitrary")),
        name="even_in_decode" if decode else "even_in_prompt",
    )(*args)


F32_MIN_NORMAL = 2.0 ** -126
F32_LOWEST = -3.4028234663852886e38
LOWEST_FINITE_CODE = -2139095040


def _score_for_search(s, valid):
    return jnp.where(valid, jnp.where(jnp.abs(s) < F32_MIN_NORMAL, 0.0, s), -jnp.inf)


def _code_to_float(code):
    b = code ^ ((code >> 31) & 0x7FFFFFFF)
    f = lax.bitcast_convert_type(b, F32)
    tiny = ((b >> 23) & 0xFF) == 0
    return jnp.where(tiny, jnp.where(b > 0, F32_MIN_NORMAL, 0.0), f)


FRACTION_BITS = 24
MAX_TIE_WALK = 4


def _topk_search(count, shape, n_index_bits, topk, first_index=None):
    def bit_step(i, carry):
        code, c_ge = carry
        cand = code + jnp.left_shift(jnp.int32(1), 31 - i)
        cf = _code_to_float(cand)
        c = count(lambda sv, idx: sv >= cf)
        ok = c >= topk
        return jnp.where(ok, cand, code), jnp.where(ok, c, c_ge)

    code, c_ge = lax.fori_loop(0, 32, bit_step, (jnp.full(shape, INT_MIN, I32), jnp.zeros(shape, I32)))
    short = code < LOWEST_FINITE_CODE
    thr = jnp.where(short, -jnp.inf, _code_to_float(code))
    nxt = jnp.where(short, F32_LOWEST, _code_to_float(code + 1))
    ambiguous = jnp.logical_and(c_ge > topk, jnp.logical_not(short))
    width = nxt - thr
    inv_width = jnp.where(jnp.logical_and(width >= F32_MIN_NORMAL, width < jnp.inf), 1.0 / width, 0.0)

    def frac(sv):
        return (sv - thr) * inv_width

    def in_band(sv):
        return jnp.logical_and(sv >= thr, sv < nxt)

    def split_band():
        need = topk - count(lambda sv, idx: sv >= nxt)
        c_pos = count(lambda sv, idx: jnp.logical_and(in_band(sv), frac(sv) > 0.0))

        def bisect_fraction():
            def frac_step(i, m):
                cand = m + jnp.left_shift(jnp.int32(1), FRACTION_BITS - 1 - i)
                cf = cand.astype(F32) * (2.0 ** -FRACTION_BITS)
                c = count(lambda sv, idx: jnp.logical_and(in_band(sv), frac(sv) >= cf))
                return jnp.where(c >= need, cand, m)
            m = lax.fori_loop(0, FRACTION_BITS, frac_step, jnp.zeros(shape, I32))
            r = m.astype(F32) * (2.0 ** -FRACTION_BITS)
            return r, count(lambda sv, idx: jnp.logical_and(in_band(sv), frac(sv) > r))

        any_frac = jnp.max(jnp.where(ambiguous, c_pos, 0)) > 0
        rstar, c_gtr = lax.cond(any_frac, bisect_fraction, lambda: (jnp.zeros(shape, F32), c_pos))
        need_eq = need - c_gtr

        def index_cut(tied):
            def idx_step(i, lo):
                cand = lo + jnp.left_shift(jnp.int32(1), n_index_bits - 1 - i)
                c = count(lambda sv, idx: jnp.logical_and(tied(sv), idx < cand))
                return jnp.where(c < need_eq, cand, lo)
            return lax.fori_loop(0, n_index_bits, idx_step, jnp.zeros(shape, I32))

        def plain_cut():
            tied = lambda sv: sv == thr
            if first_index is None:
                return index_cut(tied)
            most = jnp.max(jnp.where(ambiguous, need_eq, 1))

            def walk():
                def step(j, cut):
                    nxt_cut = first_index(lambda sv, idx: jnp.logical_and(tied(sv), idx > cut))
                    return jnp.where(j < need_eq, nxt_cut, cut)
                return lax.fori_loop(0, most, step, jnp.full(shape, -1, I32))

            return lax.cond(most <= MAX_TIE_WALK, walk, lambda: index_cut(tied))

        cut = lax.cond(any_frac,
                       lambda: index_cut(lambda sv: jnp.logical_and(in_band(sv), frac(sv) == rstar)),
                       plain_cut)
        return jnp.where(ambiguous, rstar, -1.0), jnp.where(ambiguous, cut, take_all), any_frac

    take_all = jnp.where(short, -1, 2 ** 30).astype(I32)
    any_amb = jnp.max(ambiguous.astype(I32)) > 0
    rstar, cut, any_frac = lax.cond(any_amb, split_band,
                                    lambda: (jnp.full(shape, -1.0, F32), take_all, jnp.zeros((), jnp.bool_)))

    def select(sv, idx):
        r = frac(sv)
        taken = jnp.logical_or(r > rstar, jnp.logical_and(r == rstar, idx <= cut))
        return jnp.logical_or(sv >= nxt, jnp.logical_and(in_band(sv), taken))

    def select_no_fraction(sv, idx):
        return jnp.logical_or(sv > thr, jnp.logical_and(sv == thr, idx <= cut))

    return select, select_no_fraction, jnp.maximum(thr, F32_LOWEST), any_amb, any_frac


def _index_bits(n):
    return max(1, int(np.ceil(np.log2(n))))


def _fold_rows(x, op):
    return op(x.reshape(x.shape[0] // SUBLANES, SUBLANES, x.shape[1]), axis=0)


def _dsa_prompt_kernel(q_ref, iq_ref, iwt_ref, kb_ref, vt_ref, ik2_ref, o_ref,
                       keys_s, qm_s, iqs_s, acc_s, *, tk, topk):
    i = pl.program_id(1)
    n_tiles = (i * Q_BLOCK + Q_BLOCK + tk - 1) // tk
    lane = lax.broadcasted_iota(I32, (1, LANES), 1)
    qpos = i * Q_BLOCK + lane
    lo_half = lane < HEAD_DIM
    nq = KV_GROUP * Q_BLOCK

    for jj in range(KV_GROUP):
        qt = q_ref[:, jj * LANES:(jj + 1) * LANES]
        zero = jnp.zeros_like(qt)
        qm_s[0, jj * Q_BLOCK:(jj + 1) * Q_BLOCK, :] = jnp.where(lo_half, qt, zero)
        qm_s[1, jj * Q_BLOCK:(jj + 1) * Q_BLOCK, :] = jnp.where(lo_half, zero, qt)
        iqs_s[jj * Q_BLOCK:(jj + 1) * Q_BLOCK, :] = iq_ref[:, jj * LANES:(jj + 1) * LANES]
    w = iwt_ref[0]

    sub = min(tk, 2 * LANES)

    def score_tile(t, carry):
        iqs = iqs_s[...]
        for u in range(tk // sub):
            r0 = pl.multiple_of(t * tk + u * sub, sub)
            ik2 = ik2_ref[pl.ds(r0, sub), :]
            rel_e = _dot_nt(ik2[:, 0:LANES], iqs)
            rel_o = _dot_nt(ik2[:, LANES:2 * LANES], iqs)
            s = jnp.zeros((sub, LANES), F32)
            for jj in range(4):
                cs = slice(jj * LANES, (jj + 1) * LANES)
                s = s + w[2 * jj:2 * jj + 1, :] * jnp.maximum(rel_e[:, cs], 0.0)
                s = s + w[2 * jj + 1:2 * jj + 2, :] * jnp.maximum(rel_o[:, cs], 0.0)
            kpos = r0 + lax.broadcasted_iota(I32, (sub, 1), 0)
            keys_s[pl.ds(r0, sub), :] = _score_for_search(s, kpos <= qpos)
        return carry

    lax.fori_loop(0, n_tiles, score_tile, 0)

    def count(pred):
        def body(t, acc):
            r0 = pl.multiple_of(t * tk, tk)
            sv = keys_s[pl.ds(r0, tk), :]
            idx = r0 + lax.broadcasted_iota(I32, (tk, 1), 0)
            return acc + _fold_rows(pred(sv, idx).astype(I32), jnp.sum)
        acc = lax.fori_loop(0, n_tiles, body, jnp.zeros((SUBLANES, LANES), I32))
        return acc.sum(axis=0, keepdims=True)

    def first_index(pred):
        far = jnp.int32(2 ** 30)
        def body(t, acc):
            r0 = pl.multiple_of(t * tk, tk)
            idx = r0 + lax.broadcasted_iota(I32, (tk, 1), 0)
            return jnp.minimum(acc, _fold_rows(jnp.where(pred(keys_s[pl.ds(r0, tk), :], idx), idx, far), jnp.min))
        acc = lax.fori_loop(0, n_tiles, body, jnp.full((SUBLANES, LANES), far, I32))
        return acc.min(axis=0, keepdims=True)

    select, select_no_fraction, thr_ge, any_cut, any_frac = _topk_search(
        count, (1, LANES), _index_bits(keys_s.shape[0]), topk, first_index)

    def attend(selector):
        def pv_tile(t, ms):
            r0 = pl.multiple_of(t * tk, tk)
            sv = keys_s[pl.ds(r0, tk), :]
            if selector is None:
                sel = sv >= thr_ge
            else:
                sel = selector(sv, r0 + lax.broadcasted_iota(I32, (tk, 1), 0))
            sel = jnp.concatenate([sel] * KV_GROUP, axis=1)
            kt = kb_ref[pl.ds(r0, tk), :]
            sts = [jnp.where(sel, _dot_nt(kt, qm_s[g]), NEG_BIG) for g in range(N_KV_HEADS)]
            ms_new = []
            for g, st in enumerate(sts):
                m_new = jnp.maximum(ms[g], jnp.max(_fold_rows(st, jnp.max), axis=0, keepdims=True))
                alpha = jnp.exp2(ms[g] - m_new)
                p = jnp.exp2(st - m_new)
                ms_new.append(m_new)
                vt = vt_ref[0, g * V_ROWS:(g + 1) * V_ROWS, pl.ds(r0, tk)]
                acc_s[g] = acc_s[g] * alpha + _dot(vt, p.astype(BF16))
            return tuple(ms_new)

        lax.fori_loop(0, n_tiles, pv_tile, tuple(jnp.full((1, nq), NEG_BIG, F32) for _ in range(N_KV_HEADS)))

    acc_s[...] = jnp.zeros(acc_s.shape, F32)
    lax.cond(any_cut,
             lambda: lax.cond(any_frac, lambda: attend(select), lambda: attend(select_no_fraction)),
             lambda: attend(None))
    heads = []
    for g in range(N_KV_HEADS):
        og = acc_s[g, 0:HEAD_DIM, :] / acc_s[g, HEAD_DIM:HEAD_DIM + 1, :]
        heads += [og[:, jj * Q_BLOCK:(jj + 1) * Q_BLOCK] for jj in range(KV_GROUP)]
    o_ref[...] = jnp.concatenate(heads, axis=0).T.astype(o_ref.dtype)


def _store_heads(o_ref, heads, lo_half):
    for jj in range(N_HEADS // 2):
        a, b = heads[2 * jj], heads[2 * jj + 1]
        if 2 * jj < KV_GROUP:
            tile = jnp.where(lo_half, a, pltpu.roll(b, HEAD_DIM, 1))
        else:
            tile = jnp.where(lo_half, pltpu.roll(a, HEAD_DIM, 1), b)
        o_ref[:, jj * LANES:(jj + 1) * LANES] = tile.astype(o_ref.dtype)


def _dsa_prompt(q, iq, iwt, kb, vtb, ik2, *, nb, tk, topk):
    n = q.shape[0]
    t = n // nb
    nq = t // Q_BLOCK
    assert t % tk == 0
    blk = lambda b, i: (b * nq + i, 0)
    seq = lambda b, i: (b, 0)
    feat = lambda b, i: (b, 0, 0)
    return pl.pallas_call(
        functools.partial(_dsa_prompt_kernel, tk=tk, topk=topk),
        grid=(nb, nq),
        in_specs=[
            pl.BlockSpec((Q_BLOCK, 512), blk),
            pl.BlockSpec((Q_BLOCK, 512), blk),
            pl.BlockSpec((1, 8, Q_BLOCK), lambda b, i: (b, 0, i)),
            pl.BlockSpec((t, 128), seq),
            pl.BlockSpec((1, N_KV_HEADS * V_ROWS, t), feat),
            pl.BlockSpec((t, 256), seq),
        ],
        out_specs=pl.BlockSpec((Q_BLOCK, 512), blk),
        out_shape=jax.ShapeDtypeStruct((n, 512), BF16),
        scratch_shapes=[
            pltpu.VMEM((t, LANES), F32),
            pltpu.VMEM((N_KV_HEADS, KV_GROUP * Q_BLOCK, LANES), BF16),
            pltpu.VMEM((4 * Q_BLOCK, LANES), BF16),
            pltpu.VMEM((N_KV_HEADS, V_ROWS, KV_GROUP * Q_BLOCK), F32),
        ],
        compiler_params=_cparams(("arbitrary", "arbitrary")),
        name="dsa_prompt",
    )(q, iq, iwt, kb, vtb, ik2)


SEQ_GROUP = Q_BLOCK // SUBLANES


def _pad_rows(a):
    return jnp.concatenate([a, jnp.zeros((LANES - a.shape[0], a.shape[1]), a.dtype)], axis=0).astype(BF16)


def _decode_select_kernel(*refs, n_pages, topk, nnew):
    ip = refs[1:1 + n_pages]
    iqd_ref, wcol_ref, inew_ref, bias_ref, keys_s = refs[1 + n_pages:]
    t = pl.program_id(1)
    n_keys = keys_s.shape[1]
    past = n_pages * PAGE_SIZE
    n_tiles = n_keys // LANES
    lane = lax.broadcasted_iota(I32, (1, LANES), 1)
    qrow = lax.broadcasted_iota(I32, (nnew, 1), 0)
    new_valid = jnp.logical_and(lane <= qrow, lane < nnew)

    def head_sum(x):
        return x.reshape(N_IDX_HEADS, nnew, x.shape[1]).sum(axis=0)

    r0 = pl.multiple_of(t * nnew, nnew)
    iqd = iqd_ref[0]
    wcol = wcol_ref[0]
    for p in range(n_pages):
        rel = _dot(iqd, ip[p][0].astype(BF16))
        s = head_sum(wcol * jnp.maximum(rel, 0.0))
        keys_s[pl.ds(r0, nnew), p * LANES:(p + 1) * LANES] = _score_for_search(s, True)
    rel = _dot_nt(iqd, _pad_rows(inew_ref[0]))
    s = head_sum(wcol * jnp.maximum(rel, 0.0))
    keys_s[pl.ds(r0, nnew), past:past + LANES] = _score_for_search(s, new_valid)

    @pl.when(t == SEQ_GROUP - 1)
    def _search():
        def count(pred):
            acc = jnp.zeros((Q_BLOCK, LANES), I32)
            for c in range(n_tiles):
                idx = c * LANES + lane
                acc = acc + pred(keys_s[:, c * LANES:(c + 1) * LANES], idx).astype(I32)
            return acc.sum(axis=1, keepdims=True)
        select, _, _, _, _ = _topk_search(count, (Q_BLOCK, 1), _index_bits(n_keys), topk)
        for c in range(n_tiles):
            sel = select(keys_s[:, c * LANES:(c + 1) * LANES], c * LANES + lane)
            bias_ref[:, c * LANES:(c + 1) * LANES] = jnp.where(sel, 0.0, NEG_BIG)


def _decode_attend_kernel(*refs, n_pages, nnew):
    kp = refs[1:1 + n_pages]
    vp = refs[1 + n_pages:1 + 2 * n_pages]
    qd_ref, knew_ref, vnew_ref, bias_ref, o_ref = refs[1 + 2 * n_pages:]
    past = n_pages * PAGE_SIZE
    lane = lax.broadcasted_iota(I32, (1, LANES), 1)
    bias = jnp.concatenate([bias_ref[...]] * N_HEADS, axis=0)
    qd = qd_ref[0]
    st = jnp.concatenate([_dot(qd, kp[p][0].astype(BF16)) for p in range(n_pages)]
                         + [_dot_nt(qd, _pad_rows(knew_ref[0]))], axis=1) + bias
    m = jnp.max(st, axis=1, keepdims=True)
    pr = jnp.exp2(st - m)
    l = jnp.sum(pr, axis=1, keepdims=True)
    pb = pr.astype(BF16)
    o = _dot(pb[:, past:past + LANES], _pad_rows(vnew_ref[0]))
    for p in range(n_pages):
        o = o + _dot_nt(pb[:, p * LANES:(p + 1) * LANES], vp[p][0].astype(BF16))
    o = o / l
    _store_heads(o_ref, [o[h * nnew:(h + 1) * nnew] for h in range(N_HEADS)], lane < HEAD_DIM)


def _dsa_decode(page_table, ci_t, ck_t, cv_t, iqd, wcol, inew, qd, knew, vnew, *, topk):
    ns, n_pages = page_table.shape
    nnew = knew.shape[1]
    assert ns % SEQ_GROUP == 0 and nnew == SUBLANES
    n_keys = (n_pages + 1) * PAGE_SIZE

    sel_page = lambda p: pl.BlockSpec((1, D_IDX, PAGE_SIZE),
                                      lambda g, t, pt, p=p: (pt[g * SEQ_GROUP + t, p], 0, 0))
    sel3 = lambda g, t, pt: (g * SEQ_GROUP + t, 0, 0)
    bias = pl.pallas_call(
        functools.partial(_decode_select_kernel, n_pages=n_pages, topk=topk, nnew=nnew),
        grid_spec=pltpu.PrefetchScalarGridSpec(
            num_scalar_prefetch=1, grid=(ns // SEQ_GROUP, SEQ_GROUP),
            in_specs=([sel_page(p) for p in range(n_pages)]
                      + [pl.BlockSpec((1, N_IDX_HEADS * nnew, D_IDX), sel3),
                         pl.BlockSpec((1, N_IDX_HEADS * nnew, 1), sel3),
                         pl.BlockSpec((1, nnew, D_IDX), sel3)]),
            out_specs=pl.BlockSpec((Q_BLOCK, n_keys), lambda g, t, pt: (g, 0)),
            scratch_shapes=[pltpu.VMEM((Q_BLOCK, n_keys), F32)]),
        out_shape=jax.ShapeDtypeStruct((ns * nnew, n_keys), F32),
        compiler_params=_cparams(("arbitrary", "arbitrary")),
        name="dsa_decode_select",
    )(page_table, *([ci_t] * n_pages), iqd, wcol, inew)

    att_page = lambda p: pl.BlockSpec((1, D_KV, PAGE_SIZE), lambda s, pt, p=p: (pt[s, p], 0, 0))
    att3 = lambda s, pt: (s, 0, 0)
    return pl.pallas_call(
        functools.partial(_decode_attend_kernel, n_pages=n_pages, nnew=nnew),
        grid_spec=pltpu.PrefetchScalarGridSpec(
            num_scalar_prefetch=1, grid=(ns,),
            in_specs=([att_page(p) for p in range(n_pages)] + [att_page(p) for p in range(n_pages)]
                      + [pl.BlockSpec((1, N_HEADS * nnew, D_KV), att3),
                         pl.BlockSpec((1, nnew, D_KV), att3),
                         pl.BlockSpec((1, nnew, D_KV), att3),
                         pl.BlockSpec((nnew, n_keys), lambda s, pt: (s, 0))]),
            out_specs=pl.BlockSpec((nnew, D_ATTN), lambda s, pt: (s, 0))),
        out_shape=jax.ShapeDtypeStruct((ns * nnew, D_ATTN), F32),
        compiler_params=_cparams(("arbitrary",)),
        name="dsa_decode_attend",
    )(page_table, *([ck_t] * n_pages), *([cv_t] * n_pages), qd, knew, vnew, bias)


FF_CHUNK = 256


def _ffn_block(h, gpre, gpost, wgu_ref, wd_ref, a_s):
    d_ff = wd_ref.shape[0]
    hn = _rms(h, gpre).astype(BF16)
    for c in range(0, d_ff, FF_CHUNK):
        gate = _dot(hn, wgu_ref[:, c:c + FF_CHUNK])
        up = _dot(hn, wgu_ref[:, d_ff + c:d_ff + c + FF_CHUNK])
        a_s[:, c:c + FF_CHUNK] = (gate * jax.nn.sigmoid(gate) * up).astype(BF16)
    return h + _rms(_dot(a_s[...], wd_ref[...]), gpost)


def _even_out_kernel(x_ref, attn_ref, pool_ref, wo_ref, gmix_ref, gpre_ref, gpost_ref, wgu_ref, wd_ref,
                     o_ref, a_s):
    mix = _dot(attn_ref[...], wo_ref[0:D_ATTN, :]) + _dot(pool_ref[...], wo_ref[D_ATTN:, :])
    h = x_ref[...] + _rms(mix, gmix_ref[...])
    o_ref[...] = _ffn_block(h, gpre_ref[...], gpost_ref[...], wgu_ref, wd_ref, a_s)


def _even_out(x, attn, pool, wo, gmix, gpre, gpost, wgu, wd, *, tm):
    n, d = x.shape
    row = lambda i: (i, 0)
    const = lambda i: (0, 0)
    return pl.pallas_call(
        _even_out_kernel, grid=(n // tm,),
        in_specs=[pl.BlockSpec((tm, d), row), pl.BlockSpec((tm, 512), row), pl.BlockSpec((tm, 512), row),
                  pl.BlockSpec(wo.shape, const), pl.BlockSpec((1, d), const), pl.BlockSpec((1, d), const),
                  pl.BlockSpec((1, d), const), pl.BlockSpec(wgu.shape, const), pl.BlockSpec(wd.shape, const)],
        out_specs=pl.BlockSpec((tm, d), row),
        out_shape=jax.ShapeDtypeStruct((n, d), F32),
        scratch_shapes=[pltpu.VMEM((tm, wd.shape[0]), BF16)],
        compiler_params=_cparams(("arbitrary",)),
        name="even_out_ffn",
    )(x, attn, pool, wo, gmix, gpre, gpost, wgu, wd)


def _odd_kernel(x_ref, g_ref, win_ref, lng_ref, lnb_ref, ws_ref, bs_ref, wout_ref, gmix_ref,
                gpre_ref, gpost_ref, wgu_ref, wd_ref, o_ref, z_ref, y_s, a_s, *, tm, seq_len):
    d_sgu = wout_ref.shape[0]
    dh = d_sgu // N_SGU_HEADS
    x = x_ref[...]
    xn = _rms(x, g_ref[...]).astype(BF16)
    r = lax.broadcasted_iota(I32, (CHUNK, CHUNK), 0)
    c = lax.broadcasted_iota(I32, (CHUNK, CHUNK), 1)
    causal = jnp.logical_and(r // seq_len == c // seq_len, c <= r)
    for hd in range(N_SGU_HEADS):
        cs = slice(hd * dh, (hd + 1) * dh)
        u = _gelu(_dot(xn, win_ref[:, cs]))
        y_s[:, cs] = u.astype(BF16)
    vs = []
    for hd in range(N_SGU_HEADS):
        cs = slice(d_sgu + hd * dh, d_sgu + (hd + 1) * dh)
        vs.append(_gelu(_dot(xn, win_ref[:, cs])))
    v = jnp.concatenate(vs, axis=1)
    mu = jnp.mean(v, axis=-1, keepdims=True)
    vc = v - mu
    z = vc * lax.rsqrt(jnp.mean(vc * vc, axis=-1, keepdims=True) + EPS) * lng_ref[...] + lnb_ref[...]
    z_ref[...] = z
    zb = z.astype(BF16)
    for hd in range(N_SGU_HEADS):
        cs = slice(hd * dh, (hd + 1) * dh)
        wm = jnp.where(causal, ws_ref[hd], 0.0).astype(BF16)
        bias = bs_ref[hd]
        for ch in range(tm // CHUNK):
            rs = slice(ch * CHUNK, (ch + 1) * CHUNK)
            s = _dot(wm, zb[rs, cs]) + bias
            y_s[rs, cs] = (y_s[rs, cs].astype(F32) * s).astype(BF16)
    h = x + _rms(_dot(y_s[...], wout_ref[...]), gmix_ref[...])
    o_ref[...] = _ffn_block(h, gpre_ref[...], gpost_ref[...], wgu_ref, wd_ref, a_s)


def _odd(x, g, win, lng, lnb, ws, bs, wout, gmix, gpre, gpost, wgu, wd, *, tm, seq_len):
    n, d = x.shape
    d_sgu = wout.shape[0]
    row = lambda i: (i, 0)
    const = lambda i: (0, 0)
    const3 = lambda i: (0, 0, 0)
    return pl.pallas_call(
        functools.partial(_odd_kernel, tm=tm, seq_len=seq_len), grid=(n // tm,),
        in_specs=[pl.BlockSpec((tm, d), row), pl.BlockSpec((1, d), const), pl.BlockSpec(win.shape, const),
                  pl.BlockSpec((1, d_sgu), const), pl.BlockSpec((1, d_sgu), const),
                  pl.BlockSpec(ws.shape, const3), pl.BlockSpec(bs.shape, const3),
                  pl.BlockSpec(wout.shape, const), pl.BlockSpec((1, d), const), pl.BlockSpec((1, d), const),
                  pl.BlockSpec((1, d), const), pl.BlockSpec(wgu.shape, const), pl.BlockSpec(wd.shape, const)],
        out_specs=[pl.BlockSpec((tm, d), row), pl.BlockSpec((tm, d_sgu), row)],
        out_shape=[jax.ShapeDtypeStruct((n, d), F32), jax.ShapeDtypeStruct((n, d_sgu), F32)],
        scratch_shapes=[pltpu.VMEM((tm, d_sgu), BF16), pltpu.VMEM((tm, wd.shape[0]), BF16)],
        compiler_params=_cparams(("arbitrary",)),
        name="odd_mixer_ffn",
    )(x, g, win, lng, lnb, ws, bs, wout, gmix, gpre, gpost, wgu, wd)


PROMPT_ROW_TILE = 512
DECODE_ROW_TILE = 256
PROMPT_KEY_TILE = 512


def _tile_rows(n, want):
    tm = min(want, n)
    while n % tm:
        tm //= 2
    return tm


def _from_feature_major(a, heads):
    b, f, t = a.shape
    if heads is None:
        return jnp.transpose(a, (0, 2, 1))
    return jnp.transpose(a.reshape(b, heads, f // heads, t), (0, 3, 1, 2))


def kernel(x_prompt, x_sample, cache_k, cache_v, cache_idx_k, state_pool, page_table, norm_mix_pre, norm_mix_post, norm_ffn_pre, norm_ffn_post, w_in_even, w_out_even, w_pool_group, pool_scale, w_in_odd, sgu_norm_g, sgu_norm_b, w_spatial, b_spatial, w_out_odd, w_ffn_gate_up, w_ffn_down):
    bp, tp, d = x_prompt.shape
    bs, ts, _ = x_sample.shape
    n_pages = page_table.shape[1]
    past_len = n_pages * PAGE_SIZE
    topk_p = min(TOPK_MAX, tp // 4)
    topk_s = min(TOPK_MAX, (past_len + ts) // 4)
    depth = norm_mix_pre.shape[0]
    assert ts == SUBLANES and tp % Q_BLOCK == 0 and d % LANES == 0

    hp = x_prompt.reshape(bp * tp, d)
    hs = x_sample.reshape(bs * ts, d)
    tm_p = _tile_rows(bp * tp, PROMPT_ROW_TILE)
    tm_p_seq = _tile_rows(tp, PROMPT_ROW_TILE)
    tm_s = _tile_rows(bs * ts, DECODE_ROW_TILE)
    row = lambda a: a.reshape(1, -1)

    tabs_p = _rope_tables(np.arange(tp))
    tabs_s = _rope_tables(np.tile(past_len + np.arange(ts), tm_s // ts))

    outs_p, outs_s, sgu_s = [], [], []
    for layer in range(depth):
        li = layer // 2
        gpre, gpost = row(norm_ffn_pre[layer]), row(norm_ffn_post[layer])
        wgu = w_ffn_gate_up[layer].astype(BF16)
        wd = w_ffn_down[layer].astype(BF16)
        gmix_pre, gmix_post = row(norm_mix_pre[layer]), row(norm_mix_post[layer])
        if layer % 2 == 0:
            wmain, wiwt = _arrange_w_in_even(w_in_even[li])
            wpg = w_pool_group[li].astype(BF16)
            psc = row(pool_scale[li])
            wo = w_out_even[li].astype(BF16)
            (q, iq, ik2, kb, kt, vt, vtb, ikt, iwt, pool, xp_tail) = _even_in(
                hp, gmix_pre, wmain, wiwt, tabs_p, wpg, psc, None, nb=bp, tm=tm_p_seq, past_len=0)
            attn = _dsa_prompt(q, iq, iwt, kb, vtb, ik2, nb=bp, tk=min(PROMPT_KEY_TILE, tp), topk=topk_p)
            hp = _even_out(hp, attn, pool, wo, gmix_post, gpre, gpost, wgu, wd, tm=tm_p)
            outs_p.append((_from_feature_major(kt, N_KV_HEADS), _from_feature_major(vt, N_KV_HEADS),
                           _from_feature_major(ikt, None), xp_tail[:, 1:, :]))
            prefix = jnp.pad(state_pool[li], ((0, 0), (1, 0), (0, 0)))
            (q, iq, ik2, kb, kt, vt, vtb, ikt, iwt, pool, xp) = _even_in(
                hs, gmix_pre, wmain, wiwt, tabs_s, wpg, psc, prefix, nb=1, tm=tm_s, past_len=past_len)
            del kb, vtb
            k_new = _from_feature_major(kt.reshape(1, D_KV, bs * ts), None).reshape(bs, ts, D_KV)
            v_new = _from_feature_major(vt.reshape(1, D_KV, bs * ts), None).reshape(bs, ts, D_KV)
            ik_new = _from_feature_major(ikt, None).reshape(bs, ts, D_IDX)
            q3 = q.reshape(bs, ts, 4, 2, HEAD_DIM)
            zq = jnp.zeros_like(q3[:, :, :, 0])
            qg = jnp.concatenate([jnp.concatenate([q3[:, :, :, 0], zq], axis=-1),
                                  jnp.concatenate([zq, q3[:, :, :, 1]], axis=-1)], axis=2)
            qd = jnp.transpose(qg, (0, 2, 1, 3)).reshape(bs, N_HEADS * ts, D_KV)
            iqd = jnp.transpose(iq.reshape(bs, ts, N_IDX_HEADS, D_IDX), (0, 2, 1, 3)).reshape(bs, N_IDX_HEADS * ts, D_IDX)
            wcol = jnp.transpose(iwt[0].reshape(N_IDX_HEADS, bs, ts), (1, 0, 2)).reshape(bs, N_IDX_HEADS * ts, 1)
            ci_t = jnp.transpose(cache_idx_k[li], (0, 2, 1))
            ck_t = jnp.transpose(cache_k[li], (0, 2, 3, 1)).reshape(-1, D_KV, PAGE_SIZE)
            cv_t = jnp.transpose(cache_v[li], (0, 2, 3, 1)).reshape(-1, D_KV, PAGE_SIZE)
            attn_s = _dsa_decode(page_table, ci_t, ck_t, cv_t, iqd, wcol, ik_new, qd, k_new, v_new, topk=topk_s)
            hs = _even_out(hs, attn_s.astype(BF16), pool, wo, gmix_post, gpre, gpost, wgu, wd, tm=tm_s)
            new_pool_s = jnp.concatenate([state_pool[li][:, ts:, :], xp.reshape(bs, ts, -1)], axis=1)
            outs_s.append((k_new.reshape(bs, ts, N_KV_HEADS, HEAD_DIM), v_new.reshape(bs, ts, N_KV_HEADS, HEAD_DIM),
                           ik_new, new_pool_s))
        else:
            win = w_in_odd[li].astype(BF16)
            wout = w_out_odd[li].astype(BF16)
            lng, lnb = row(sgu_norm_g[li]), row(sgu_norm_b[li])
            ws_p = w_spatial[li][:, :CHUNK, :CHUNK]
            bs_p = b_spatial[li][:, :CHUNK, None]
            hp, _ = _odd(hp, gmix_pre, win, lng, lnb, ws_p, bs_p, wout, gmix_post, gpre, gpost, wgu, wd,
                         tm=tm_p, seq_len=CHUNK)
            reps = CHUNK // ts
            ws_s = jnp.tile(w_spatial[li][:, :ts, :ts], (1, reps, reps))
            bs_s = jnp.tile(b_spatial[li][:, :ts], (1, reps))[:, :, None]
            hs, zs = _odd(hs, gmix_pre, win, lng, lnb, ws_s, bs_s, wout, gmix_post, gpre, gpost, wgu, wd,
                          tm=tm_s, seq_len=ts)
            sgu_s.append(zs.reshape(bs, ts, -1))

    return (hp.reshape(bp, tp, d), hs.reshape(bs, ts, d),
            jnp.stack([o[0] for o in outs_p]), jnp.stack([o[1] for o in outs_p]),
            jnp.stack([o[2] for o in outs_p]), jnp.stack([o[3] for o in outs_p]),
            jnp.stack([o[0] for o in outs_s]), jnp.stack([o[1] for o in outs_s]),
            jnp.stack([o[2] for o in outs_s]), jnp.stack([o[3] for o in outs_s]),
            jnp.stack(sgu_s))
```
